```python
import math
import jax, jax.numpy as jnp
from jax import lax
import numpy as np

D_MODEL = 1024
BATCH = 8
SEQ = 2048
DEPTH = 2

BRANCH_WIDTH = D_MODEL // 2
N_BRANCHES = 4
CHUNK = 64
CONV_WIDTH = 4
NORM_EPS = 1e-6

HGRN_HEADS = 4
HGRN_EXPAND = BRANCH_WIDTH // HGRN_HEADS
HGRN_FDIM = HGRN_HEADS * HGRN_EXPAND
HGRN_VDIM = BRANCH_WIDTH
HGRN_VHEAD = HGRN_VDIM // HGRN_HEADS
SSD_DINNER = BRANCH_WIDTH
SSD_HEADDIM = 64
SSD_HEADS = SSD_DINNER // SSD_HEADDIM
SSD_GROUPS = 2
SSD_HPG = SSD_HEADS // SSD_GROUPS
SSD_STATE = 64
SSD_CONV_DIM = SSD_DINNER + 2 * SSD_GROUPS * SSD_STATE
GLA_HEADS = 4
GLA_KDIM = BRANCH_WIDTH // 2
GLA_VDIM = BRANCH_WIDTH
GLA_HEAD_K = GLA_KDIM // GLA_HEADS
GLA_HEAD_V = GLA_VDIM // GLA_HEADS
GLA_GATE_RANK = 16
GLA_GATE_NORMALIZER = 16.0
LRU_WIDTH = BRANCH_WIDTH
LRU_BLOCKS = 8
LRU_BLOCK = LRU_WIDTH // LRU_BLOCKS
LRU_C = 8.0
D_FF = ((8 * D_MODEL // 3 + 255) // 256) * 256

IN_SPLITS = (
    HGRN_FDIM, HGRN_FDIM, HGRN_VDIM, HGRN_VDIM,
    SSD_DINNER, SSD_CONV_DIM, SSD_HEADS,
    GLA_KDIM, GLA_KDIM, GLA_VDIM, GLA_VDIM, GLA_GATE_RANK,
    LRU_WIDTH, LRU_WIDTH,
    N_BRANCHES * D_MODEL,
)
D_IN = sum(IN_SPLITS)

kernel_name = "hybrid_hgrn2_ssd_gla_rglru_block"

F32 = jnp.float32


def rmsnorm(x, w):
    xf = x.astype(F32)
    y = xf * lax.rsqrt(jnp.mean(xf * xf, axis=-1, keepdims=True) + NORM_EPS)
    return (y * w.astype(F32)).astype(x.dtype)


def causal_dwconv(x, w, b):
    c = x.shape[-1]
    y = lax.conv_general_dilated(
        x, w[:, None, :].astype(x.dtype), window_strides=(1,),
        padding=[(w.shape[0] - 1, 0)], dimension_numbers=("NWC", "WIO", "NWC"),
        feature_group_count=c)
    return y + b.astype(x.dtype)


def chunk_gated_linear_attn(q, k, v, log_f):
    bsz, seqlen, nh, dk = q.shape
    dv = v.shape[-1]
    n = seqlen // CHUNK

    def to_chunks(t):
        return jnp.moveaxis(t.astype(F32).reshape(bsz, n, CHUNK, nh, t.shape[-1]), 1, 0)

    qc, kc, vc, gc = to_chunks(q), to_chunks(k), to_chunks(v), to_chunks(log_f)
    causal = jnp.tril(jnp.ones((CHUNK, CHUNK), bool))

    def step(state, inp):
        qi, ki, vi, gi = inp
        bcum = jnp.cumsum(gi, axis=1)
        rel = bcum[:, :, None] - bcum[:, None, :]
        rel = jnp.where(causal[None, :, :, None, None], rel, -jnp.inf)
        scores = jnp.einsum('bihd,bjhd,bijhd->bhij', qi, ki, jnp.exp(rel))
        o_intra = jnp.einsum('bhij,bjhv->bihv', scores, vi)
        o_inter = jnp.einsum('bihd,bhdv->bihv', qi * jnp.exp(bcum), state)
        b_last = bcum[:, -1:]
        k_dec = ki * jnp.exp(b_last - bcum)
        new_state = state * jnp.exp(b_last[:, 0])[..., None] + jnp.einsum('bjhd,bjhv->bhdv', k_dec, vi)
        return new_state, o_intra + o_inter

    s0 = jnp.zeros((bsz, nh, dk, dv), F32)
    _, o = lax.scan(step, s0, (qc, kc, vc, gc))
    return jnp.moveaxis(o, 0, 1).reshape(bsz, seqlen, nh, dv)


def hgrn2_branch(q, f_pre, i, g, lb, norm_w):
    bsz, seqlen, _ = q.shape
    lbf = lb.astype(F32)
    log_f = jnp.logaddexp(jnp.log(lbf), jnp.log1p(-lbf) + jax.nn.log_sigmoid(f_pre.astype(F32)))
    key = -jnp.expm1(log_f)
    qf = jax.nn.silu(q.astype(F32))
    hk = lambda t: t.reshape(bsz, seqlen, HGRN_HEADS, HGRN_EXPAND)
    hv = lambda t: t.reshape(bsz, seqlen, HGRN_HEADS, HGRN_VHEAD)
    o = chunk_gated_linear_attn(hk(qf), hk(key), hv(i.astype(F32)), hk(log_f))
    o = rmsnorm(o, norm_w) * jax.nn.silu(hv(g.astype(F32)))
    return o.reshape(bsz, seqlen, HGRN_VDIM)


def gla_branch(q, k, v, g, gate_lr, gate_w, gate_b, norm_w):
    bsz, seqlen, _ = q.shape
    log_a = jax.nn.log_sigmoid(gate_lr.astype(F32) @ gate_w.astype(F32) + gate_b.astype(F32)) / GLA_GATE_NORMALIZER
    hk = lambda t: t.astype(F32).reshape(bsz, seqlen, GLA_HEADS, GLA_HEAD_K)
    hv = lambda t: t.astype(F32).reshape(bsz, seqlen, GLA_HEADS, GLA_HEAD_V)
    o = chunk_gated_linear_attn(hk(q) * (GLA_HEAD_K ** -0.5), hk(k), hv(v), hk(log_a))
    o = rmsnorm(o, norm_w) * jax.nn.silu(hv(g))
    return o.reshape(bsz, seqlen, GLA_VDIM)


def ssd_branch(z, xbc, dt_raw, conv_w, conv_b, dt_bias, a_log, d_skip, norm_w):
    bsz, seqlen, _ = z.shape
    n = seqlen // CHUNK
    xbc = jax.nn.silu(causal_dwconv(xbc, conv_w, conv_b).astype(F32))
    xs, bm, cm = jnp.split(xbc, [SSD_DINNER, SSD_DINNER + SSD_GROUPS * SSD_STATE], axis=-1)
    dt = jax.nn.softplus(dt_raw.astype(F32) + dt_bias.astype(F32))
    a_neg = -jnp.exp(a_log.astype(F32)).reshape(SSD_GROUPS, SSD_HPG)
    x_c = xs.reshape(bsz, n, CHUNK, SSD_GROUPS, SSD_HPG, SSD_HEADDIM)
    dt_c = dt.reshape(bsz, n, CHUNK, SSD_GROUPS, SSD_HPG)
    b_c = bm.reshape(bsz, n, CHUNK, SSD_GROUPS, SSD_STATE)
    c_c = cm.reshape(bsz, n, CHUNK, SSD_GROUPS, SSD_STATE)
    a_cs = jnp.cumsum(dt_c * a_neg, axis=2)
    xdt = x_c * dt_c[..., None]
    causal = jnp.tril(jnp.ones((CHUNK, CHUNK), bool))
    seg = a_cs[:, :, :, None] - a_cs[:, :, None, :]
    lmat = jnp.exp(jnp.where(causal[None, None, :, :, None, None], seg, -jnp.inf))
    y_diag = jnp.einsum('bcigs,bcjgs,bcijgh,bcjghp->bcighp', c_c, b_c, lmat, xdt)
    decay_states = jnp.exp(a_cs[:, :, -1:] - a_cs)
    states = jnp.einsum('bcjgs,bcjgh,bcjghp->bcghps', b_c, decay_states, xdt)
    cs = jnp.cumsum(a_cs[:, :, -1], axis=1)
    rel = cs[:, :, None] - cs[:, None, :]
    ctril = jnp.tril(jnp.ones((n, n), bool))
    carried = jnp.einsum('bzcgh,bcghps->bzghps',
                         jnp.exp(jnp.where(ctril[None, :, :, None, None], rel, -jnp.inf)), states)
    prev = jnp.concatenate([jnp.zeros_like(carried[:, :1]), carried[:, :-1]], axis=1)
    y_off = jnp.einsum('bcigs,bcghps,bcigh->bcighp', c_c, prev, jnp.exp(a_cs))
    y = y_diag + y_off + x_c * d_skip.astype(F32).reshape(SSD_GROUPS, SSD_HPG)[..., None]
    y = y.reshape(bsz, seqlen, SSD_DINNER) * jax.nn.silu(z.astype(F32))
    yg = y.reshape(bsz, seqlen, SSD_GROUPS, SSD_DINNER // SSD_GROUPS)
    yg = yg * lax.rsqrt(jnp.mean(yg * yg, axis=-1, keepdims=True) + NORM_EPS)
    return yg.reshape(bsz, seqlen, SSD_DINNER) * norm_w.astype(F32)


def rglru_branch(xb, gate, conv_w, conv_b, wa, ba, wx, bx, lam):
    bsz, seqlen, _ = xb.shape
    u = causal_dwconv(xb, conv_w, conv_b).astype(F32)
    ub = u.reshape(bsz, seqlen, LRU_BLOCKS, LRU_BLOCK)
    r = jax.nn.sigmoid(jnp.einsum('blkd,kde->blke', ub, wa.astype(F32)).reshape(bsz, seqlen, LRU_WIDTH) + ba.astype(F32))
    i = jax.nn.sigmoid(jnp.einsum('blkd,kde->blke', ub, wx.astype(F32)).reshape(bsz, seqlen, LRU_WIDTH) + bx.astype(F32))
    log_a = -LRU_C * r * jax.nn.softplus(-lam.astype(F32))
    a = jnp.exp(log_a)
    bterm = jnp.sqrt(-jnp.expm1(2.0 * log_a)) * (i * u)

    def combine(e1, e2):
        a1, b1 = e1
        a2, b2 = e2
        return a1 * a2, a2 * b1 + b2

    _, h = lax.associative_scan(combine, (a, bterm), axis=1)
    return h * jax.nn.gelu(gate.astype(F32), approximate=True)


def setup_inputs(seed: int = 0) -> dict:
    key = jax.random.key(seed)
    ks = jax.random.split(key, 32)
    nrm = lambda k, shape, scale: jax.random.normal(k, shape, F32) * scale
    gain = lambda k, shape: 1.0 + 0.02 * jax.random.normal(k, shape, F32)
    dt = jnp.exp(jax.random.uniform(ks[8], (DEPTH, SSD_HEADS), F32, math.log(1e-3), math.log(1e-1)))
    lam_u = jax.random.uniform(ks[19], (DEPTH, LRU_WIDTH), F32, 0.9, 0.999) ** (1.0 / LRU_C)
    return {
        "x": nrm(ks[0], (BATCH, SEQ, D_MODEL), 1.0),
        "norm_mix_w": gain(ks[1], (DEPTH, D_MODEL)),
        "w_in": nrm(ks[2], (DEPTH, D_MODEL, D_IN), D_MODEL ** -0.5),
        "hgrn_lower_bounds": nrm(ks[3], (DEPTH, HGRN_FDIM), 0.1),
        "hgrn_norm_w": gain(ks[4], (DEPTH, HGRN_VHEAD)),
        "ssd_conv_w": nrm(ks[5], (DEPTH, CONV_WIDTH, SSD_CONV_DIM), CONV_WIDTH ** -0.5),
        "ssd_conv_b": nrm(ks[6], (DEPTH, SSD_CONV_DIM), 0.01),
        "ssd_dt_bias": dt + jnp.log(-jnp.expm1(-dt)),
        "ssd_a_log": jnp.log(jax.random.uniform(ks[9], (DEPTH, SSD_HEADS), F32, 1.0, 16.0)),
        "ssd_d": gain(ks[10], (DEPTH, SSD_HEADS)),
        "ssd_norm_w": gain(ks[11], (DEPTH, SSD_DINNER)),
        "gla_gate_w": nrm(ks[12], (DEPTH, GLA_GATE_RANK, GLA_KDIM), GLA_GATE_RANK ** -0.5),
        "gla_gate_b": nrm(ks[13], (DEPTH, GLA_KDIM), 0.01),
        "gla_norm_w": gain(ks[14], (DEPTH, GLA_HEAD_V)),
        "lru_conv_w": nrm(ks[15], (DEPTH, CONV_WIDTH, LRU_WIDTH), CONV_WIDTH ** -0.5),
        "lru_conv_b": nrm(ks[16], (DEPTH, LRU_WIDTH), 0.01),
        "lru_wa": nrm(ks[17], (DEPTH, LRU_BLOCKS, LRU_BLOCK, LRU_BLOCK), LRU_BLOCK ** -0.5),
        "lru_ba": nrm(ks[18], (DEPTH, LRU_WIDTH), 0.01),
        "lru_wx": nrm(ks[20], (DEPTH, LRU_BLOCKS, LRU_BLOCK, LRU_BLOCK), LRU_BLOCK ** -0.5),
        "lru_bx": nrm(ks[21], (DEPTH, LRU_WIDTH), 0.01),
        "lru_lambda": jnp.log(lam_u) - jnp.log1p(-lam_u),
        "w_branch": nrm(ks[22], (DEPTH, N_BRANCHES, BRANCH_WIDTH, D_MODEL), BRANCH_WIDTH ** -0.5),
        "w_out": nrm(ks[23], (DEPTH, D_MODEL, D_MODEL), D_MODEL ** -0.5),
        "norm_ffn_w": gain(ks[24], (DEPTH, D_MODEL)),
        "w_ffn_in": nrm(ks[25], (DEPTH, D_MODEL, 2 * D_FF), D_MODEL ** -0.5),
        "w_ffn_out": nrm(ks[26], (DEPTH, D_FF, D_MODEL), D_FF ** -0.5),
        "norm_f_w": gain(ks[27], (D_MODEL,)),
    }


def reference(x, norm_mix_w, w_in, hgrn_lower_bounds, hgrn_norm_w, ssd_conv_w, ssd_conv_b, ssd_dt_bias,
              ssd_a_log, ssd_d, ssd_norm_w, gla_gate_w, gla_gate_b, gla_norm_w, lru_conv_w, lru_conv_b,
              lru_wa, lru_ba, lru_wx, lru_bx, lru_lambda, w_branch, w_out, norm_ffn_w, w_ffn_in,
              w_ffn_out, norm_f_w):
    bsz, seqlen, _ = x.shape
    split_idx = np.cumsum(IN_SPLITS)[:-1].tolist()
    lb_all = jnp.cumsum(jax.nn.softmax(hgrn_lower_bounds.astype(F32), axis=0), axis=0)
    lb_all = lb_all - lb_all[0:1]
    h = x
    for l in range(DEPTH):
        xn = rmsnorm(h, norm_mix_w[l])
        proj = xn @ w_in[l]
        (hq, hf, hi, hg, sz, sxbc, sdt, gq, gk, gv, gg, glr, lx, lg, mg) = jnp.split(proj, split_idx, axis=-1)
        ya = hgrn2_branch(hq, hf, hi, hg, lb_all[l], hgrn_norm_w[l])
        yb = ssd_branch(sz, sxbc, sdt, ssd_conv_w[l], ssd_conv_b[l], ssd_dt_bias[l], ssd_a_log[l],
                        ssd_d[l], ssd_norm_w[l])
        yc = gla_branch(gq, gk, gv, gg, glr, gla_gate_w[l], gla_gate_b[l], gla_norm_w[l])
        yd = rglru_branch(lx, lg, lru_conv_w[l], lru_conv_b[l], lru_wa[l], lru_ba[l], lru_wx[l], lru_bx[l],
                          lru_lambda[l])
        ys = jnp.stack([ya, yb, yc, yd], axis=2).astype(h.dtype)
        branch_out = jnp.einsum('blnw,nwd->blnd', ys, w_branch[l])
        gates = jax.nn.sigmoid(mg.astype(F32)).reshape(bsz, seqlen, N_BRANCHES, D_MODEL)
        merged = jnp.sum(gates * branch_out.astype(F32), axis=2).astype(h.dtype)
        h = h + (merged @ w_out[l]).astype(h.dtype)
        xn = rmsnorm(h, norm_ffn_w[l])
        gate_up = xn @ w_ffn_in[l]
        g_ff, u_ff = jnp.split(gate_up, [D_FF], axis=-1)
        h = h + ((jax.nn.silu(g_ff) * u_ff) @ w_ffn_out[l]).astype(h.dtype)
    return rmsnorm(h, norm_f_w)
```

```python
import functools
import math

import jax
import jax.numpy as jnp
import numpy as np
from jax import lax
from jax.experimental import pallas as pl
from jax.experimental.pallas import tpu as pltpu

F32 = jnp.float32
BF16 = jnp.bfloat16

D_MODEL = 1024
BRANCH_WIDTH = 512
N_BRANCHES = 4
CONV_WIDTH = 4
NORM_EPS = 1e-6
HGRN_HEADS = 4
SSD_HEADS = 8
SSD_HEADDIM = 64
SSD_GROUPS = 2
SSD_STATE = 64
SSD_CONV_DIM = 768
GLA_HEADS = 4
GLA_KDIM = 256
GLA_HEAD_K = 64
GLA_GATE_RANK = 16
GLA_GATE_NORMALIZER = 16.0
LRU_BLOCKS = 8
LRU_BLOCK = 64
LRU_C = 8.0
D_FF = 2816

LANES = 128
SUBLANES = 8
T_BLK = 256
HEAD_V = 128
VMEM_LIMIT = 48 * 1024 * 1024


def _dot(a, b):
    return jnp.dot(a, b, preferred_element_type=F32)


def _dot_nt(a, b):
    return lax.dot_general(a, b, (((1,), (1,)), ((), ())), preferred_element_type=F32)


def _dot_tn(a, b):
    return lax.dot_general(a, b, (((0,), (0,)), ((), ())), preferred_element_type=F32)


def _split3(x):
    hi = x.astype(BF16)
    r1 = x - hi.astype(F32)
    mid = r1.astype(BF16)
    lo = (r1 - mid.astype(F32)).astype(BF16)
    return hi, mid, lo


def _sel_dot(mat, x):
    hi, mid, lo = _split3(x)
    return (_dot(mat, lo) + _dot(mat, mid)) + _dot(mat, hi)


def _dot_sel(x, mat):
    hi, mid, lo = _split3(x)
    return (_dot(lo, mat) + _dot(mid, mat)) + _dot(hi, mat)


def _sigmoid(x):
    return jax.nn.sigmoid(x)


def _silu(x):
    return x * _sigmoid(x)


def _softplus(x):
    return jnp.maximum(x, 0.0) + jnp.log1p(jnp.exp(-jnp.abs(x)))


def _log_sigmoid(x):
    return jnp.minimum(x, 0.0) - jnp.log1p(jnp.exp(-jnp.abs(x)))


def _rmsnorm(x, w):
    return x * lax.rsqrt(jnp.mean(x * x, axis=-1, keepdims=True) + NORM_EPS) * w


def _level_ids(t):
    row = lax.broadcasted_iota(jnp.int32, (t, t), 0)
    col = lax.broadcasted_iota(jnp.int32, (t, t), 1)
    x = jnp.bitwise_xor(row, col)
    lower = col <= row
    lvl = jnp.where(lower & (x < SUBLANES), 0, -1)
    for li in range(1, _num_levels(t) + 1):
        lvl = jnp.where(lower & (jnp.right_shift(x, 2 + li) == 1), li, lvl)
    return lvl


def _num_levels(t):
    return int(math.log2(t // SUBLANES))


def _gla_block(q, k, logf, v_ref, st_ref, ql_ref, kl_ref, p_ref, tril_ref, sel_ref, hpt):
    t, nk = q.shape
    ntile = nk // LANES
    dk = LANES // hpt
    nlev = _num_levels(t)

    bcum = _sel_dot(tril_ref[...], logf)
    blast = bcum[t - 1:t, :]
    ql_ref[0] = (q * jnp.exp(bcum)).astype(BF16)
    kl_ref[0] = (k * jnp.exp(blast - bcum)).astype(BF16)
    for li in range(1, nlev + 1):
        s = SUBLANES << (li - 1)
        bc3 = bcum.reshape(t // (2 * s), 2 * s, nk)
        e = jnp.exp(-jnp.abs(bc3 - bc3[:, s - 1:s, :])).reshape(t, nk)
        ql_ref[li] = (q * e).astype(BF16)
        kl_ref[li] = (k * e).astype(BF16)
    bc8 = bcum.reshape(t // SUBLANES, SUBLANES, nk)
    q8 = q.reshape(t // SUBLANES, SUBLANES, nk)
    k8 = k.reshape(t // SUBLANES, SUBLANES, nk)
    for j in range(SUBLANES):
        pj = q8 * k8[:, j:j + 1, :] * jnp.exp(jnp.minimum(bc8 - bc8[:, j:j + 1, :], 0.0))
        pj = pj.reshape(t, nk).astype(BF16)
        for tau in range(ntile):
            p_ref[tau, :, j * LANES:(j + 1) * LANES] = pj[:, tau * LANES:(tau + 1) * LANES]

    lvl = _level_ids(t)
    lane = lax.broadcasted_iota(jnp.int32, (1, LANES), 1)
    outs = []
    for h in range(ntile * hpt):
        tau, sub = divmod(h, hpt)
        tl = slice(tau * LANES, (tau + 1) * LANES)
        hm = (lane // dk) == sub

        def own(x):
            return x if hpt == 1 else jnp.where(hm, x, jnp.zeros_like(x))

        sc = jnp.where(lvl == 0, _dot(p_ref[tau], sel_ref[:, sub * t:(sub + 1) * t]), 0.0)
        for li in range(1, nlev + 1):
            sc = jnp.where(lvl == li, _dot_nt(own(ql_ref[li, :, tl]), kl_ref[li, :, tl]), sc)
        o = _dot(sc.astype(BF16), v_ref[:, h * HEAD_V:(h + 1) * HEAD_V])
        o = o + _dot_nt(own(ql_ref[0, :, tl]), st_ref[tau].astype(BF16))
        outs.append(o)
    for tau in range(ntile):
        tl = slice(tau * LANES, (tau + 1) * LANES)
        kd = kl_ref[0, :, tl]
        upd = _dot_tn(v_ref[:, tau * hpt * HEAD_V:(tau * hpt + 1) * HEAD_V], kd)
        for sub in range(1, hpt):
            h = tau * hpt + sub
            upd = jnp.where((lane // dk) == sub, _dot_tn(v_ref[:, h * HEAD_V:(h + 1) * HEAD_V], kd), upd)
        st_ref[tau] = st_ref[tau] * jnp.exp(blast[:, tl]) + upd
    return outs


def _causal_conv(buf_ref, x, w_ref, b_ref, first):
    t = x.shape[0]

    @pl.when(first)
    def _():
        buf_ref[0:SUBLANES, :] = jnp.zeros((SUBLANES, x.shape[1]), F32)

    buf_ref[SUBLANES:SUBLANES + t, :] = x
    acc = b_ref[...] + w_ref[CONV_WIDTH - 1:CONV_WIDTH, :] * x
    for s in range(1, CONV_WIDTH):
        acc = acc + w_ref[CONV_WIDTH - 1 - s:CONV_WIDTH - s, :] * buf_ref[pl.ds(SUBLANES - s, t), :]
    buf_ref[0:SUBLANES, :] = buf_ref[t:t + SUBLANES, :]
    return acc


def _hgrn_kernel(bps, h_ref, nw_ref, w_ref, lga_ref, l1m_ref, oml_ref, hnw_ref, tril_ref, sel_ref,
                 o_ref, st_ref, ql_ref, kl_ref, p_ref, v_ref):
    first = pl.program_id(0) % bps == 0

    @pl.when(first)
    def _():
        st_ref[...] = jnp.zeros_like(st_ref)

    w = BRANCH_WIDTH
    xn = _rmsnorm(h_ref[...], nw_ref[...]).astype(BF16)
    proj = _dot(xn, w_ref[...])
    fp = proj[:, w:2 * w]
    b = l1m_ref[...] + _log_sigmoid(fp)
    a = lga_ref[...]
    logf = jnp.maximum(a, b) + jnp.log1p(jnp.exp(-jnp.abs(a - b)))
    key = oml_ref[...] * _sigmoid(-fp)
    qf = _silu(proj[:, 0:w])
    v_ref[...] = proj[:, 2 * w:3 * w].astype(BF16)
    outs = _gla_block(qf, key, logf, v_ref, st_ref, ql_ref, kl_ref, p_ref, tril_ref, sel_ref, 1)
    for h in range(HGRN_HEADS):
        g = proj[:, 3 * w + h * HEAD_V:3 * w + (h + 1) * HEAD_V]
        o_ref[:, h * HEAD_V:(h + 1) * HEAD_V] = (_rmsnorm(outs[h], hnw_ref[...]) * _silu(g)).astype(o_ref.dtype)


def _gla_kernel(bps, h_ref, nw_ref, w_ref, wlr_ref, gw_ref, gb_ref, gnw_ref, tril_ref, sel_ref,
                o_ref, st_ref, ql_ref, kl_ref, p_ref, v_ref):
    first = pl.program_id(0) % bps == 0

    @pl.when(first)
    def _():
        st_ref[...] = jnp.zeros_like(st_ref)

    kd, w = GLA_KDIM, BRANCH_WIDTH
    xn = _rmsnorm(h_ref[...], nw_ref[...]).astype(BF16)
    proj = _dot(xn, w_ref[...])
    glr = _dot(xn, wlr_ref[...])
    hi, mid, lo = _split3(glr)
    ghi, gmid, glo = _split3(gw_ref[...])
    gate = (_dot(hi, glo) + _dot(mid, gmid) + _dot(lo, ghi) + _dot(hi, gmid) + _dot(mid, ghi)) + _dot(hi, ghi)
    logf = _log_sigmoid(gate + gb_ref[...]) * (1.0 / GLA_GATE_NORMALIZER)
    q = proj[:, 0:kd] * (GLA_HEAD_K ** -0.5)
    k = proj[:, kd:2 * kd]
    v_ref[...] = proj[:, 2 * kd:2 * kd + w].astype(BF16)
    outs = _gla_block(q, k, logf, v_ref, st_ref, ql_ref, kl_ref, p_ref, tril_ref, sel_ref, 2)
    for h in range(GLA_HEADS):
        g = proj[:, 2 * kd + w + h * HEAD_V:2 * kd + w + (h + 1) * HEAD_V]
        o_ref[:, h * HEAD_V:(h + 1) * HEAD_V] = (_rmsnorm(outs[h], gnw_ref[...]) * _silu(g)).astype(o_ref.dtype)


def _ssd_kernel(bps, h_ref, nw_ref, w_ref, cw_ref, cb_ref, dtb_ref, aneg_ref, dsk_ref, snw_ref, tril_ref,
                exp_ref, o_ref, buf_ref, st_ref, y_ref):
    first = pl.program_id(0) % bps == 0

    @pl.when(first)
    def _():
        st_ref[...] = jnp.zeros_like(st_ref)

    t = h_ref.shape[0]
    w = BRANCH_WIDTH
    gs = SSD_GROUPS * SSD_STATE
    xn = _rmsnorm(h_ref[...], nw_ref[...]).astype(BF16)
    proj = _dot(xn, w_ref[...])
    z = proj[:, 0:w]
    xbc = _silu(_causal_conv(buf_ref, proj[:, w:w + SSD_CONV_DIM], cw_ref, cb_ref, first))
    xs = xbc[:, 0:w]
    bm = xbc[:, w:w + gs].astype(BF16)
    cm = xbc[:, w + gs:w + 2 * gs]
    dt = _softplus(proj[:, w + SSD_CONV_DIM:] + dtb_ref[...])
    acs = _sel_dot(tril_ref[...], dt * aneg_ref[...])
    dt_e = _dot_sel(dt, exp_ref[...])
    acs_e = _dot_sel(acs, exp_ref[...])
    alast_e = acs_e[t - 1:t, :]
    xdt = xs * dt_e
    xdt_bf = xdt.astype(BF16)

    row = lax.broadcasted_iota(jnp.int32, (t, t), 0)
    col = lax.broadcasted_iota(jnp.int32, (t, t), 1)
    causal = col <= row
    lane = lax.broadcasted_iota(jnp.int32, (1, LANES), 1)
    acs_t = acs.T
    gmats = []
    for g in range(SSD_GROUPS):
        cg = jnp.where((lane // SSD_STATE) == g, cm, 0.0).astype(BF16)
        gmats.append(_dot_nt(cg, bm))
    hpg = SSD_HEADS // SSD_GROUPS
    for pair in range(SSD_HEADS // 2):
        tl = slice(pair * LANES, (pair + 1) * LANES)
        ys = []
        for sub in range(2):
            h = 2 * pair + sub
            seg = jnp.minimum(acs[:, h:h + 1] - acs_t[h:h + 1, :], 0.0)
            m = jnp.where(causal, gmats[h // hpg] * jnp.exp(seg), 0.0).astype(BF16)
            ys.append(_dot(m, xdt_bf[:, tl]))
        y_diag = jnp.where(lane < SSD_HEADDIM, ys[0], ys[1])
        y_off = _dot(cm.astype(BF16), st_ref[:, tl].astype(BF16)) * jnp.exp(acs_e[:, tl])
        y = y_diag + y_off + xs[:, tl] * dsk_ref[:, tl]
        y_ref[:, tl] = y * _silu(z[:, tl])
    xdec = (xdt * jnp.exp(alast_e - acs_e)).astype(BF16)
    upd = _dot_tn(bm, xdec)
    srow = lax.broadcasted_iota(jnp.int32, (gs, w), 0) // SSD_STATE
    scol = lax.broadcasted_iota(jnp.int32, (gs, w), 1) // (hpg * SSD_HEADDIM)
    st_ref[...] = st_ref[...] * jnp.exp(alast_e) + jnp.where(srow == scol, upd, 0.0)
    gw = w // SSD_GROUPS
    for g in range(SSD_GROUPS):
        yg = y_ref[:, g * gw:(g + 1) * gw]
        o_ref[:, g * gw:(g + 1) * gw] = (
            yg * lax.rsqrt(jnp.mean(yg * yg, axis=-1, keepdims=True) + NORM_EPS) * snw_ref[:, g * gw:(g + 1) * gw]
        ).astype(o_ref.dtype)


def _neg_expm1(x):
    series = x * (1.0 + x * (0.5 + x * (1.0 / 6.0 + x * (1.0 / 24.0 + x * (1.0 / 120.0 + x * (
        1.0 / 720.0 + x * (1.0 / 5040.0)))))))
    return jnp.where(x > -0.25, -series, 1.0 - jnp.exp(x))


def _lru_kernel(bps, h_ref, nw_ref, w_ref, cw_ref, cb_ref, wg_ref, bg_ref, lam_ref, o_ref, buf_ref, hc_ref):
    first = pl.program_id(0) % bps == 0

    @pl.when(first)
    def _():
        hc_ref[...] = jnp.zeros_like(hc_ref)

    t = h_ref.shape[0]
    w = BRANCH_WIDTH
    xn = _rmsnorm(h_ref[...], nw_ref[...]).astype(BF16)
    proj = _dot(xn, w_ref[...])
    u = _causal_conv(buf_ref, proj[:, 0:w], cw_ref, cb_ref, first)
    rg = _dot(u.astype(BF16), wg_ref[...]) + bg_ref[...]
    r = _sigmoid(rg[:, 0:w])
    ig = _sigmoid(rg[:, w:2 * w])
    log_a = (-LRU_C) * r * _softplus(-lam_ref[...])
    a = jnp.exp(log_a)
    b = jnp.sqrt(_neg_expm1(2.0 * log_a)) * (ig * u)
    row = lax.broadcasted_iota(jnp.int32, (t, 1), 0)
    off = 1
    while off < t:
        keep = row >= off
        a_sh = jnp.where(keep, pltpu.roll(a, off, 0), 1.0)
        b_sh = jnp.where(keep, pltpu.roll(b, off, 0), 0.0)
        b = a * b_sh + b
        a = a * a_sh
        off *= 2
    hseq = b + a * hc_ref[...]
    hc_ref[...] = hseq[t - 1:t, :]
    gate = proj[:, w:2 * w]
    gelu = 0.5 * gate * (1.0 + jnp.tanh(math.sqrt(2.0 / math.pi) * (gate + 0.044715 * gate * gate * gate)))
    o_ref[...] = (hseq * gelu).astype(o_ref.dtype)


def _merge_kernel(h_ref, nw_ref, ya_ref, yb_ref, yc_ref, yd_ref, wmg_ref, wbr_ref, wout_ref, o_ref):
    hres = h_ref[...]
    xn = _rmsnorm(hres, nw_ref[...]).astype(BF16)
    merged = None
    for n, y_ref in enumerate((ya_ref, yb_ref, yc_ref, yd_ref)):
        gate = _sigmoid(_dot(xn, wmg_ref[:, n * D_MODEL:(n + 1) * D_MODEL]))
        term = gate * _dot(y_ref[...], wbr_ref[n])
        merged = term if merged is None else merged + term
    o_ref[...] = hres + _dot(merged.astype(BF16), wout_ref[...])


def _ffn_kernel(final_norm, h_ref, nw_ref, win_ref, wout_ref, fnw_ref, o_ref):
    hres = h_ref[...]
    xn = _rmsnorm(hres, nw_ref[...]).astype(BF16)
    gu = _dot(xn, win_ref[...])
    act = (_silu(gu[:, 0:D_FF]) * gu[:, D_FF:2 * D_FF]).astype(BF16)
    out = hres + _dot(act, wout_ref[...])
    if final_norm:
        out = _rmsnorm(out, fnw_ref[...])
    o_ref[...] = out


def _const_spec(shape):
    nd = len(shape)
    return pl.BlockSpec(shape, lambda i, _nd=nd: (0,) * _nd, pipeline_mode=pl.Buffered(1))


def _row_spec(width):
    return pl.BlockSpec((T_BLK, width), lambda i: (i, 0))


def _pcall(kernel, name, n_tokens, operands, row_flags, out_width, out_dtype, scratch):
    in_specs = [(_row_spec(a.shape[1]) if is_row else _const_spec(a.shape)) for a, is_row in zip(operands, row_flags)]
    return pl.pallas_call(
        kernel,
        name=name,
        grid=(n_tokens // T_BLK,),
        in_specs=in_specs,
        out_specs=_row_spec(out_width),
        out_shape=jax.ShapeDtypeStruct((n_tokens, out_width), out_dtype),
        scratch_shapes=scratch,
        compiler_params=pltpu.CompilerParams(dimension_semantics=("arbitrary",), vmem_limit_bytes=VMEM_LIMIT),
    )(*operands)


def _tril_const(t):
    return jnp.asarray(np.tril(np.ones((t, t), np.float32)), BF16)


def _sel_const(t, hpt):
    dk = LANES // hpt
    j = np.arange(SUBLANES * LANES)[:, None] // LANES
    d = np.arange(SUBLANES * LANES)[:, None] % LANES
    n = np.arange(hpt * t)[None, :]
    return jnp.asarray(((n % t) % SUBLANES == j) & (d // dk == n // t), BF16)


def _gla_scratch(nk):
    nlev = _num_levels(T_BLK)
    ntile = nk // LANES
    return [
        pltpu.VMEM((ntile, HEAD_V, LANES), F32),
        pltpu.VMEM((nlev + 1, T_BLK, nk), BF16),
        pltpu.VMEM((nlev + 1, T_BLK, nk), BF16),
        pltpu.VMEM((ntile, T_BLK, SUBLANES * LANES), BF16),
        pltpu.VMEM((T_BLK, BRANCH_WIDTH), BF16),
    ]


def _row(x):
    return x.reshape(1, -1).astype(F32)


def _layer(h, seq, p):
    n_tokens = h.shape[0]
    bps = seq // T_BLK
    tril = _tril_const(T_BLK)
    w_in = p["w_in"]
    splits = np.cumsum([512, 512, 512, 512, 512, 768, 8, 256, 256, 512, 512, 16, 512, 512, 4096])
    c = [0] + splits.tolist()
    nw = _row(p["norm_mix_w"])

    lb = p["lb"].astype(F32)
    ya = _pcall(
        functools.partial(_hgrn_kernel, bps), "hgrn2", n_tokens,
        [h, nw, w_in[:, c[0]:c[4]].astype(BF16), _row(jnp.log(lb)), _row(jnp.log1p(-lb)), _row(1.0 - lb),
         _row(p["hgrn_norm_w"]), tril, _sel_const(T_BLK, 1)],
        [True] + [False] * 8, BRANCH_WIDTH, BF16, _gla_scratch(BRANCH_WIDTH))

    w_ssd = jnp.concatenate([w_in[:, c[4]:c[7]], jnp.zeros((D_MODEL, LANES - SSD_HEADS), F32)], axis=1).astype(BF16)
    pad8 = lambda v: jnp.concatenate([v.astype(F32), jnp.zeros((LANES - SSD_HEADS,), F32)]).reshape(1, LANES)
    expand = jnp.asarray(np.arange(LANES)[:, None] == (np.arange(BRANCH_WIDTH)[None, :] // SSD_HEADDIM), BF16)
    yb = _pcall(
        functools.partial(_ssd_kernel, bps), "ssd", n_tokens,
        [h, nw, w_ssd, p["ssd_conv_w"].astype(F32), _row(p["ssd_conv_b"]), pad8(p["ssd_dt_bias"]),
         pad8(-jnp.exp(p["ssd_a_log"].astype(F32))), _row(jnp.repeat(p["ssd_d"], SSD_HEADDIM)), _row(p["ssd_norm_w"]),
         tril, expand],
        [True] + [False] * 10, BRANCH_WIDTH, BF16,
        [pltpu.VMEM((T_BLK + SUBLANES, SSD_CONV_DIM), F32), pltpu.VMEM((SSD_GROUPS * SSD_STATE, BRANCH_WIDTH), F32),
         pltpu.VMEM((T_BLK, BRANCH_WIDTH), F32)])

    w_lr = jnp.concatenate([w_in[:, c[11]:c[12]], jnp.zeros((D_MODEL, LANES - GLA_GATE_RANK), F32)], axis=1).astype(BF16)
    gate_w = jnp.concatenate([p["gla_gate_w"].astype(F32), jnp.zeros((LANES - GLA_GATE_RANK, GLA_KDIM), F32)], axis=0)
    yc = _pcall(
        functools.partial(_gla_kernel, bps), "gla", n_tokens,
        [h, nw, w_in[:, c[7]:c[11]].astype(BF16), w_lr, gate_w, _row(p["gla_gate_b"]), _row(p["gla_norm_w"]), tril,
         _sel_const(T_BLK, 2)],
        [True] + [False] * 8, BRANCH_WIDTH, BF16, _gla_scratch(GLA_KDIM))

    eye = jnp.eye(LRU_BLOCKS, dtype=F32)
    bdiag = lambda wt: jnp.einsum("kde,kl->kdle", wt.astype(F32), eye).reshape(BRANCH_WIDTH, BRANCH_WIDTH)
    w_gates = jnp.concatenate([bdiag(p["lru_wa"]), bdiag(p["lru_wx"])], axis=1).astype(BF16)
    b_gates = _row(jnp.concatenate([p["lru_ba"], p["lru_bx"]]))
    yd = _pcall(
        functools.partial(_lru_kernel, bps), "rglru", n_tokens,
        [h, nw, w_in[:, c[12]:c[14]].astype(BF16), p["lru_conv_w"].astype(F32), _row(p["lru_conv_b"]), w_gates,
         b_gates, _row(p["lru_lambda"])],
        [True] + [False] * 7, BRANCH_WIDTH, BF16,
        [pltpu.VMEM((T_BLK + SUBLANES, BRANCH_WIDTH), F32), pltpu.VMEM((1, BRANCH_WIDTH), F32)])

    h = _pcall(
        _merge_kernel, "merge", n_tokens,
        [h, nw, ya, yb, yc, yd, w_in[:, c[14]:c[15]].astype(BF16), p["w_branch"].astype(BF16),
         p["w_out"].astype(BF16)],
        [True, False, True, True, True, True, False, False, False], D_MODEL, F32, [])

    h = _pcall(
        functools.partial(_ffn_kernel, p["final"]), "swiglu", n_tokens,
        [h, _row(p["norm_ffn_w"]), p["w_ffn_in"].astype(BF16), p["w_ffn_out"].astype(BF16), _row(p["norm_f_w"])],
        [True, False, False, False, False], D_MODEL, F32, [])
    return h


def kernel(x, norm_mix_w, w_in, hgrn_lower_bounds, hgrn_norm_w, ssd_conv_w, ssd_conv_b, ssd_dt_bias, ssd_a_log, ssd_d, ssd_norm_w, gla_gate_w, gla_gate_b, gla_norm_w, lru_conv_w, lru_conv_b, lru_wa, lru_ba, lru_wx, lru_bx, lru_lambda, w_branch, w_out, norm_ffn_w, w_ffn_in, w_ffn_out, norm_f_w):
    bsz, seq, _ = x.shape
    depth = w_in.shape[0]
    lb_all = jnp.cumsum(jax.nn.softmax(hgrn_lower_bounds.astype(F32), axis=0), axis=0)
    lb_all = lb_all - lb_all[0:1]
    h = x.reshape(bsz * seq, D_MODEL)
    for l in range(depth):
        p = dict(
            norm_mix_w=norm_mix_w[l], w_in=w_in[l], lb=lb_all[l], hgrn_norm_w=hgrn_norm_w[l],
            ssd_conv_w=ssd_conv_w[l], ssd_conv_b=ssd_conv_b[l], ssd_dt_bias=ssd_dt_bias[l], ssd_a_log=ssd_a_log[l],
            ssd_d=ssd_d[l], ssd_norm_w=ssd_norm_w[l], gla_gate_w=gla_gate_w[l], gla_gate_b=gla_gate_b[l],
            gla_norm_w=gla_norm_w[l], lru_conv_w=lru_conv_w[l], lru_conv_b=lru_conv_b[l], lru_wa=lru_wa[l],
            lru_ba=lru_ba[l], lru_wx=lru_wx[l], lru_bx=lru_bx[l], lru_lambda=lru_lambda[l], w_branch=w_branch[l],
            w_out=w_out[l], norm_ffn_w=norm_ffn_w[l], w_ffn_in=w_ffn_in[l], w_ffn_out=w_ffn_out[l],
            norm_f_w=norm_f_w, final=(l == depth - 1))
        h = _layer(h, seq, p)
    return h.reshape(bsz, seq, D_MODEL)
```

```python
import functools
import math

import jax
import jax.numpy as jnp
import numpy as np
from jax import lax
from jax.experimental import pallas as pl
from jax.experimental.pallas import tpu as pltpu

F32 = jnp.float32
BF16 = jnp.bfloat16

D_MODEL = 1024
BRANCH_WIDTH = 512
N_BRANCHES = 4
CONV_WIDTH = 4
NORM_EPS = 1e-6
HGRN_HEADS = 4
SSD_HEADS = 8
SSD_HEADDIM = 64
SSD_GROUPS = 2
SSD_STATE = 64
SSD_CONV_DIM = 768
GLA_HEADS = 4
GLA_KDIM = 256
GLA_HEAD_K = 64
GLA_GATE_RANK = 16
GLA_GATE_NORMALIZER = 16.0
LRU_BLOCKS = 8
LRU_BLOCK = 64
LRU_C = 8.0
D_FF = 2816

LANES = 128
SUBLANES = 8
T_BLK = 256
BLOCKS_PER_STEP = 2
T_STEP = BLOCKS_PER_STEP * T_BLK
HEAD_V = 128
VMEM_LIMIT = 48 * 1024 * 1024


def _dot(a, b):
    return jnp.dot(a, b, preferred_element_type=F32)


def _dot_nt(a, b):
    return lax.dot_general(a, b, (((1,), (1,)), ((), ())), preferred_element_type=F32)


def _dot_tn(a, b):
    return lax.dot_general(a, b, (((0,), (0,)), ((), ())), preferred_element_type=F32)


def _split3(x):
    hi = x.astype(BF16)
    r1 = x - hi.astype(F32)
    mid = r1.astype(BF16)
    lo = (r1 - mid.astype(F32)).astype(BF16)
    return hi, mid, lo


def _sel_dot(mat, x):
    hi, mid, lo = _split3(x)
    return (_dot(mat, lo) + _dot(mat, mid)) + _dot(mat, hi)


def _dot_sel(x, mat):
    hi, mid, lo = _split3(x)
    return (_dot(lo, mat) + _dot(mid, mat)) + _dot(hi, mat)


def _sigmoid(x):
    return jax.nn.sigmoid(x)


def _silu(x):
    return x * _sigmoid(x)


def _softplus(x):
    return jnp.maximum(x, 0.0) + jnp.log1p(jnp.exp(-jnp.abs(x)))


def _log_sigmoid(x):
    return jnp.minimum(x, 0.0) - jnp.log1p(jnp.exp(-jnp.abs(x)))


def _rmsnorm(x, w):
    return x * lax.rsqrt(jnp.mean(x * x, axis=-1, keepdims=True) + NORM_EPS) * w


GLA_SMALL = (1, 2, 4)
GLA_MID = (8, 16, 32, 64)
GLA_PARTS = 2 + len(GLA_SMALL) + len(GLA_MID) + 1
LOG2E = 1.4426950408889634


def _gla_part_ids():
    row = np.arange(LANES)[:, None]
    col = np.arange(LANES)[None, :]
    x = row ^ col
    ids = np.where(x == 0, 0, np.floor(np.log2(np.maximum(x, 1))).astype(np.int64) + 1)
    return jnp.asarray(np.where(col <= row, ids, -1), jnp.int32)


def _gla_decay_matrix(t):
    u = np.arange(t)[None, :]
    r = np.arange(t)[:, None]
    blocks = [u <= r]
    for s in GLA_SMALL:
        mid = (r // (2 * s)) * (2 * s) + s - 1
        blocks.append(((u > mid) & (u <= r)) | ((u > r) & (u <= mid)))
    m = np.concatenate(blocks, axis=0)
    return jnp.asarray(np.concatenate([m, m], axis=1), BF16)


def _gla_block(q, k, logf, v_ref, st_ref, ql_ref, kl_ref, rhs_ref, dm_ref, ids_ref, hpt):
    t, nk = q.shape
    assert t == 2 * LANES
    ntile = nk // LANES
    dk = LANES // hpt

    lf2 = logf * LOG2E
    hi = lf2.astype(BF16)
    rhs_ref[0:t, :] = hi
    rhs_ref[t:2 * t, :] = (lf2 - hi.astype(F32)).astype(BF16)
    d4 = _dot(dm_ref[...], rhs_ref[...])
    bcum = d4[0:t]
    blast = bcum[t - 1:t, :]
    qb = q.astype(BF16)
    kb = k.astype(BF16)
    ql_ref[0] = qb * jnp.exp2(bcum).astype(BF16)
    kl_ref[0] = kb * jnp.exp2(blast - bcum).astype(BF16)
    part = 1
    for i in range(len(GLA_SMALL)):
        e = jnp.exp2(d4[(i + 1) * t:(i + 2) * t]).astype(BF16)
        ql_ref[part] = qb * e
        kl_ref[part] = kb * e
        part += 1
    for s in GLA_MID + (LANES,):
        bc3 = bcum.reshape(t // (2 * s), 2 * s, nk)
        e = jnp.exp2(-jnp.abs(bc3 - bc3[:, s - 1:s, :])).reshape(t, nk).astype(BF16)
        ql_ref[part] = qb * e
        kl_ref[part] = kb * e
        part += 1
    ql_ref[part] = qb
    kl_ref[part] = kb
    p_unscaled, p_top = part, part - 1
    masked = [(0, p_unscaled)] + [(1 + i, 1 + i) for i in range(len(GLA_SMALL) + len(GLA_MID))]

    ids = ids_ref[...]
    lane = lax.broadcasted_iota(jnp.int32, (1, LANES), 1)
    outs = []
    for h in range(ntile * hpt):
        tau, sub = divmod(h, hpt)
        tl = slice(tau * LANES, (tau + 1) * LANES)
        hv = slice(h * HEAD_V, (h + 1) * HEAD_V)
        hm = (lane // dk) == sub

        def own(x):
            return x if hpt == 1 else jnp.where(hm, x, jnp.zeros_like(x))

        diag = []
        for c in range(2):
            rows = slice(c * LANES, (c + 1) * LANES)
            sc = jnp.zeros((LANES, LANES), F32)
            for pid, p in masked:
                sc = jnp.where(ids == pid, _dot_nt(own(ql_ref[p, rows, tl]), kl_ref[p, rows, tl]), sc)
            diag.append(sc.astype(BF16))
        low = _dot_nt(own(ql_ref[p_top, LANES:t, tl]), kl_ref[p_top, 0:LANES, tl]).astype(BF16)
        o_top = _dot(diag[0], v_ref[0:LANES, hv])
        o_bot = _dot(jnp.concatenate([low, diag[1]], axis=1), v_ref[:, hv])
        o = jnp.concatenate([o_top, o_bot], axis=0)
        outs.append(o + _dot_nt(own(ql_ref[0, :, tl]), st_ref[tau].astype(BF16)))
    for tau in range(ntile):
        tl = slice(tau * LANES, (tau + 1) * LANES)
        kd = kl_ref[0, :, tl]
        upd = _dot_tn(v_ref[:, tau * hpt * HEAD_V:(tau * hpt + 1) * HEAD_V], kd)
        for sub in range(1, hpt):
            h = tau * hpt + sub
            upd = jnp.where((lane // dk) == sub, _dot_tn(v_ref[:, h * HEAD_V:(h + 1) * HEAD_V], kd), upd)
        st_ref[tau] = st_ref[tau] * jnp.exp2(blast[:, tl]) + upd
    return outs


def _causal_conv(buf_ref, x, w_ref, b_ref, base):
    t = x.shape[0]
    buf_ref[SUBLANES + base:SUBLANES + base + t, :] = x
    acc = b_ref[...] + w_ref[CONV_WIDTH - 1:CONV_WIDTH, :] * x
    for s in range(1, CONV_WIDTH):
        acc = acc + w_ref[CONV_WIDTH - 1 - s:CONV_WIDTH - s, :] * buf_ref[pl.ds(SUBLANES + base - s, t), :]
    return acc


def _step_start(bps, *carry_refs):
    @pl.when((pl.program_id(0) * BLOCKS_PER_STEP) % bps == 0)
    def _():
        for ref in carry_refs:
            ref[...] = jnp.zeros_like(ref)


def _conv_carry(buf_ref):
    buf_ref[0:SUBLANES, :] = buf_ref[T_STEP:T_STEP + SUBLANES, :]


def _hgrn_kernel(bps, h_ref, nw_ref, w_ref, lga_ref, l1m_ref, oml_ref, hnw_ref, dm_ref, ids_ref,
                 o_ref, st_ref, *scratch):
    _step_start(bps, st_ref)
    w = BRANCH_WIDTH
    staged = []
    for blk in range(BLOCKS_PER_STEP):
        rows = slice(blk * T_BLK, (blk + 1) * T_BLK)
        ql_ref, kl_ref, rhs_ref, v_ref = scratch[4 * blk:4 * blk + 4]
        xn = _rmsnorm(h_ref[rows, :], nw_ref[...]).astype(BF16)
        proj = _dot(xn, w_ref[...])
        fp = proj[:, w:2 * w]
        b = l1m_ref[...] + _log_sigmoid(fp)
        a = lga_ref[...]
        logf = jnp.maximum(a, b) + jnp.log1p(jnp.exp(-jnp.abs(a - b)))
        key = oml_ref[...] * _sigmoid(-fp)
        qf = _silu(proj[:, 0:w])
        v_ref[...] = proj[:, 2 * w:3 * w].astype(BF16)
        staged.append((qf, key, logf, _silu(proj[:, 3 * w:4 * w])))
    for blk in range(BLOCKS_PER_STEP):
        rows = slice(blk * T_BLK, (blk + 1) * T_BLK)
        ql_ref, kl_ref, rhs_ref, v_ref = scratch[4 * blk:4 * blk + 4]
        qf, key, logf, gs = staged[blk]
        outs = _gla_block(qf, key, logf, v_ref, st_ref, ql_ref, kl_ref, rhs_ref, dm_ref, ids_ref, 1)
        for h in range(HGRN_HEADS):
            o_ref[rows, h * HEAD_V:(h + 1) * HEAD_V] = (
                _rmsnorm(outs[h], hnw_ref[...]) * gs[:, h * HEAD_V:(h + 1) * HEAD_V]).astype(o_ref.dtype)


def _gla_kernel(bps, h_ref, nw_ref, w_ref, gb_ref, gnw_ref, dm_ref, ids_ref, o_ref, st_ref, *scratch):
    _step_start(bps, st_ref)
    kd, w = GLA_KDIM, BRANCH_WIDTH
    for blk in range(BLOCKS_PER_STEP):
        rows = slice(blk * T_BLK, (blk + 1) * T_BLK)
        ql_ref, kl_ref, rhs_ref, v_ref = scratch[4 * blk:4 * blk + 4]
        xn = _rmsnorm(h_ref[rows, :], nw_ref[...]).astype(BF16)
        proj = _dot(xn, w_ref[...])
        logf = _log_sigmoid(proj[:, 0:kd] + gb_ref[...]) * (1.0 / GLA_GATE_NORMALIZER)
        proj = proj[:, kd:]
        q = proj[:, 0:kd] * (GLA_HEAD_K ** -0.5)
        k = proj[:, kd:2 * kd]
        v_ref[...] = proj[:, 2 * kd:2 * kd + w].astype(BF16)
        outs = _gla_block(q, k, logf, v_ref, st_ref, ql_ref, kl_ref, rhs_ref, dm_ref, ids_ref, 2)
        for h in range(GLA_HEADS):
            g = proj[:, 2 * kd + w + h * HEAD_V:2 * kd + w + (h + 1) * HEAD_V]
            o_ref[rows, h * HEAD_V:(h + 1) * HEAD_V] = (
                _rmsnorm(outs[h], gnw_ref[...]) * _silu(g)).astype(o_ref.dtype)


def _ssd_kernel(bps, h_ref, nw_ref, w_ref, cw_ref, cb_ref, dtb_ref, aneg_ref, dsk_ref, snw_ref, tril_ref,
                exp_ref, o_ref, buf_ref, st_ref, *y_refs):
    _step_start(bps, st_ref, buf_ref)
    for blk in range(BLOCKS_PER_STEP):
        _ssd_block(blk, h_ref, nw_ref, w_ref, cw_ref, cb_ref, dtb_ref, aneg_ref, dsk_ref, snw_ref, tril_ref,
                   exp_ref, o_ref, buf_ref, st_ref, y_refs[blk])
    _conv_carry(buf_ref)


def _ssd_block(blk, h_ref, nw_ref, w_ref, cw_ref, cb_ref, dtb_ref, aneg_ref, dsk_ref, snw_ref, tril_ref,
               exp_ref, o_ref, buf_ref, st_ref, y_ref):
    t = T_BLK
    rows = slice(blk * t, (blk + 1) * t)
    w = BRANCH_WIDTH
    gs = SSD_GROUPS * SSD_STATE
    xn = _rmsnorm(h_ref[rows, :], nw_ref[...]).astype(BF16)
    proj = _dot(xn, w_ref[...])
    z = proj[:, 0:w]
    xbc = _silu(_causal_conv(buf_ref, proj[:, w:w + SSD_CONV_DIM], cw_ref, cb_ref, blk * t))
    xs = xbc[:, 0:w]
    bm = xbc[:, w:w + gs].astype(BF16)
    cm = xbc[:, w + gs:w + 2 * gs]
    dt = _softplus(proj[:, w + SSD_CONV_DIM:] + dtb_ref[...])
    acs = _sel_dot(tril_ref[...], dt * aneg_ref[...])
    dt_e = _dot_sel(dt, exp_ref[...])
    acs_e = _dot_sel(acs, exp_ref[...])
    alast_e = acs_e[t - 1:t, :]
    xdt = xs * dt_e
    xdt_bf = xdt.astype(BF16)

    row = lax.broadcasted_iota(jnp.int32, (t, t), 0)
    col = lax.broadcasted_iota(jnp.int32, (t, t), 1)
    causal = col <= row
    lane = lax.broadcasted_iota(jnp.int32, (1, LANES), 1)
    acs_t = acs.T
    gmats = []
    for g in range(SSD_GROUPS):
        cg = jnp.where((lane // SSD_STATE) == g, cm, 0.0).astype(BF16)
        gmats.append(_dot_nt(cg, bm))
    hpg = SSD_HEADS // SSD_GROUPS
    for pair in range(SSD_HEADS // 2):
        tl = slice(pair * LANES, (pair + 1) * LANES)
        ys = []
        for sub in range(2):
            h = 2 * pair + sub
            seg = jnp.minimum(acs[:, h:h + 1] - acs_t[h:h + 1, :], 0.0)
            m = jnp.where(causal, gmats[h // hpg] * jnp.exp(seg), 0.0).astype(BF16)
            ys.append(_dot(m, xdt_bf[:, tl]))
        y_diag = jnp.where(lane < SSD_HEADDIM, ys[0], ys[1])
        y_off = _dot(cm.astype(BF16), st_ref[:, tl].astype(BF16)) * jnp.exp(acs_e[:, tl])
        y = y_diag + y_off + xs[:, tl] * dsk_ref[:, tl]
        y_ref[:, tl] = y * _silu(z[:, tl])
    xdec = (xdt * jnp.exp(alast_e - acs_e)).astype(BF16)
    upd = _dot_tn(bm, xdec)
    srow = lax.broadcasted_iota(jnp.int32, (gs, w), 0) // SSD_STATE
    scol = lax.broadcasted_iota(jnp.int32, (gs, w), 1) // (hpg * SSD_HEADDIM)
    st_ref[...] = st_ref[...] * jnp.exp(alast_e) + jnp.where(srow == scol, upd, 0.0)
    gw = w // SSD_GROUPS
    for g in range(SSD_GROUPS):
        yg = y_ref[:, g * gw:(g + 1) * gw]
        o_ref[rows, g * gw:(g + 1) * gw] = (
            yg * lax.rsqrt(jnp.mean(yg * yg, axis=-1, keepdims=True) + NORM_EPS) * snw_ref[:, g * gw:(g + 1) * gw]
        ).astype(o_ref.dtype)


def _sqrt_one_minus_exp2x(x):
    th = jnp.tanh(x)
    y = (-2.0 * th) / (1.0 - th)
    return y * lax.rsqrt(jnp.maximum(y, 1e-30))


def _doubling_scan(a, b, pos, n, axis):
    off = 1
    while off < n:
        keep = pos >= off
        a_sh = jnp.where(keep, pltpu.roll(a, off, axis), 1.0)
        b_sh = jnp.where(keep, pltpu.roll(b, off, axis), 0.0)
        b = a * b_sh + b
        a = a * a_sh
        off *= 2
    return a, b


def _scan_rows(a, b, hc_ref, ab_ref, cg_ref):
    t, n = a.shape
    g = t // SUBLANES
    nt = n // LANES
    a8, b8 = _doubling_scan(a.reshape(g, SUBLANES, n), b.reshape(g, SUBLANES, n),
                            lax.broadcasted_iota(jnp.int32, (1, SUBLANES, 1), 1), SUBLANES, 1)
    a8 = a8.reshape(t, n)
    b8 = b8.reshape(t, n)
    for i in range(nt):
        ab_ref[i] = a8[:, i * LANES:(i + 1) * LANES]
        ab_ref[nt + i] = b8[:, i * LANES:(i + 1) * LANES]
    last = pl.ds(SUBLANES - 1, g, stride=SUBLANES)
    gidx = lax.broadcasted_iota(jnp.int32, (g, 1), 0)
    ag = jnp.concatenate([ab_ref[i, last, :] for i in range(nt)], axis=1)
    bg = jnp.concatenate([ab_ref[nt + i, last, :] for i in range(nt)], axis=1)
    ag, bg = _doubling_scan(ag, bg, gidx, g, 0)
    state = bg + ag * hc_ref[...]
    cg_ref[...] = jnp.where(gidx >= 1, pltpu.roll(state, 1, 0), hc_ref[...])
    hc_ref[...] = state[g - 1:g, :]
    out = []
    for i in range(nt):
        cg = cg_ref[:, i * LANES:(i + 1) * LANES].reshape(g, 1, LANES)
        hi = ab_ref[nt + i].reshape(g, SUBLANES, LANES) + ab_ref[i].reshape(g, SUBLANES, LANES) * cg
        out.append(hi.reshape(t, LANES))
    return jnp.concatenate(out, axis=1)


def _lru_kernel(bps, h_ref, nw_ref, w_ref, cw_ref, cb_ref, wg_ref, bg_ref, lam_ref, o_ref, buf_ref, hc_ref,
                *scratch):
    _step_start(bps, hc_ref, buf_ref)
    for blk in range(BLOCKS_PER_STEP):
        _lru_block(blk, h_ref, nw_ref, w_ref, cw_ref, cb_ref, wg_ref, bg_ref, lam_ref, o_ref, buf_ref, hc_ref,
                   *scratch[2 * blk:2 * blk + 2])
    _conv_carry(buf_ref)


def _lru_block(blk, h_ref, nw_ref, w_ref, cw_ref, cb_ref, wg_ref, bg_ref, lam_ref, o_ref, buf_ref, hc_ref,
               ab_ref, cg_ref):
    t = T_BLK
    rows = slice(blk * t, (blk + 1) * t)
    w = BRANCH_WIDTH
    xn = _rmsnorm(h_ref[rows, :], nw_ref[...]).astype(BF16)
    proj = _dot(xn, w_ref[...])
    u = _causal_conv(buf_ref, proj[:, 0:w], cw_ref, cb_ref, blk * t)
    rg = _dot(u.astype(BF16), wg_ref[...]) + bg_ref[...]
    r = _sigmoid(rg[:, 0:w])
    ig = _sigmoid(rg[:, w:2 * w])
    log_a = (-LRU_C) * r * _softplus(-lam_ref[...])
    a = jnp.exp(log_a)
    b = _sqrt_one_minus_exp2x(log_a) * (ig * u)
    hseq = _scan_rows(a, b, hc_ref, ab_ref, cg_ref)
    gate = proj[:, w:2 * w]
    gelu = 0.5 * gate * (1.0 + jnp.tanh(math.sqrt(2.0 / math.pi) * (gate + 0.044715 * gate * gate * gate)))
    o_ref[rows, :] = (hseq * gelu).astype(o_ref.dtype)


def _merge_kernel(h_ref, nw_ref, ya_ref, yb_ref, yc_ref, yd_ref, wmg_ref, wbr_ref, wout_ref, o_ref):
    hres = h_ref[...]
    xn = _rmsnorm(hres, nw_ref[...]).astype(BF16)
    merged = None
    for n, y_ref in enumerate((ya_ref, yb_ref, yc_ref, yd_ref)):
        gate = _sigmoid(_dot(xn, wmg_ref[:, n * D_MODEL:(n + 1) * D_MODEL]))
        term = gate * _dot(y_ref[...], wbr_ref[n])
        merged = term if merged is None else merged + term
    o_ref[...] = hres + _dot(merged.astype(BF16), wout_ref[...])


def _ffn_kernel(final_norm, h_ref, nw_ref, win_ref, wout_ref, fnw_ref, o_ref):
    hres = h_ref[...]
    xn = _rmsnorm(hres, nw_ref[...]).astype(BF16)
    gu = _dot(xn, win_ref[...])
    act = (_silu(gu[:, 0:D_FF]) * gu[:, D_FF:2 * D_FF]).astype(BF16)
    out = hres + _dot(act, wout_ref[...])
    if final_norm:
        out = _rmsnorm(out, fnw_ref[...])
    o_ref[...] = out


def _const_spec(shape):
    nd = len(shape)
    return pl.BlockSpec(shape, lambda i, _nd=nd: (0,) * _nd, pipeline_mode=pl.Buffered(1))


def _row_spec(rows, width):
    return pl.BlockSpec((rows, width), lambda i: (i, 0))


def _pcall(kernel, name, n_tokens, rows, operands, row_flags, out_width, out_dtype, scratch):
    in_specs = [(_row_spec(rows, a.shape[1]) if is_row else _const_spec(a.shape))
                for a, is_row in zip(operands, row_flags)]
    return pl.pallas_call(
        kernel,
        name=name,
        grid=(n_tokens // rows,),
        in_specs=in_specs,
        out_specs=_row_spec(rows, out_width),
        out_shape=jax.ShapeDtypeStruct((n_tokens, out_width), out_dtype),
        scratch_shapes=scratch,
        compiler_params=pltpu.CompilerParams(dimension_semantics=("arbitrary",), vmem_limit_bytes=VMEM_LIMIT),
    )(*operands)


def _tril_const(t):
    return jnp.asarray(np.tril(np.ones((t, t), np.float32)), BF16)


def _gla_scratch(nk):
    ntile = nk // LANES
    per_block = [
        pltpu.VMEM((GLA_PARTS, T_BLK, nk), BF16),
        pltpu.VMEM((GLA_PARTS, T_BLK, nk), BF16),
        pltpu.VMEM((2 * T_BLK, nk), BF16),
        pltpu.VMEM((T_BLK, BRANCH_WIDTH), BF16),
    ]
    return [pltpu.VMEM((ntile, HEAD_V, LANES), F32)] + per_block * BLOCKS_PER_STEP


def _row(x):
    return x.reshape(1, -1).astype(F32)


def _layer(h, seq, p):
    n_tokens = h.shape[0]
    bps = seq // T_BLK
    assert seq % T_STEP == 0 and n_tokens % seq == 0
    tril = _tril_const(T_BLK)
    dmat = _gla_decay_matrix(T_BLK)
    part_ids = _gla_part_ids()
    w_in = p["w_in"]
    splits = np.cumsum([512, 512, 512, 512, 512, 768, 8, 256, 256, 512, 512, 16, 512, 512, 4096])
    c = [0] + splits.tolist()
    nw = _row(p["norm_mix_w"])

    lb = p["lb"].astype(F32)
    ya = _pcall(
        functools.partial(_hgrn_kernel, bps), "hgrn2", n_tokens, T_STEP,
        [h, nw, w_in[:, c[0]:c[4]].astype(BF16), _row(jnp.log(lb)), _row(jnp.log1p(-lb)), _row(1.0 - lb),
         _row(p["hgrn_norm_w"]), dmat, part_ids],
        [True] + [False] * 8, BRANCH_WIDTH, BF16, _gla_scratch(BRANCH_WIDTH))

    w_ssd = jnp.concatenate([w_in[:, c[4]:c[7]], jnp.zeros((D_MODEL, LANES - SSD_HEADS), F32)], axis=1).astype(BF16)
    pad8 = lambda v: jnp.concatenate([v.astype(F32), jnp.zeros((LANES - SSD_HEADS,), F32)]).reshape(1, LANES)
    expand = jnp.asarray(np.arange(LANES)[:, None] == (np.arange(BRANCH_WIDTH)[None, :] // SSD_HEADDIM), BF16)
    yb = _pcall(
        functools.partial(_ssd_kernel, bps), "ssd", n_tokens, T_STEP,
        [h, nw, w_ssd, p["ssd_conv_w"].astype(F32), _row(p["ssd_conv_b"]), pad8(p["ssd_dt_bias"]),
         pad8(-jnp.exp(p["ssd_a_log"].astype(F32))), _row(jnp.repeat(p["ssd_d"], SSD_HEADDIM)), _row(p["ssd_norm_w"]),
         tril, expand],
        [True] + [False] * 10, BRANCH_WIDTH, BF16,
        [pltpu.VMEM((T_STEP + SUBLANES, SSD_CONV_DIM), F32), pltpu.VMEM((SSD_GROUPS * SSD_STATE, BRANCH_WIDTH), F32)]
        + [pltpu.VMEM((T_BLK, BRANCH_WIDTH), F32)] * BLOCKS_PER_STEP)

    w_gate = jnp.dot(w_in[:, c[11]:c[12]].astype(F32), p["gla_gate_w"].astype(F32), precision=lax.Precision.HIGHEST)
    w_gla = jnp.concatenate([w_gate, w_in[:, c[7]:c[11]]], axis=1).astype(BF16)
    yc = _pcall(
        functools.partial(_gla_kernel, bps), "gla", n_tokens, T_STEP,
        [h, nw, w_gla, _row(p["gla_gate_b"]), _row(p["gla_norm_w"]), dmat, part_ids],
        [True] + [False] * 6, BRANCH_WIDTH, BF16, _gla_scratch(GLA_KDIM))

    eye = jnp.eye(LRU_BLOCKS, dtype=F32)
    bdiag = lambda wt: jnp.einsum("kde,kl->kdle", wt.astype(F32), eye).reshape(BRANCH_WIDTH, BRANCH_WIDTH)
    w_gates = jnp.concatenate([bdiag(p["lru_wa"]), bdiag(p["lru_wx"])], axis=1).astype(BF16)
    b_gates = _row(jnp.concatenate([p["lru_ba"], p["lru_bx"]]))
    yd = _pcall(
        functools.partial(_lru_kernel, bps), "rglru", n_tokens, T_STEP,
        [h, nw, w_in[:, c[12]:c[14]].astype(BF16), p["lru_conv_w"].astype(F32), _row(p["lru_conv_b"]), w_gates,
         b_gates, _row(p["lru_lambda"])],
        [True] + [False] * 7, BRANCH_WIDTH, BF16,
        [pltpu.VMEM((T_STEP + SUBLANES, BRANCH_WIDTH), F32), pltpu.VMEM((1, BRANCH_WIDTH), F32)]
        + [pltpu.VMEM((2 * BRANCH_WIDTH // LANES, T_BLK, LANES), F32),
           pltpu.VMEM((T_BLK // SUBLANES, BRANCH_WIDTH), F32)] * BLOCKS_PER_STEP)

    h = _pcall(
        _merge_kernel, "merge", n_tokens, T_STEP,
        [h, nw, ya, yb, yc, yd, w_in[:, c[14]:c[15]].astype(BF16), p["w_branch"].astype(BF16),
         p["w_out"].astype(BF16)],
        [True, False, True, True, True, True, False, False, False], D_MODEL, F32, [])

    h = _pcall(
        functools.partial(_ffn_kernel, p["final"]), "swiglu", n_tokens, T_STEP,
        [h, _row(p["norm_ffn_w"]), p["w_ffn_in"].astype(BF16), p["w_ffn_out"].astype(BF16), _row(p["norm_f_w"])],
        [True, False, False, False, False], D_MODEL, F32, [])
    return h


def kernel(x, norm_mix_w, w_in, hgrn_lower_bounds, hgrn_norm_w, ssd_conv_w, ssd_conv_b, ssd_dt_bias, ssd_a_log, ssd_d, ssd_norm_w, gla_gate_w, gla_gate_b, gla_norm_w, lru_conv_w, lru_conv_b, lru_wa, lru_ba, lru_wx, lru_bx, lru_lambda, w_branch, w_out, norm_ffn_w, w_ffn_in, w_ffn_out, norm_f_w):
    bsz, seq, _ = x.shape
    depth = w_in.shape[0]
    lb_all = jnp.cumsum(jax.nn.softmax(hgrn_lower_bounds.astype(F32), axis=0), axis=0)
    lb_all = lb_all - lb_all[0:1]
    h = x.reshape(bsz * seq, D_MODEL)
    for l in range(depth):
        p = dict(
            norm_mix_w=norm_mix_w[l], w_in=w_in[l], lb=lb_all[l], hgrn_norm_w=hgrn_norm_w[l],
            ssd_conv_w=ssd_conv_w[l], ssd_conv_b=ssd_conv_b[l], ssd_dt_bias=ssd_dt_bias[l], ssd_a_log=ssd_a_log[l],
            ssd_d=ssd_d[l], ssd_norm_w=ssd_norm_w[l], gla_gate_w=gla_gate_w[l], gla_gate_b=gla_gate_b[l],
            gla_norm_w=gla_norm_w[l], lru_conv_w=lru_conv_w[l], lru_conv_b=lru_conv_b[l], lru_wa=lru_wa[l],
            lru_ba=lru_ba[l], lru_wx=lru_wx[l], lru_bx=lru_bx[l], lru_lambda=lru_lambda[l], w_branch=w_branch[l],
            w_out=w_out[l], norm_ffn_w=norm_ffn_w[l], w_ffn_in=w_ffn_in[l], w_ffn_out=w_ffn_out[l],
            norm_f_w=norm_f_w, final=(l == depth - 1))
        h = _layer(h, seq, p)
    return h.reshape(bsz, seq, D_MODEL)
```

```python
import functools
import math

import jax
import jax.numpy as jnp
import numpy as np
from jax import lax
from jax.experimental import pallas as pl
from jax.experimental.pallas import tpu as pltpu

F32 = jnp.float32
BF16 = jnp.bfloat16

D_MODEL = 1024
BRANCH_WIDTH = 512
N_BRANCHES = 4
CONV_WIDTH = 4
NORM_EPS = 1e-6
HGRN_HEADS = 4
SSD_HEADS = 8
SSD_HEADDIM = 64
SSD_GROUPS = 2
SSD_STATE = 64
SSD_CONV_DIM = 768
GLA_HEADS = 4
GLA_KDIM = 256
GLA_HEAD_K = 64
GLA_GATE_RANK = 16
GLA_GATE_NORMALIZER = 16.0
LRU_BLOCKS = 8
LRU_BLOCK = 64
LRU_C = 8.0
D_FF = 2816

LANES = 128
SUBLANES = 8
T_BLK = 256
BLOCKS_PER_STEP = 2
T_STEP = BLOCKS_PER_STEP * T_BLK
HEAD_V = 128
VMEM_LIMIT = 48 * 1024 * 1024
F32_FLOOR = 1e-37


def _dot(a, b):
    return jnp.dot(a, b, preferred_element_type=F32)


def _dot_nt(a, b):
    return lax.dot_general(a, b, (((1,), (1,)), ((), ())), preferred_element_type=F32)


def _dot_tn(a, b):
    return lax.dot_general(a, b, (((0,), (0,)), ((), ())), preferred_element_type=F32)


def _split3(x):
    hi = x.astype(BF16)
    r1 = x - hi.astype(F32)
    mid = r1.astype(BF16)
    lo = (r1 - mid.astype(F32)).astype(BF16)
    return hi, mid, lo


def _sel_dot(mat, x):
    hi, mid, lo = _split3(x)
    return (_dot(mat, lo) + _dot(mat, mid)) + _dot(mat, hi)


def _dot_sel(x, mat):
    hi, mid, lo = _split3(x)
    return (_dot(lo, mat) + _dot(mid, mat)) + _dot(hi, mat)


def _sigmoid(x):
    return jax.nn.sigmoid(x)


def _silu(x):
    return x * _sigmoid(x)


def _softplus(x):
    return jnp.maximum(x, 0.0) + jnp.log1p(jnp.exp(-jnp.abs(x)))


def _log_sigmoid(x):
    return jnp.minimum(x, 0.0) - jnp.log1p(jnp.exp(-jnp.abs(x)))


def _rmsnorm(x, w):
    return x * lax.rsqrt(jnp.mean(x * x, axis=-1, keepdims=True) + NORM_EPS) * w


GLA_SMALL = (1, 2, 4)
GLA_MID = (8, 16, 32, 64)
GLA_PARTS = 2 + len(GLA_SMALL) + len(GLA_MID) + 1
LOG2E = 1.4426950408889634


def _gla_part_ids():
    row = np.arange(LANES)[:, None]
    col = np.arange(LANES)[None, :]
    x = row ^ col
    ids = np.where(x == 0, 0, np.floor(np.log2(np.maximum(x, 1))).astype(np.int64) + 1)
    return jnp.asarray(np.where(col <= row, ids, -1), jnp.int32)


def _gla_decay_matrix(t):
    u = np.arange(t)[None, :]
    r = np.arange(t)[:, None]
    blocks = [u <= r]
    for s in GLA_SMALL:
        mid = (r // (2 * s)) * (2 * s) + s - 1
        blocks.append(((u > mid) & (u <= r)) | ((u > r) & (u <= mid)))
    m = np.concatenate(blocks, axis=0)
    return jnp.asarray(np.concatenate([m, m], axis=1), BF16)


def _interleave(*chains):
    live = list(chains)
    while live:
        for c in list(live):
            try:
                next(c)
            except StopIteration:
                live.remove(c)


def _gla_chunk(q, k, logf, v_ref, st_ref, ql_ref, kl_ref, rhs_ref, dm_ref, ids_ref, hpt, emit):
    t, nk = q.shape
    assert t == 2 * LANES
    ntile = nk // LANES
    dk = LANES // hpt

    lf2 = logf * LOG2E
    hi = lf2.astype(BF16)
    rhs_ref[0:t, :] = hi
    rhs_ref[t:2 * t, :] = (lf2 - hi.astype(F32)).astype(BF16)
    d4 = _dot(dm_ref[...], rhs_ref[...])
    bcum = d4[0:t]
    blast = bcum[t - 1:t, :]
    qb = q.astype(BF16)
    kb = k.astype(BF16)
    ql_ref[0] = qb * jnp.exp2(bcum).astype(BF16)
    kl_ref[0] = kb * jnp.exp2(blast - bcum).astype(BF16)
    part = 1
    for i in range(len(GLA_SMALL)):
        e = jnp.exp2(d4[(i + 1) * t:(i + 2) * t]).astype(BF16)
        ql_ref[part] = qb * e
        kl_ref[part] = kb * e
        part += 1
    yield
    for s in GLA_MID + (LANES,):
        bc3 = bcum.reshape(t // (2 * s), 2 * s, nk)
        ref = bc3[:, s - 1:s, :]
        e = jnp.exp2(jnp.concatenate([ref - bc3[:, 0:s, :], bc3[:, s:2 * s, :] - ref], axis=1))
        e = e.reshape(t, nk).astype(BF16)
        ql_ref[part] = qb * e
        kl_ref[part] = kb * e
        part += 1
    ql_ref[part] = qb
    kl_ref[part] = kb
    p_unscaled, p_top = part, part - 1
    masked = [(0, p_unscaled)] + [(1 + i, 1 + i) for i in range(len(GLA_SMALL) + len(GLA_MID))]
    yield

    ids = ids_ref[...]
    lane = lax.broadcasted_iota(jnp.int32, (1, LANES), 1)

    def own(x, sub):
        return x if hpt == 1 else jnp.where((lane // dk) == sub, x, jnp.zeros_like(x))

    def stacked(x):
        return x if hpt == 1 else jnp.concatenate([own(x, sub) for sub in range(hpt)], axis=0)

    ids_st = ids if hpt == 1 else jnp.concatenate([ids] * hpt, axis=0)
    intra = []
    for tau in range(ntile):
        tl = slice(tau * LANES, (tau + 1) * LANES)
        diag = []
        for c in range(2):
            rows = slice(c * LANES, (c + 1) * LANES)
            sc = jnp.zeros((hpt * LANES, LANES), F32)
            for pid, p in masked:
                sc = jnp.where(ids_st == pid, _dot_nt(stacked(ql_ref[p, rows, tl]), kl_ref[p, rows, tl]), sc)
            diag.append(sc.astype(BF16))
        low = _dot_nt(stacked(ql_ref[p_top, LANES:t, tl]), kl_ref[p_top, 0:LANES, tl]).astype(BF16)
        for sub in range(hpt):
            hv = slice((tau * hpt + sub) * HEAD_V, (tau * hpt + sub + 1) * HEAD_V)
            hr = slice(sub * LANES, (sub + 1) * LANES)
            o_top = _dot(diag[0][hr], v_ref[0:LANES, hv])
            o_bot = _dot(jnp.concatenate([low[hr], diag[1][hr]], axis=1), v_ref[:, hv])
            intra.append(jnp.concatenate([o_top, o_bot], axis=0))
        yield
    for tau in range(ntile):
        tl = slice(tau * LANES, (tau + 1) * LANES)
        inter = _dot_nt(stacked(ql_ref[0, :, tl]), st_ref[tau].astype(BF16))
        for sub in range(hpt):
            h = tau * hpt + sub
            emit(h, intra[h] + inter[sub * t:(sub + 1) * t])
    for tau in range(ntile):
        tl = slice(tau * LANES, (tau + 1) * LANES)
        kd = kl_ref[0, :, tl]
        upd = _dot_tn(v_ref[:, tau * hpt * HEAD_V:(tau * hpt + 1) * HEAD_V], kd)
        for sub in range(1, hpt):
            h = tau * hpt + sub
            upd = jnp.where((lane // dk) == sub, _dot_tn(v_ref[:, h * HEAD_V:(h + 1) * HEAD_V], kd), upd)
        st_ref[tau] = st_ref[tau] * jnp.exp2(blast[:, tl]) + upd


def _causal_conv(buf_ref, x, w_ref, b_ref, base):
    t = x.shape[0]
    buf_ref[SUBLANES + base:SUBLANES + base + t, :] = x
    acc = b_ref[...] + w_ref[CONV_WIDTH - 1:CONV_WIDTH, :] * x
    for s in range(1, CONV_WIDTH):
        acc = acc + w_ref[CONV_WIDTH - 1 - s:CONV_WIDTH - s, :] * buf_ref[pl.ds(SUBLANES + base - s, t), :]
    return acc


def _step_start(bps, *carry_refs):
    @pl.when((pl.program_id(0) * BLOCKS_PER_STEP) % bps == 0)
    def _():
        for ref in carry_refs:
            ref[...] = jnp.zeros_like(ref)


def _conv_carry(buf_ref):
    buf_ref[0:SUBLANES, :] = buf_ref[T_STEP:T_STEP + SUBLANES, :]


def _hgrn_kernel(bps, xn_ref, w_ref, lb_ref, oml_ref, hnw_ref, dm_ref, ids_ref,
                 o_ref, st_ref, *scratch):
    _step_start(bps, st_ref)
    w = BRANCH_WIDTH
    staged = []
    for blk in range(BLOCKS_PER_STEP):
        rows = slice(blk * T_BLK, (blk + 1) * T_BLK)
        ql_ref, kl_ref, rhs_ref, v_ref = scratch[4 * blk:4 * blk + 4]
        proj = _dot(xn_ref[rows, :], w_ref[...])
        fp = proj[:, w:2 * w]
        u = jnp.exp(-jnp.abs(fp))
        r = 1.0 / (1.0 + u)
        pos = fp >= 0.0
        sig = jnp.where(pos, r, u * r)
        key = oml_ref[...] * jnp.where(pos, u * r, r)
        logf = jnp.log(jnp.maximum(lb_ref[...] + oml_ref[...] * sig, F32_FLOOR))
        qf = _silu(proj[:, 0:w])
        v_ref[...] = proj[:, 2 * w:3 * w].astype(BF16)
        staged.append((qf, key, logf, _silu(proj[:, 3 * w:4 * w])))
    chains = []
    for blk in range(BLOCKS_PER_STEP):
        rows = slice(blk * T_BLK, (blk + 1) * T_BLK)
        ql_ref, kl_ref, rhs_ref, v_ref = scratch[4 * blk:4 * blk + 4]
        qf, key, logf, gs = staged[blk]

        def emit(h, o, rows=rows, gs=gs):
            cols = slice(h * HEAD_V, (h + 1) * HEAD_V)
            o_ref[rows, cols] = (_rmsnorm(o, hnw_ref[...]) * gs[:, cols]).astype(o_ref.dtype)

        chains.append(_gla_chunk(qf, key, logf, v_ref, st_ref, ql_ref, kl_ref, rhs_ref, dm_ref, ids_ref, 1, emit))
    _interleave(*chains)


def _gla_kernel(bps, xn_ref, w_ref, gb_ref, gnw_ref, dm_ref, ids_ref, o_ref, st_ref, *scratch):
    _step_start(bps, st_ref)
    kd, w = GLA_KDIM, BRANCH_WIDTH
    chains = []
    for blk in range(BLOCKS_PER_STEP):
        rows = slice(blk * T_BLK, (blk + 1) * T_BLK)
        ql_ref, kl_ref, rhs_ref, v_ref = scratch[4 * blk:4 * blk + 4]
        proj = _dot(xn_ref[rows, :], w_ref[...])
        logf = _log_sigmoid(proj[:, 0:kd] + gb_ref[...]) * (1.0 / GLA_GATE_NORMALIZER)
        proj = proj[:, kd:]
        q = proj[:, 0:kd] * (GLA_HEAD_K ** -0.5)
        k = proj[:, kd:2 * kd]
        v_ref[...] = proj[:, 2 * kd:2 * kd + w].astype(BF16)
        gs = _silu(proj[:, 2 * kd + w:2 * kd + 2 * w])

        def emit(h, o, rows=rows, gs=gs):
            cols = slice(h * HEAD_V, (h + 1) * HEAD_V)
            o_ref[rows, cols] = (_rmsnorm(o, gnw_ref[...]) * gs[:, cols]).astype(o_ref.dtype)

        chains.append(_gla_chunk(q, k, logf, v_ref, st_ref, ql_ref, kl_ref, rhs_ref, dm_ref, ids_ref, 2, emit))
    _interleave(*chains)


def _ssd_kernel(bps, xn_ref, w_ref, cw_ref, cb_ref, dtb_ref, aneg_ref, dsk_ref, snw_ref, tril_ref,
                exp_ref, o_ref, buf_ref, st_ref, *y_refs):
    _step_start(bps, st_ref, buf_ref)
    for blk in range(BLOCKS_PER_STEP):
        _ssd_block(blk, xn_ref, w_ref, cw_ref, cb_ref, dtb_ref, aneg_ref, dsk_ref, snw_ref, tril_ref,
                   exp_ref, o_ref, buf_ref, st_ref, y_refs[blk])
    _conv_carry(buf_ref)


def _ssd_block(blk, xn_ref, w_ref, cw_ref, cb_ref, dtb_ref, aneg_ref, dsk_ref, snw_ref, tril_ref,
               exp_ref, o_ref, buf_ref, st_ref, y_ref):
    t = T_BLK
    rows = slice(blk * t, (blk + 1) * t)
    w = BRANCH_WIDTH
    gs = SSD_GROUPS * SSD_STATE
    proj = _dot(xn_ref[rows, :], w_ref[...])
    z = proj[:, 0:w]
    xbc = _silu(_causal_conv(buf_ref, proj[:, w:w + SSD_CONV_DIM], cw_ref, cb_ref, blk * t))
    xs = xbc[:, 0:w]
    bm = xbc[:, w:w + gs].astype(BF16)
    cm = xbc[:, w + gs:w + 2 * gs]
    dt = _softplus(proj[:, w + SSD_CONV_DIM:] + dtb_ref[...])
    acs = _sel_dot(tril_ref[...], dt * aneg_ref[...])
    dt_e = _dot_sel(dt, exp_ref[...])
    acs_e = _dot_sel(acs, exp_ref[...])
    alast_e = acs_e[t - 1:t, :]
    xdt = xs * dt_e
    xdt_bf = xdt.astype(BF16)

    row = lax.broadcasted_iota(jnp.int32, (t, t), 0)
    col = lax.broadcasted_iota(jnp.int32, (t, t), 1)
    causal = col <= row
    lane = lax.broadcasted_iota(jnp.int32, (1, LANES), 1)
    acs_t = acs.T
    gmats = []
    for g in range(SSD_GROUPS):
        cg = jnp.where((lane // SSD_STATE) == g, cm, 0.0).astype(BF16)
        gmats.append(_dot_nt(cg, bm))
    hpg = SSD_HEADS // SSD_GROUPS
    for pair in range(SSD_HEADS // 2):
        tl = slice(pair * LANES, (pair + 1) * LANES)
        ys = []
        for sub in range(2):
            h = 2 * pair + sub
            seg = jnp.minimum(acs[:, h:h + 1] - acs_t[h:h + 1, :], 0.0)
            m = jnp.where(causal, gmats[h // hpg] * jnp.exp(seg), 0.0).astype(BF16)
            ys.append(_dot(m, xdt_bf[:, tl]))
        y_diag = jnp.where(lane < SSD_HEADDIM, ys[0], ys[1])
        y_off = _dot(cm.astype(BF16), st_ref[:, tl].astype(BF16)) * jnp.exp(acs_e[:, tl])
        y = y_diag + y_off + xs[:, tl] * dsk_ref[:, tl]
        y_ref[:, tl] = y * _silu(z[:, tl])
    xdec = (xdt * jnp.exp(alast_e - acs_e)).astype(BF16)
    upd = _dot_tn(bm, xdec)
    srow = lax.broadcasted_iota(jnp.int32, (gs, w), 0) // SSD_STATE
    scol = lax.broadcasted_iota(jnp.int32, (gs, w), 1) // (hpg * SSD_HEADDIM)
    st_ref[...] = st_ref[...] * jnp.exp(alast_e) + jnp.where(srow == scol, upd, 0.0)
    gw = w // SSD_GROUPS
    for g in range(SSD_GROUPS):
        yg = y_ref[:, g * gw:(g + 1) * gw]
        o_ref[rows, g * gw:(g + 1) * gw] = (
            yg * lax.rsqrt(jnp.mean(yg * yg, axis=-1, keepdims=True) + NORM_EPS) * snw_ref[:, g * gw:(g + 1) * gw]
        ).astype(o_ref.dtype)


def _sqrt_one_minus_exp2x(x):
    th = jnp.tanh(x)
    y = (-2.0 * th) / (1.0 - th)
    return y * lax.rsqrt(jnp.maximum(y, 1e-30))


def _doubling_scan(a, b, pos, n, axis):
    off = 1
    while off < n:
        keep = pos >= off
        a_sh = jnp.where(keep, pltpu.roll(a, off, axis), 1.0)
        b_sh = jnp.where(keep, pltpu.roll(b, off, axis), 0.0)
        b = a * b_sh + b
        a = a * a_sh
        off *= 2
    return a, b


def _scan_rows(a, b, hc_ref, ab_ref, cg_ref):
    t, n = a.shape
    g = t // SUBLANES
    nt = n // LANES
    a8, b8 = _doubling_scan(a.reshape(g, SUBLANES, n), b.reshape(g, SUBLANES, n),
                            lax.broadcasted_iota(jnp.int32, (1, SUBLANES, 1), 1), SUBLANES, 1)
    a8 = a8.reshape(t, n)
    b8 = b8.reshape(t, n)
    for i in range(nt):
        ab_ref[i] = a8[:, i * LANES:(i + 1) * LANES]
        ab_ref[nt + i] = b8[:, i * LANES:(i + 1) * LANES]
    last = pl.ds(SUBLANES - 1, g, stride=SUBLANES)
    gidx = lax.broadcasted_iota(jnp.int32, (g, 1), 0)
    ag = jnp.concatenate([ab_ref[i, last, :] for i in range(nt)], axis=1)
    bg = jnp.concatenate([ab_ref[nt + i, last, :] for i in range(nt)], axis=1)
    ag, bg = _doubling_scan(ag, bg, gidx, g, 0)
    state = bg + ag * hc_ref[...]
    cg_ref[...] = jnp.where(gidx >= 1, pltpu.roll(state, 1, 0), hc_ref[...])
    hc_ref[...] = state[g - 1:g, :]
    out = []
    for i in range(nt):
        cg = cg_ref[:, i * LANES:(i + 1) * LANES].reshape(g, 1, LANES)
        hi = ab_ref[nt + i].reshape(g, SUBLANES, LANES) + ab_ref[i].reshape(g, SUBLANES, LANES) * cg
        out.append(hi.reshape(t, LANES))
    return jnp.concatenate(out, axis=1)


def _lru_kernel(bps, xn_ref, w_ref, cw_ref, cb_ref, wg_ref, bg_ref, lam_ref, o_ref, buf_ref, hc_ref,
                *scratch):
    _step_start(bps, hc_ref, buf_ref)
    for blk in range(BLOCKS_PER_STEP):
        _lru_block(blk, xn_ref, w_ref, cw_ref, cb_ref, wg_ref, bg_ref, lam_ref, o_ref, buf_ref, hc_ref,
                   *scratch[2 * blk:2 * blk + 2])
    _conv_carry(buf_ref)


def _lru_block(blk, xn_ref, w_ref, cw_ref, cb_ref, wg_ref, bg_ref, lam_ref, o_ref, buf_ref, hc_ref,
               ab_ref, cg_ref):
    t = T_BLK
    rows = slice(blk * t, (blk + 1) * t)
    w = BRANCH_WIDTH
    proj = _dot(xn_ref[rows, :], w_ref[...])
    u = _causal_conv(buf_ref, proj[:, 0:w], cw_ref, cb_ref, blk * t)
    rg = _dot(u.astype(BF16), wg_ref[...]) + bg_ref[...]
    r = _sigmoid(rg[:, 0:w])
    ig = _sigmoid(rg[:, w:2 * w])
    log_a = (-LRU_C) * r * _softplus(-lam_ref[...])
    a = jnp.exp(log_a)
    b = _sqrt_one_minus_exp2x(log_a) * (ig * u)
    hseq = _scan_rows(a, b, hc_ref, ab_ref, cg_ref)
    gate = proj[:, w:2 * w]
    gelu = 0.5 * gate * (1.0 + jnp.tanh(math.sqrt(2.0 / math.pi) * (gate + 0.044715 * gate * gate * gate)))
    o_ref[rows, :] = (hseq * gelu).astype(o_ref.dtype)


def _merge_kernel(h_ref, xn_ref, ya_ref, yb_ref, yc_ref, yd_ref, wmg_ref, wbr_ref, wout_ref, o_ref):
    xn = xn_ref[...]
    merged = None
    for n, y_ref in enumerate((ya_ref, yb_ref, yc_ref, yd_ref)):
        gate = _sigmoid(_dot(xn, wmg_ref[:, n * D_MODEL:(n + 1) * D_MODEL]))
        term = gate * _dot(y_ref[...], wbr_ref[n])
        merged = term if merged is None else merged + term
    o_ref[...] = h_ref[...] + _dot(merged.astype(BF16), wout_ref[...])


def _ffn_kernel(final, h_ref, nw_ref, win_ref, wout_ref, nnw_ref, o_ref, *xn_out):
    hres = h_ref[...]
    xn = _rmsnorm(hres, nw_ref[...]).astype(BF16)
    gu = _dot(xn, win_ref[...])
    act = (_silu(gu[:, 0:D_FF]) * gu[:, D_FF:2 * D_FF]).astype(BF16)
    out = hres + _dot(act, wout_ref[...])
    if final:
        o_ref[...] = _rmsnorm(out, nnw_ref[...])
    else:
        o_ref[...] = out
        xn_out[0][...] = _rmsnorm(out, nnw_ref[...]).astype(BF16)


def _norm_kernel(x_ref, nw_ref, o_ref):
    o_ref[...] = _rmsnorm(x_ref[...], nw_ref[...]).astype(o_ref.dtype)


def _const_spec(shape):
    nd = len(shape)
    return pl.BlockSpec(shape, lambda i, _nd=nd: (0,) * _nd, pipeline_mode=pl.Buffered(1))


def _row_spec(rows, width):
    return pl.BlockSpec((rows, width), lambda i: (i, 0))


def _pcall(kernel, name, n_tokens, rows, operands, row_flags, outs, scratch):
    in_specs = [(_row_spec(rows, a.shape[1]) if is_row else _const_spec(a.shape))
                for a, is_row in zip(operands, row_flags)]
    many = isinstance(outs, list)
    out_specs = [_row_spec(rows, wd) for wd, _ in (outs if many else [outs])]
    out_shape = [jax.ShapeDtypeStruct((n_tokens, wd), dt) for wd, dt in (outs if many else [outs])]
    return pl.pallas_call(
        kernel,
        name=name,
        grid=(n_tokens // rows,),
        in_specs=in_specs,
        out_specs=out_specs if many else out_specs[0],
        out_shape=out_shape if many else out_shape[0],
        scratch_shapes=scratch,
        compiler_params=pltpu.CompilerParams(dimension_semantics=("arbitrary",), vmem_limit_bytes=VMEM_LIMIT),
    )(*operands)


def _tril_const(t):
    return jnp.asarray(np.tril(np.ones((t, t), np.float32)), BF16)


def _gla_scratch(nk):
    ntile = nk // LANES
    per_block = [
        pltpu.VMEM((GLA_PARTS, T_BLK, nk), BF16),
        pltpu.VMEM((GLA_PARTS, T_BLK, nk), BF16),
        pltpu.VMEM((2 * T_BLK, nk), BF16),
        pltpu.VMEM((T_BLK, BRANCH_WIDTH), BF16),
    ]
    return [pltpu.VMEM((ntile, HEAD_V, LANES), F32)] + per_block * BLOCKS_PER_STEP


def _row(x):
    return x.reshape(1, -1).astype(F32)


def _layer(h, xn, seq, p):
    n_tokens = h.shape[0]
    bps = seq // T_BLK
    assert seq % T_STEP == 0 and n_tokens % seq == 0
    tril = _tril_const(T_BLK)
    dmat = _gla_decay_matrix(T_BLK)
    part_ids = _gla_part_ids()
    w_in = p["w_in"]
    splits = np.cumsum([512, 512, 512, 512, 512, 768, 8, 256, 256, 512, 512, 16, 512, 512, 4096])
    c = [0] + splits.tolist()

    lb = p["lb"].astype(F32)
    ya = _pcall(
        functools.partial(_hgrn_kernel, bps), "hgrn2", n_tokens, T_STEP,
        [xn, w_in[:, c[0]:c[4]].astype(BF16), _row(lb), _row(1.0 - lb), _row(p["hgrn_norm_w"]), dmat, part_ids],
        [True] + [False] * 6, (BRANCH_WIDTH, BF16), _gla_scratch(BRANCH_WIDTH))

    w_ssd = jnp.concatenate([w_in[:, c[4]:c[7]], jnp.zeros((D_MODEL, LANES - SSD_HEADS), F32)], axis=1).astype(BF16)
    pad8 = lambda v: jnp.concatenate([v.astype(F32), jnp.zeros((LANES - SSD_HEADS,), F32)]).reshape(1, LANES)
    expand = jnp.asarray(np.arange(LANES)[:, None] == (np.arange(BRANCH_WIDTH)[None, :] // SSD_HEADDIM), BF16)
    yb = _pcall(
        functools.partial(_ssd_kernel, bps), "ssd", n_tokens, T_STEP,
        [xn, w_ssd, p["ssd_conv_w"].astype(F32), _row(p["ssd_conv_b"]), pad8(p["ssd_dt_bias"]),
         pad8(-jnp.exp(p["ssd_a_log"].astype(F32))), _row(jnp.repeat(p["ssd_d"], SSD_HEADDIM)), _row(p["ssd_norm_w"]),
         tril, expand],
        [True] + [False] * 9, (BRANCH_WIDTH, BF16),
        [pltpu.VMEM((T_STEP + SUBLANES, SSD_CONV_DIM), F32), pltpu.VMEM((SSD_GROUPS * SSD_STATE, BRANCH_WIDTH), F32)]
        + [pltpu.VMEM((T_BLK, BRANCH_WIDTH), F32)] * BLOCKS_PER_STEP)

    w_gate = jnp.dot(w_in[:, c[11]:c[12]].astype(F32), p["gla_gate_w"].astype(F32), precision=lax.Precision.HIGHEST)
    w_gla = jnp.concatenate([w_gate, w_in[:, c[7]:c[11]]], axis=1).astype(BF16)
    yc = _pcall(
        functools.partial(_gla_kernel, bps), "gla", n_tokens, T_STEP,
        [xn, w_gla, _row(p["gla_gate_b"]), _row(p["gla_norm_w"]), dmat, part_ids],
        [True] + [False] * 5, (BRANCH_WIDTH, BF16), _gla_scratch(GLA_KDIM))

    eye = jnp.eye(LRU_BLOCKS, dtype=F32)
    bdiag = lambda wt: jnp.einsum("kde,kl->kdle", wt.astype(F32), eye).reshape(BRANCH_WIDTH, BRANCH_WIDTH)
    w_gates = jnp.concatenate([bdiag(p["lru_wa"]), bdiag(p["lru_wx"])], axis=1).astype(BF16)
    b_gates = _row(jnp.concatenate([p["lru_ba"], p["lru_bx"]]))
    yd = _pcall(
        functools.partial(_lru_kernel, bps), "rglru", n_tokens, T_STEP,
        [xn, w_in[:, c[12]:c[14]].astype(BF16), p["lru_conv_w"].astype(F32), _row(p["lru_conv_b"]), w_gates,
         b_gates, _row(p["lru_lambda"])],
        [True] + [False] * 6, (BRANCH_WIDTH, BF16),
        [pltpu.VMEM((T_STEP + SUBLANES, BRANCH_WIDTH), F32), pltpu.VMEM((1, BRANCH_WIDTH), F32)]
        + [pltpu.VMEM((2 * BRANCH_WIDTH // LANES, T_BLK, LANES), F32),
           pltpu.VMEM((T_BLK // SUBLANES, BRANCH_WIDTH), F32)] * BLOCKS_PER_STEP)

    h = _pcall(
        _merge_kernel, "merge", n_tokens, T_STEP,
        [h, xn, ya, yb, yc, yd, w_in[:, c[14]:c[15]].astype(BF16), p["w_branch"].astype(BF16),
         p["w_out"].astype(BF16)],
        [True, True, True, True, True, True, False, False, False], (D_MODEL, F32), [])

    outs = (D_MODEL, F32) if p["final"] else [(D_MODEL, F32), (D_MODEL, BF16)]
    res = _pcall(
        functools.partial(_ffn_kernel, p["final"]), "swiglu", n_tokens, T_STEP,
        [h, _row(p["norm_ffn_w"]), p["w_ffn_in"].astype(BF16), p["w_ffn_out"].astype(BF16), _row(p["next_norm_w"])],
        [True, False, False, False, False], outs, [])
    return (res, None) if p["final"] else tuple(res)


def kernel(x, norm_mix_w, w_in, hgrn_lower_bounds, hgrn_norm_w, ssd_conv_w, ssd_conv_b, ssd_dt_bias, ssd_a_log, ssd_d, ssd_norm_w, gla_gate_w, gla_gate_b, gla_norm_w, lru_conv_w, lru_conv_b, lru_wa, lru_ba, lru_wx, lru_bx, lru_lambda, w_branch, w_out, norm_ffn_w, w_ffn_in, w_ffn_out, norm_f_w):
    bsz, seq, _ = x.shape
    depth = w_in.shape[0]
    lb_all = jnp.cumsum(jax.nn.softmax(hgrn_lower_bounds.astype(F32), axis=0), axis=0)
    lb_all = lb_all - lb_all[0:1]
    h = x.reshape(bsz * seq, D_MODEL)
    xn = _pcall(_norm_kernel, "norm", bsz * seq, T_STEP, [h, _row(norm_mix_w[0])], [True, False], (D_MODEL, BF16), [])
    for l in range(depth):
        p = dict(
            w_in=w_in[l], lb=lb_all[l], hgrn_norm_w=hgrn_norm_w[l],
            ssd_conv_w=ssd_conv_w[l], ssd_conv_b=ssd_conv_b[l], ssd_dt_bias=ssd_dt_bias[l], ssd_a_log=ssd_a_log[l],
            ssd_d=ssd_d[l], ssd_norm_w=ssd_norm_w[l], gla_gate_w=gla_gate_w[l], gla_gate_b=gla_gate_b[l],
            gla_norm_w=gla_norm_w[l], lru_conv_w=lru_conv_w[l], lru_conv_b=lru_conv_b[l], lru_wa=lru_wa[l],
            lru_ba=lru_ba[l], lru_wx=lru_wx[l], lru_bx=lru_bx[l], lru_lambda=lru_lambda[l], w_branch=w_branch[l],
            w_out=w_out[l], norm_ffn_w=norm_ffn_w[l], w_ffn_in=w_ffn_in[l], w_ffn_out=w_ffn_out[l],
            next_norm_w=(norm_f_w if l == depth - 1 else norm_mix_w[l + 1]), final=(l == depth - 1))
        h, xn = _layer(h, xn, seq, p)
    return h.reshape(bsz, seq, D_MODEL)
```

```python
import functools
import math

import jax
import jax.numpy as jnp
import numpy as np
from jax import lax
from jax.experimental import pallas as pl
from jax.experimental.pallas import tpu as pltpu

F32 = jnp.float32
BF16 = jnp.bfloat16

D_MODEL = 1024
BRANCH_WIDTH = 512
N_BRANCHES = 4
CONV_WIDTH = 4
NORM_EPS = 1e-6
HGRN_HEADS = 4
SSD_HEADS = 8
SSD_HEADDIM = 64
SSD_GROUPS = 2
SSD_STATE = 64
SSD_CONV_DIM = 768
GLA_HEADS = 4
GLA_KDIM = 256
GLA_HEAD_K = 64
GLA_GATE_RANK = 16
GLA_GATE_NORMALIZER = 16.0
LRU_BLOCKS = 8
LRU_BLOCK = 64
LRU_C = 8.0
D_FF = 2816

LANES = 128
SUBLANES = 8
T_BLK = 256
BLOCKS_PER_STEP = 2
T_STEP = BLOCKS_PER_STEP * T_BLK
HEAD_V = 128
VMEM_LIMIT = 48 * 1024 * 1024
F32_FLOOR = 1e-37
SSD_PROJ = BRANCH_WIDTH + SSD_CONV_DIM + LANES
GLA_PROJ = 3 * GLA_KDIM + 2 * BRANCH_WIDTH


def _dot(a, b):
    return jnp.dot(a, b, preferred_element_type=F32)


def _dot_nt(a, b):
    return lax.dot_general(a, b, (((1,), (1,)), ((), ())), preferred_element_type=F32)


def _dot_tn(a, b):
    return lax.dot_general(a, b, (((0,), (0,)), ((), ())), preferred_element_type=F32)


def _split3(x):
    hi = x.astype(BF16)
    r1 = x - hi.astype(F32)
    mid = r1.astype(BF16)
    lo = (r1 - mid.astype(F32)).astype(BF16)
    return hi, mid, lo


def _sel_dot(mat, x):
    hi, mid, lo = _split3(x)
    return (_dot(mat, lo) + _dot(mat, mid)) + _dot(mat, hi)


def _dot_sel(x, mat):
    hi, mid, lo = _split3(x)
    return (_dot(lo, mat) + _dot(mid, mat)) + _dot(hi, mat)


def _sigmoid(x):
    return jax.nn.sigmoid(x)


def _silu(x):
    return x * _sigmoid(x)


def _softplus(x):
    return jnp.maximum(x, 0.0) + jnp.log1p(jnp.exp(-jnp.abs(x)))


def _log_sigmoid(x):
    return jnp.minimum(x, 0.0) - jnp.log1p(jnp.exp(-jnp.abs(x)))


def _rmsnorm(x, w):
    return x * lax.rsqrt(jnp.mean(x * x, axis=-1, keepdims=True) + NORM_EPS) * w


GLA_SMALL = (1, 2, 4)
GLA_MID = (8, 16, 32, 64)
GLA_PARTS = 2 + len(GLA_SMALL) + len(GLA_MID) + 1
LOG2E = 1.4426950408889634


def _gla_part_ids():
    row = np.arange(LANES)[:, None]
    col = np.arange(LANES)[None, :]
    x = row ^ col
    ids = np.where(x == 0, 0, np.floor(np.log2(np.maximum(x, 1))).astype(np.int64) + 1)
    return jnp.asarray(np.where(col <= row, ids, -1), jnp.int32)


def _gla_decay_matrix(t):
    u = np.arange(t)[None, :]
    r = np.arange(t)[:, None]
    blocks = [u <= r]
    for s in GLA_SMALL:
        mid = (r // (2 * s)) * (2 * s) + s - 1
        blocks.append(((u > mid) & (u <= r)) | ((u > r) & (u <= mid)))
    m = np.concatenate(blocks, axis=0)
    return jnp.asarray(np.concatenate([m, m], axis=1), BF16)


def _interleave(*chains):
    live = list(chains)
    while live:
        for c in list(live):
            try:
                next(c)
            except StopIteration:
                live.remove(c)


def _gla_chunk(q, k, logf, v_ref, st_ref, ql_ref, kl_ref, rhs_ref, dm_ref, ids_ref, hpt, emit):
    t, nk = q.shape
    assert t == 2 * LANES
    ntile = nk // LANES
    dk = LANES // hpt

    lf2 = logf * LOG2E
    hi = lf2.astype(BF16)
    rhs_ref[0:t, :] = hi
    rhs_ref[t:2 * t, :] = (lf2 - hi.astype(F32)).astype(BF16)
    d4 = _dot(dm_ref[...], rhs_ref[...])
    bcum = d4[0:t]
    blast = bcum[t - 1:t, :]
    qb = q.astype(BF16)
    kb = k.astype(BF16)
    ql_ref[0] = qb * jnp.exp2(bcum).astype(BF16)
    kl_ref[0] = kb * jnp.exp2(blast - bcum).astype(BF16)
    part = 1
    for i in range(len(GLA_SMALL)):
        e = jnp.exp2(d4[(i + 1) * t:(i + 2) * t]).astype(BF16)
        ql_ref[part] = qb * e
        kl_ref[part] = kb * e
        part += 1
    yield
    for s in GLA_MID + (LANES,):
        bc3 = bcum.reshape(t // (2 * s), 2 * s, nk)
        ref = bc3[:, s - 1:s, :]
        e = jnp.exp2(jnp.concatenate([ref - bc3[:, 0:s, :], bc3[:, s:2 * s, :] - ref], axis=1))
        e = e.reshape(t, nk).astype(BF16)
        ql_ref[part] = qb * e
        kl_ref[part] = kb * e
        part += 1
    ql_ref[part] = qb
    kl_ref[part] = kb
    p_unscaled, p_top = part, part - 1
    masked = [(0, p_unscaled)] + [(1 + i, 1 + i) for i in range(len(GLA_SMALL) + len(GLA_MID))]
    yield

    ids = ids_ref[...]
    lane = lax.broadcasted_iota(jnp.int32, (1, LANES), 1)

    def own(x, sub):
        return x if hpt == 1 else jnp.where((lane // dk) == sub, x, jnp.zeros_like(x))

    def stacked(x):
        return x if hpt == 1 else jnp.concatenate([own(x, sub) for sub in range(hpt)], axis=0)

    ids_st = ids if hpt == 1 else jnp.concatenate([ids] * hpt, axis=0)
    intra = []
    for tau in range(ntile):
        tl = slice(tau * LANES, (tau + 1) * LANES)
        diag = []
        for c in range(2):
            rows = slice(c * LANES, (c + 1) * LANES)
            sc = jnp.zeros((hpt * LANES, LANES), F32)
            for pid, p in masked:
                sc = jnp.where(ids_st == pid, _dot_nt(stacked(ql_ref[p, rows, tl]), kl_ref[p, rows, tl]), sc)
            diag.append(sc.astype(BF16))
        low = _dot_nt(stacked(ql_ref[p_top, LANES:t, tl]), kl_ref[p_top, 0:LANES, tl]).astype(BF16)
        for sub in range(hpt):
            hv = slice((tau * hpt + sub) * HEAD_V, (tau * hpt + sub + 1) * HEAD_V)
            hr = slice(sub * LANES, (sub + 1) * LANES)
            o_top = _dot(diag[0][hr], v_ref[0:LANES, hv])
            o_bot = _dot(jnp.concatenate([low[hr], diag[1][hr]], axis=1), v_ref[:, hv])
            intra.append(jnp.concatenate([o_top, o_bot], axis=0))
        yield
    for tau in range(ntile):
        tl = slice(tau * LANES, (tau + 1) * LANES)
        inter = _dot_nt(stacked(ql_ref[0, :, tl]), st_ref[tau].astype(BF16))
        for sub in range(hpt):
            h = tau * hpt + sub
            emit(h, intra[h] + inter[sub * t:(sub + 1) * t])
    for tau in range(ntile):
        tl = slice(tau * LANES, (tau + 1) * LANES)
        kd = kl_ref[0, :, tl]
        upd = _dot_tn(v_ref[:, tau * hpt * HEAD_V:(tau * hpt + 1) * HEAD_V], kd)
        for sub in range(1, hpt):
            h = tau * hpt + sub
            upd = jnp.where((lane // dk) == sub, _dot_tn(v_ref[:, h * HEAD_V:(h + 1) * HEAD_V], kd), upd)
        st_ref[tau] = st_ref[tau] * jnp.exp2(blast[:, tl]) + upd


def _causal_conv(buf_ref, x, w_ref, b_ref, base):
    t = x.shape[0]
    buf_ref[SUBLANES + base:SUBLANES + base + t, :] = x
    acc = b_ref[...] + w_ref[CONV_WIDTH - 1:CONV_WIDTH, :] * x
    for s in range(1, CONV_WIDTH):
        acc = acc + w_ref[CONV_WIDTH - 1 - s:CONV_WIDTH - s, :] * buf_ref[pl.ds(SUBLANES + base - s, t), :]
    return acc


def _step_start(bps, *carry_refs):
    @pl.when((pl.program_id(0) * BLOCKS_PER_STEP) % bps == 0)
    def _():
        for ref in carry_refs:
            ref[...] = jnp.zeros_like(ref)


def _conv_carry(buf_ref):
    buf_ref[0:SUBLANES, :] = buf_ref[T_STEP:T_STEP + SUBLANES, :]


def _hgrn_kernel(bps, xn_ref, w_ref, lb_ref, oml_ref, hnw_ref, dm_ref, ids_ref,
                 o_ref, st_ref, *scratch):
    _step_start(bps, st_ref)
    w = BRANCH_WIDTH
    staged = []
    for blk in range(BLOCKS_PER_STEP):
        rows = slice(blk * T_BLK, (blk + 1) * T_BLK)
        ql_ref, kl_ref, rhs_ref, v_ref = scratch[4 * blk:4 * blk + 4]
        proj = _dot(xn_ref[rows, :], w_ref[...])
        fp = proj[:, w:2 * w]
        u = jnp.exp(-jnp.abs(fp))
        r = 1.0 / (1.0 + u)
        pos = fp >= 0.0
        sig = jnp.where(pos, r, u * r)
        key = oml_ref[...] * jnp.where(pos, u * r, r)
        logf = jnp.log(jnp.maximum(lb_ref[...] + oml_ref[...] * sig, F32_FLOOR))
        qf = _silu(proj[:, 0:w])
        v_ref[...] = proj[:, 2 * w:3 * w].astype(BF16)
        staged.append((qf, key, logf, _silu(proj[:, 3 * w:4 * w])))
    chains = []
    for blk in range(BLOCKS_PER_STEP):
        rows = slice(blk * T_BLK, (blk + 1) * T_BLK)
        ql_ref, kl_ref, rhs_ref, v_ref = scratch[4 * blk:4 * blk + 4]
        qf, key, logf, gs = staged[blk]

        def emit(h, o, rows=rows, gs=gs):
            cols = slice(h * HEAD_V, (h + 1) * HEAD_V)
            o_ref[rows, cols] = (_rmsnorm(o, hnw_ref[...]) * gs[:, cols]).astype(o_ref.dtype)

        chains.append(_gla_chunk(qf, key, logf, v_ref, st_ref, ql_ref, kl_ref, rhs_ref, dm_ref, ids_ref, 1, emit))
    _interleave(*chains)


def _gla_kernel(bps, xn_ref, w_ref, gb_ref, gnw_ref, dm_ref, ids_ref, o_ref, st_ref, *scratch):
    _step_start(bps, st_ref)
    kd, w = GLA_KDIM, BRANCH_WIDTH
    chains = []
    for blk in range(BLOCKS_PER_STEP):
        rows = slice(blk * T_BLK, (blk + 1) * T_BLK)
        ql_ref, kl_ref, rhs_ref, v_ref = scratch[4 * blk:4 * blk + 4]
        proj = _dot(xn_ref[rows, :], w_ref[:, 0:GLA_PROJ])
        logf = _log_sigmoid(proj[:, 0:kd] + gb_ref[...]) * (1.0 / GLA_GATE_NORMALIZER)
        proj = proj[:, kd:]
        q = proj[:, 0:kd] * (GLA_HEAD_K ** -0.5)
        k = proj[:, kd:2 * kd]
        v_ref[...] = proj[:, 2 * kd:2 * kd + w].astype(BF16)
        gs = _silu(proj[:, 2 * kd + w:2 * kd + 2 * w])

        def emit(h, o, rows=rows, gs=gs):
            cols = slice(h * HEAD_V, (h + 1) * HEAD_V)
            o_ref[rows, cols] = (_rmsnorm(o, gnw_ref[...]) * gs[:, cols]).astype(o_ref.dtype)

        chains.append(_gla_chunk(q, k, logf, v_ref, st_ref, ql_ref, kl_ref, rhs_ref, dm_ref, ids_ref, 2, emit))
    _interleave(*chains)


def _ssd_kernel(bps, xn_ref, w_ref, cw_ref, cb_ref, dtb_ref, aneg_ref, dsk_ref, snw_ref, tril_ref,
                exp_ref, o_ref, buf_ref, st_ref, *y_refs):
    _step_start(bps, st_ref, buf_ref)
    for blk in range(BLOCKS_PER_STEP):
        _ssd_block(blk, xn_ref, w_ref, cw_ref, cb_ref, dtb_ref, aneg_ref, dsk_ref, snw_ref, tril_ref,
                   exp_ref, o_ref, buf_ref, st_ref, y_refs[blk])
    _conv_carry(buf_ref)


def _ssd_block(blk, xn_ref, w_ref, cw_ref, cb_ref, dtb_ref, aneg_ref, dsk_ref, snw_ref, tril_ref,
               exp_ref, o_ref, buf_ref, st_ref, y_ref):
    t = T_BLK
    rows = slice(blk * t, (blk + 1) * t)
    w = BRANCH_WIDTH
    gs = SSD_GROUPS * SSD_STATE
    proj = _dot(xn_ref[rows, :], w_ref[:, 0:SSD_PROJ])
    z = proj[:, 0:w]
    xbc = _silu(_causal_conv(buf_ref, proj[:, w:w + SSD_CONV_DIM], cw_ref, cb_ref, blk * t))
    xs = xbc[:, 0:w]
    bm = xbc[:, w:w + gs].astype(BF16)
    cm = xbc[:, w + gs:w + 2 * gs]
    dt = _softplus(proj[:, w + SSD_CONV_DIM:] + dtb_ref[...])
    acs = _sel_dot(tril_ref[...], dt * aneg_ref[...])
    dt_e = _dot_sel(dt, exp_ref[...])
    acs_e = _dot_sel(acs, exp_ref[...])
    alast_e = acs_e[t - 1:t, :]
    xdt = xs * dt_e
    xdt_bf = xdt.astype(BF16)

    row = lax.broadcasted_iota(jnp.int32, (t, t), 0)
    col = lax.broadcasted_iota(jnp.int32, (t, t), 1)
    causal = col <= row
    lane = lax.broadcasted_iota(jnp.int32, (1, LANES), 1)
    acs_t = acs.T
    gmats = []
    for g in range(SSD_GROUPS):
        cg = jnp.where((lane // SSD_STATE) == g, cm, 0.0).astype(BF16)
        gmats.append(_dot_nt(cg, bm))
    hpg = SSD_HEADS // SSD_GROUPS
    for pair in range(SSD_HEADS // 2):
        tl = slice(pair * LANES, (pair + 1) * LANES)
        ys = []
        for sub in range(2):
            h = 2 * pair + sub
            seg = jnp.minimum(acs[:, h:h + 1] - acs_t[h:h + 1, :], 0.0)
            m = jnp.where(causal, gmats[h // hpg] * jnp.exp(seg), 0.0).astype(BF16)
            ys.append(_dot(m, xdt_bf[:, tl]))
        y_diag = jnp.where(lane < SSD_HEADDIM, ys[0], ys[1])
        y_off = _dot(cm.astype(BF16), st_ref[:, tl].astype(BF16)) * jnp.exp(acs_e[:, tl])
        y = y_diag + y_off + xs[:, tl] * dsk_ref[:, tl]
        y_ref[:, tl] = y * _silu(z[:, tl])
    xdec = (xdt * jnp.exp(alast_e - acs_e)).astype(BF16)
    upd = _dot_tn(bm, xdec)
    srow = lax.broadcasted_iota(jnp.int32, (gs, w), 0) // SSD_STATE
    scol = lax.broadcasted_iota(jnp.int32, (gs, w), 1) // (hpg * SSD_HEADDIM)
    st_ref[...] = st_ref[...] * jnp.exp(alast_e) + jnp.where(srow == scol, upd, 0.0)
    gw = w // SSD_GROUPS
    for g in range(SSD_GROUPS):
        yg = y_ref[:, g * gw:(g + 1) * gw]
        o_ref[rows, g * gw:(g + 1) * gw] = (
            yg * lax.rsqrt(jnp.mean(yg * yg, axis=-1, keepdims=True) + NORM_EPS) * snw_ref[:, g * gw:(g + 1) * gw]
        ).astype(o_ref.dtype)


def _sqrt_one_minus_exp2x(x):
    th = jnp.tanh(x)
    y = (-2.0 * th) / (1.0 - th)
    return y * lax.rsqrt(jnp.maximum(y, 1e-30))


def _doubling_scan(a, b, pos, n, axis):
    off = 1
    while off < n:
        keep = pos >= off
        a_sh = jnp.where(keep, pltpu.roll(a, off, axis), 1.0)
        b_sh = jnp.where(keep, pltpu.roll(b, off, axis), 0.0)
        b = a * b_sh + b
        a = a * a_sh
        off *= 2
    return a, b


def _scan_rows(a, b, hc_ref, ab_ref, cg_ref):
    t, n = a.shape
    g = t // SUBLANES
    nt = n // LANES
    a8, b8 = _doubling_scan(a.reshape(g, SUBLANES, n), b.reshape(g, SUBLANES, n),
                            lax.broadcasted_iota(jnp.int32, (1, SUBLANES, 1), 1), SUBLANES, 1)
    a8 = a8.reshape(t, n)
    b8 = b8.reshape(t, n)
    for i in range(nt):
        ab_ref[i] = a8[:, i * LANES:(i + 1) * LANES]
        ab_ref[nt + i] = b8[:, i * LANES:(i + 1) * LANES]
    last = pl.ds(SUBLANES - 1, g, stride=SUBLANES)
    gidx = lax.broadcasted_iota(jnp.int32, (g, 1), 0)
    ag = jnp.concatenate([ab_ref[i, last, :] for i in range(nt)], axis=1)
    bg = jnp.concatenate([ab_ref[nt + i, last, :] for i in range(nt)], axis=1)
    ag, bg = _doubling_scan(ag, bg, gidx, g, 0)
    state = bg + ag * hc_ref[...]
    cg_ref[...] = jnp.where(gidx >= 1, pltpu.roll(state, 1, 0), hc_ref[...])
    hc_ref[...] = state[g - 1:g, :]
    out = []
    for i in range(nt):
        cg = cg_ref[:, i * LANES:(i + 1) * LANES].reshape(g, 1, LANES)
        hi = ab_ref[nt + i].reshape(g, SUBLANES, LANES) + ab_ref[i].reshape(g, SUBLANES, LANES) * cg
        out.append(hi.reshape(t, LANES))
    return jnp.concatenate(out, axis=1)


def _lru_kernel(bps, xn_ref, w_ref, cw_ref, cb_ref, wg_ref, bg_ref, lam_ref, o_ref, buf_ref, hc_ref,
                *scratch):
    _step_start(bps, hc_ref, buf_ref)
    for blk in range(BLOCKS_PER_STEP):
        _lru_block(blk, xn_ref, w_ref, cw_ref, cb_ref, wg_ref, bg_ref, lam_ref, o_ref, buf_ref, hc_ref,
                   *scratch[2 * blk:2 * blk + 2])
    _conv_carry(buf_ref)


def _lru_block(blk, xn_ref, w_ref, cw_ref, cb_ref, wg_ref, bg_ref, lam_ref, o_ref, buf_ref, hc_ref,
               ab_ref, cg_ref):
    t = T_BLK
    rows = slice(blk * t, (blk + 1) * t)
    w = BRANCH_WIDTH
    proj = _dot(xn_ref[rows, :], w_ref[...])
    u = _causal_conv(buf_ref, proj[:, 0:w], cw_ref, cb_ref, blk * t)
    rg = _dot(u.astype(BF16), wg_ref[...]) + bg_ref[...]
    r = _sigmoid(rg[:, 0:w])
    ig = _sigmoid(rg[:, w:2 * w])
    log_a = (-LRU_C) * r * _softplus(-lam_ref[...])
    a = jnp.exp(log_a)
    b = _sqrt_one_minus_exp2x(log_a) * (ig * u)
    hseq = _scan_rows(a, b, hc_ref, ab_ref, cg_ref)
    gate = proj[:, w:2 * w]
    gelu = 0.5 * gate * (1.0 + jnp.tanh(math.sqrt(2.0 / math.pi) * (gate + 0.044715 * gate * gate * gate)))
    o_ref[rows, :] = (hseq * gelu).astype(o_ref.dtype)


def _merge_kernel(h_ref, xn_ref, ya_ref, yb_ref, yc_ref, yd_ref, wmg_ref, wbr_ref, wout_ref, o_ref):
    xn = xn_ref[...]
    merged = None
    for n, y_ref in enumerate((ya_ref, yb_ref, yc_ref, yd_ref)):
        gate = _sigmoid(_dot(xn, wmg_ref[:, n * D_MODEL:(n + 1) * D_MODEL]))
        term = gate * _dot(y_ref[...], wbr_ref[n])
        merged = term if merged is None else merged + term
    o_ref[...] = h_ref[...] + _dot(merged.astype(BF16), wout_ref[...])


def _ffn_kernel(final, h_ref, nw_ref, win_ref, wout_ref, nnw_ref, o_ref, *xn_out):
    hres = h_ref[...]
    xn = _rmsnorm(hres, nw_ref[...]).astype(BF16)
    gu = _dot(xn, win_ref[...])
    act = (_silu(gu[:, 0:D_FF]) * gu[:, D_FF:2 * D_FF]).astype(BF16)
    out = hres + _dot(act, wout_ref[...])
    if final:
        o_ref[...] = _rmsnorm(out, nnw_ref[...])
    else:
        o_ref[...] = out
        xn_out[0][...] = _rmsnorm(out, nnw_ref[...]).astype(BF16)


def _norm_kernel(x_ref, nw_ref, o_ref):
    o_ref[...] = _rmsnorm(x_ref[...], nw_ref[...]).astype(o_ref.dtype)


def _const_spec(shape):
    nd = len(shape)
    return pl.BlockSpec(shape, lambda i, _nd=nd: (0,) * _nd, pipeline_mode=pl.Buffered(1))


def _row_spec(rows, width):
    return pl.BlockSpec((rows, width), lambda i: (i, 0))


class _LayerWindow:
    def __init__(self, array, layer, width=None, col=0):
        self.array, self.layer, self.width, self.col = array, layer, width or array.shape[-1], col

    def spec(self):
        shape = self.array.shape
        mid = (0,) * (len(shape) - 2)
        layer, col = self.layer, self.col
        return pl.BlockSpec((None,) + shape[1:-1] + (self.width,), lambda i: (layer,) + mid + (col,),
                            pipeline_mode=pl.Buffered(1))


def _pcall(kernel, name, n_tokens, rows, operands, row_flags, outs, scratch):
    in_specs = [a.spec() if isinstance(a, _LayerWindow) else
                (_row_spec(rows, a.shape[1]) if is_row else _const_spec(a.shape))
                for a, is_row in zip(operands, row_flags)]
    operands = [a.array if isinstance(a, _LayerWindow) else a for a in operands]
    many = isinstance(outs, list)
    out_specs = [_row_spec(rows, wd) for wd, _ in (outs if many else [outs])]
    out_shape = [jax.ShapeDtypeStruct((n_tokens, wd), dt) for wd, dt in (outs if many else [outs])]
    return pl.pallas_call(
        kernel,
        name=name,
        grid=(n_tokens // rows,),
        in_specs=in_specs,
        out_specs=out_specs if many else out_specs[0],
        out_shape=out_shape if many else out_shape[0],
        scratch_shapes=scratch,
        compiler_params=pltpu.CompilerParams(dimension_semantics=("arbitrary",), vmem_limit_bytes=VMEM_LIMIT),
    )(*operands)


def _tril_const(t):
    return jnp.asarray(np.tril(np.ones((t, t), np.float32)), BF16)


def _gla_scratch(nk):
    ntile = nk // LANES
    per_block = [
        pltpu.VMEM((GLA_PARTS, T_BLK, nk), BF16),
        pltpu.VMEM((GLA_PARTS, T_BLK, nk), BF16),
        pltpu.VMEM((2 * T_BLK, nk), BF16),
        pltpu.VMEM((T_BLK, BRANCH_WIDTH), BF16),
    ]
    return [pltpu.VMEM((ntile, HEAD_V, LANES), F32)] + per_block * BLOCKS_PER_STEP


def _row(x):
    return x.reshape(1, -1).astype(F32)


PACK_MERGE = (4096, 0)
PACK_HGRN = (2048, 2)
PACK_SSD = (2048, 3)
PACK_GLA = (2048, 4)
PACK_LRU = (1024, 10)


def _pack_in_proj(w_in, gla_gate_w):
    c = [0] + np.cumsum([512, 512, 512, 512, 512, 768, 8, 256, 256, 512, 512, 16, 512, 512, 4096]).tolist()
    depth = w_in.shape[0]
    zeros = lambda n: jnp.zeros((depth, D_MODEL, n), F32)
    w_gate = jnp.einsum("lkr,lrn->lkn", w_in[:, :, c[11]:c[12]].astype(F32), gla_gate_w.astype(F32),
                        precision=lax.Precision.HIGHEST)
    ssd_cols = c[7] - c[4]
    return jnp.concatenate([
        w_in[:, :, c[14]:c[15]],
        w_in[:, :, c[0]:c[4]],
        w_in[:, :, c[4]:c[7]], zeros(PACK_SSD[0] - ssd_cols),
        w_gate, w_in[:, :, c[7]:c[11]], zeros(PACK_GLA[0] - GLA_PROJ),
        w_in[:, :, c[12]:c[14]],
    ], axis=2).astype(BF16)


def _layer(h, xn, seq, l, wts, p):
    n_tokens = h.shape[0]
    bps = seq // T_BLK
    assert seq % T_STEP == 0 and n_tokens % seq == 0
    tril = _tril_const(T_BLK)
    dmat = _gla_decay_matrix(T_BLK)
    part_ids = _gla_part_ids()
    w_all, w_br, w_o, w_fi, w_fo = wts
    win = lambda pack: _LayerWindow(w_all, l, *pack)

    lb = p["lb"].astype(F32)
    ya = _pcall(
        functools.partial(_hgrn_kernel, bps), "hgrn2", n_tokens, T_STEP,
        [xn, win(PACK_HGRN), _row(lb), _row(1.0 - lb), _row(p["hgrn_norm_w"]), dmat, part_ids],
        [True] + [False] * 6, (BRANCH_WIDTH, BF16), _gla_scratch(BRANCH_WIDTH))

    pad8 = lambda v: jnp.concatenate([v.astype(F32), jnp.zeros((LANES - SSD_HEADS,), F32)]).reshape(1, LANES)
    expand = jnp.asarray(np.arange(LANES)[:, None] == (np.arange(BRANCH_WIDTH)[None, :] // SSD_HEADDIM), BF16)
    yb = _pcall(
        functools.partial(_ssd_kernel, bps), "ssd", n_tokens, T_STEP,
        [xn, win(PACK_SSD), p["ssd_conv_w"].astype(F32), _row(p["ssd_conv_b"]), pad8(p["ssd_dt_bias"]),
         pad8(-jnp.exp(p["ssd_a_log"].astype(F32))), _row(jnp.repeat(p["ssd_d"], SSD_HEADDIM)), _row(p["ssd_norm_w"]),
         tril, expand],
        [True] + [False] * 9, (BRANCH_WIDTH, BF16),
        [pltpu.VMEM((T_STEP + SUBLANES, SSD_CONV_DIM), F32), pltpu.VMEM((SSD_GROUPS * SSD_STATE, BRANCH_WIDTH), F32)]
        + [pltpu.VMEM((T_BLK, BRANCH_WIDTH), F32)] * BLOCKS_PER_STEP)

    yc = _pcall(
        functools.partial(_gla_kernel, bps), "gla", n_tokens, T_STEP,
        [xn, win(PACK_GLA), _row(p["gla_gate_b"]), _row(p["gla_norm_w"]), dmat, part_ids],
        [True] + [False] * 5, (BRANCH_WIDTH, BF16), _gla_scratch(GLA_KDIM))

    eye = jnp.eye(LRU_BLOCKS, dtype=F32)
    bdiag = lambda wt: jnp.einsum("kde,kl->kdle", wt.astype(F32), eye).reshape(BRANCH_WIDTH, BRANCH_WIDTH)
    w_gates = jnp.concatenate([bdiag(p["lru_wa"]), bdiag(p["lru_wx"])], axis=1).astype(BF16)
    b_gates = _row(jnp.concatenate([p["lru_ba"], p["lru_bx"]]))
    yd = _pcall(
        functools.partial(_lru_kernel, bps), "rglru", n_tokens, T_STEP,
        [xn, win(PACK_LRU), p["lru_conv_w"].astype(F32), _row(p["lru_conv_b"]), w_gates,
         b_gates, _row(p["lru_lambda"])],
        [True] + [False] * 6, (BRANCH_WIDTH, BF16),
        [pltpu.VMEM((T_STEP + SUBLANES, BRANCH_WIDTH), F32), pltpu.VMEM((1, BRANCH_WIDTH), F32)]
        + [pltpu.VMEM((2 * BRANCH_WIDTH // LANES, T_BLK, LANES), F32),
           pltpu.VMEM((T_BLK // SUBLANES, BRANCH_WIDTH), F32)] * BLOCKS_PER_STEP)

    h = _pcall(
        _merge_kernel, "merge", n_tokens, T_STEP,
        [h, xn, ya, yb, yc, yd, win(PACK_MERGE), _LayerWindow(w_br, l), _LayerWindow(w_o, l)],
        [True, True, True, True, True, True, False, False, False], (D_MODEL, F32), [])

    outs = (D_MODEL, F32) if p["final"] else [(D_MODEL, F32), (D_MODEL, BF16)]
    res = _pcall(
        functools.partial(_ffn_kernel, p["final"]), "swiglu", n_tokens, T_STEP,
        [h, _row(p["norm_ffn_w"]), _LayerWindow(w_fi, l), _LayerWindow(w_fo, l), _row(p["next_norm_w"])],
        [True, False, False, False, False], outs, [])
    return (res, None) if p["final"] else tuple(res)


def kernel(x, norm_mix_w, w_in, hgrn_lower_bounds, hgrn_norm_w, ssd_conv_w, ssd_conv_b, ssd_dt_bias, ssd_a_log, ssd_d, ssd_norm_w, gla_gate_w, gla_gate_b, gla_norm_w, lru_conv_w, lru_conv_b, lru_wa, lru_ba, lru_wx, lru_bx, lru_lambda, w_branch, w_out, norm_ffn_w, w_ffn_in, w_ffn_out, norm_f_w):
    bsz, seq, _ = x.shape
    depth = w_in.shape[0]
    lb_all = jnp.cumsum(jax.nn.softmax(hgrn_lower_bounds.astype(F32), axis=0), axis=0)
    lb_all = lb_all - lb_all[0:1]
    wts = (_pack_in_proj(w_in, gla_gate_w), w_branch.astype(BF16), w_out.astype(BF16), w_ffn_in.astype(BF16),
           w_ffn_out.astype(BF16))
    h = x.reshape(bsz * seq, D_MODEL)
    xn = _pcall(_norm_kernel, "norm", bsz * seq, T_STEP, [h, _row(norm_mix_w[0])], [True, False], (D_MODEL, BF16), [])
    for l in range(depth):
        p = dict(
            lb=lb_all[l], hgrn_norm_w=hgrn_norm_w[l],
            ssd_conv_w=ssd_conv_w[l], ssd_conv_b=ssd_conv_b[l], ssd_dt_bias=ssd_dt_bias[l], ssd_a_log=ssd_a_log[l],
            ssd_d=ssd_d[l], ssd_norm_w=ssd_norm_w[l], gla_gate_b=gla_gate_b[l],
            gla_norm_w=gla_norm_w[l], lru_conv_w=lru_conv_w[l], lru_conv_b=lru_conv_b[l], lru_wa=lru_wa[l],
            lru_ba=lru_ba[l], lru_wx=lru_wx[l], lru_bx=lru_bx[l], lru_lambda=lru_lambda[l],
            norm_ffn_w=norm_ffn_w[l],
            next_norm_w=(norm_f_w if l == depth - 1 else norm_mix_w[l + 1]), final=(l == depth - 1))
        h, xn = _layer(h, xn, seq, l, wts, p)
    return h.reshape(bsz, seq, D_MODEL)
```

```python
import functools
import math

import jax
import jax.numpy as jnp
import numpy as np
from jax import lax
from jax.experimental import pallas as pl
from jax.experimental.pallas import tpu as pltpu

F32 = jnp.float32
BF16 = jnp.bfloat16

D_MODEL = 1024
BRANCH_WIDTH = 512
N_BRANCHES = 4
CONV_WIDTH = 4
NORM_EPS = 1e-6
HGRN_HEADS = 4
SSD_HEADS = 8
SSD_HEADDIM = 64
SSD_GROUPS = 2
SSD_STATE = 64
SSD_CONV_DIM = 768
GLA_HEADS = 4
GLA_KDIM = 256
GLA_HEAD_K = 64
GLA_GATE_RANK = 16
GLA_GATE_NORMALIZER = 16.0
LRU_BLOCKS = 8
LRU_BLOCK = 64
LRU_C = 8.0
D_FF = 2816

LANES = 128
SUBLANES = 8
T_BLK = 256
BLOCKS_PER_STEP = 2
T_STEP = BLOCKS_PER_STEP * T_BLK
NORM_ROWS = 2048
HEAD_V = 128
VMEM_LIMIT = 48 * 1024 * 1024
F32_FLOOR = 1e-37
SSD_PROJ = BRANCH_WIDTH + SSD_CONV_DIM + LANES
GLA_PROJ = 3 * GLA_KDIM + 2 * BRANCH_WIDTH


def _dot(a, b):
    return jnp.dot(a, b, preferred_element_type=F32)


def _dot_nt(a, b):
    return lax.dot_general(a, b, (((1,), (1,)), ((), ())), preferred_element_type=F32)


def _dot_tn(a, b):
    return lax.dot_general(a, b, (((0,), (0,)), ((), ())), preferred_element_type=F32)


def _split3(x):
    hi = x.astype(BF16)
    r1 = x - hi.astype(F32)
    mid = r1.astype(BF16)
    lo = (r1 - mid.astype(F32)).astype(BF16)
    return hi, mid, lo


def _sel_dot(mat, x):
    hi, mid, lo = _split3(x)
    return (_dot(mat, lo) + _dot(mat, mid)) + _dot(mat, hi)


def _dot_sel(x, mat):
    hi, mid, lo = _split3(x)
    return (_dot(lo, mat) + _dot(mid, mat)) + _dot(hi, mat)


def _sigmoid(x):
    return jax.nn.sigmoid(x)


def _silu(x):
    return x * _sigmoid(x)


def _softplus(x):
    return jnp.maximum(x, 0.0) + jnp.log1p(jnp.exp(-jnp.abs(x)))


def _log_sigmoid(x):
    return jnp.minimum(x, 0.0) - jnp.log1p(jnp.exp(-jnp.abs(x)))


def _rmsnorm(x, w):
    return x * lax.rsqrt(jnp.mean(x * x, axis=-1, keepdims=True) + NORM_EPS) * w


GLA_SMALL = (1, 2, 4)
GLA_MID = (8, 16, 32, 64)
GLA_PARTS = 2 + len(GLA_SMALL) + len(GLA_MID) + 1
LOG2E = 1.4426950408889634
GLA_SPAN_LIMIT = 48.0
HGRN_BASE = 32
GLA_BASE = 128


def _gla_part_ids():
    row = np.arange(LANES)[:, None]
    col = np.arange(LANES)[None, :]
    x = row ^ col
    ids = np.where(x == 0, 0, np.floor(np.log2(np.maximum(x, 1))).astype(np.int64) + 1)
    return jnp.asarray(np.where(col <= row, ids, -1), jnp.int32)


def _gla_decay_matrix(t):
    u = np.arange(t)[None, :]
    r = np.arange(t)[:, None]
    blocks = [u <= r]
    for s in GLA_SMALL:
        mid = (r // (2 * s)) * (2 * s) + s - 1
        blocks.append(((u > mid) & (u <= r)) | ((u > r) & (u <= mid)))
    m = np.concatenate(blocks, axis=0)
    return jnp.asarray(np.concatenate([m, m], axis=1), BF16)


def _store_decay_pieces(rhs_ref, logf):
    t = logf.shape[0]
    lf2 = logf * LOG2E
    hi = lf2.astype(BF16)
    rhs_ref[0:t, :] = hi
    rhs_ref[t:2 * t, :] = (lf2 - hi.astype(F32)).astype(BF16)


def _gla_chunk(q, k, logf, v_ref, st_ref, ql_ref, kl_ref, rhs_ref, dm_ref, ids_ref, hpt, base, emit):
    t, nk = q.shape
    assert t == 2 * LANES
    ntile = nk // LANES
    dk = LANES // hpt
    levels = GLA_SMALL + GLA_MID

    _store_decay_pieces(rhs_ref, logf)
    bcum = _dot(dm_ref[0:t, :], rhs_ref[...])
    blast = bcum[t - 1:t, :]
    qb = q.astype(BF16)
    kb = k.astype(BF16)
    ql_ref[0] = qb * jnp.exp2(bcum).astype(BF16)
    kl_ref[0] = kb * jnp.exp2(blast - bcum).astype(BF16)

    def level_part(part, s):
        bc3 = bcum.reshape(t // (2 * s), 2 * s, nk)
        ref = bc3[:, s - 1:s, :]
        e = jnp.exp2(jnp.concatenate([ref - bc3[:, 0:s, :], bc3[:, s:2 * s, :] - ref], axis=1))
        e = e.reshape(t, nk).astype(BF16)
        ql_ref[part] = qb * e
        kl_ref[part] = kb * e

    part_of = {}
    part = 1
    for s in levels + (LANES,):
        if s >= base:
            level_part(part, s)
            part_of[s] = part
            part += 1
    free = part
    p_top = part_of[LANES]

    bcb = bcum.reshape(t // base, base, nk)
    span = bcb - bcb[:, base // 2 - 1:base // 2, :]
    fits = jnp.max(jnp.abs(span)) <= GLA_SPAN_LIMIT

    ids = ids_ref[...]
    lane = lax.broadcasted_iota(jnp.int32, (1, LANES), 1)

    def own(x, sub):
        return x if hpt == 1 else jnp.where((lane // dk) == sub, x, jnp.zeros_like(x))

    def stacked(x):
        return x if hpt == 1 else jnp.concatenate([own(x, sub) for sub in range(hpt)], axis=0)

    ids_st = ids if hpt == 1 else jnp.concatenate([ids] * hpt, axis=0)

    def finish(first, overrides):
        intra = []
        for tau in range(ntile):
            tl = slice(tau * LANES, (tau + 1) * LANES)
            diag = []
            for c in range(2):
                rows = slice(c * LANES, (c + 1) * LANES)
                sc = jnp.where(ids_st >= 0, _dot_nt(stacked(ql_ref[first, rows, tl]), kl_ref[first, rows, tl]), 0.0)
                for pid, p in overrides:
                    sc = jnp.where(ids_st == pid, _dot_nt(stacked(ql_ref[p, rows, tl]), kl_ref[p, rows, tl]), sc)
                diag.append(sc.astype(BF16))
            low = _dot_nt(stacked(ql_ref[p_top, LANES:t, tl]), kl_ref[p_top, 0:LANES, tl]).astype(BF16)
            for sub in range(hpt):
                hv = slice((tau * hpt + sub) * HEAD_V, (tau * hpt + sub + 1) * HEAD_V)
                hr = slice(sub * LANES, (sub + 1) * LANES)
                o_top = _dot(diag[0][hr], v_ref[0:LANES, hv])
                o_bot = _dot(jnp.concatenate([low[hr], diag[1][hr]], axis=1), v_ref[:, hv])
                intra.append(jnp.concatenate([o_top, o_bot], axis=0))
        for tau in range(ntile):
            tl = slice(tau * LANES, (tau + 1) * LANES)
            inter = _dot_nt(stacked(ql_ref[0, :, tl]), st_ref[tau].astype(BF16))
            for sub in range(hpt):
                h = tau * hpt + sub
                emit(h, intra[h] + inter[sub * t:(sub + 1) * t])
        for tau in range(ntile):
            tl = slice(tau * LANES, (tau + 1) * LANES)
            kd = kl_ref[0, :, tl]
            upd = _dot_tn(v_ref[:, tau * hpt * HEAD_V:(tau * hpt + 1) * HEAD_V], kd)
            for sub in range(1, hpt):
                h = tau * hpt + sub
                upd = jnp.where((lane // dk) == sub, _dot_tn(v_ref[:, h * HEAD_V:(h + 1) * HEAD_V], kd), upd)
            st_ref[tau] = st_ref[tau] * jnp.exp2(blast[:, tl]) + upd

    def part_id(s):
        return 1 + levels.index(s)

    upper = [(part_id(s), part_of[s]) for s in levels if s >= base]

    @pl.when(fits)
    def _():
        ql_ref[free] = qb * jnp.exp2(span).reshape(t, nk).astype(BF16)
        kl_ref[free] = kb * jnp.exp2(-span).reshape(t, nk).astype(BF16)
        finish(free, upper)

    @pl.when(jnp.logical_not(fits))
    def _():
        d3 = _dot(dm_ref[t:(1 + len(GLA_SMALL)) * t, :], rhs_ref[...])
        lower = []
        part = free
        for i, s in enumerate(GLA_SMALL):
            if s < base:
                e = jnp.exp2(d3[i * t:(i + 1) * t]).astype(BF16)
                ql_ref[part] = qb * e
                kl_ref[part] = kb * e
                lower.append((part_id(s), part))
                part += 1
        for s in GLA_MID:
            if s < base:
                level_part(part, s)
                lower.append((part_id(s), part))
                part += 1
        ql_ref[part] = qb
        kl_ref[part] = kb
        finish(part, lower + upper)


def _causal_conv(buf_ref, x, w_ref, b_ref, base):
    t = x.shape[0]
    buf_ref[SUBLANES + base:SUBLANES + base + t, :] = x
    acc = b_ref[...] + w_ref[CONV_WIDTH - 1:CONV_WIDTH, :] * x
    for s in range(1, CONV_WIDTH):
        acc = acc + w_ref[CONV_WIDTH - 1 - s:CONV_WIDTH - s, :] * buf_ref[pl.ds(SUBLANES + base - s, t), :]
    return acc


def _step_start(bps, *carry_refs):
    @pl.when((pl.program_id(0) * BLOCKS_PER_STEP) % bps == 0)
    def _():
        for ref in carry_refs:
            ref[...] = jnp.zeros_like(ref)


def _conv_carry(buf_ref):
    buf_ref[0:SUBLANES, :] = buf_ref[T_STEP:T_STEP + SUBLANES, :]


def _hgrn_kernel(bps, xn_ref, w_ref, lb_ref, oml_ref, hnw_ref, dm_ref, ids_ref,
                 o_ref, st_ref, *scratch):
    _step_start(bps, st_ref)
    w = BRANCH_WIDTH
    staged = []
    for blk in range(BLOCKS_PER_STEP):
        rows = slice(blk * T_BLK, (blk + 1) * T_BLK)
        ql_ref, kl_ref, rhs_ref, v_ref = scratch[4 * blk:4 * blk + 4]
        proj = _dot(xn_ref[rows, :], w_ref[...])
        fp = proj[:, w:2 * w]
        u = jnp.exp(-jnp.abs(fp))
        r = 1.0 / (1.0 + u)
        pos = fp >= 0.0
        sig = jnp.where(pos, r, u * r)
        key = oml_ref[...] * jnp.where(pos, u * r, r)
        logf = jnp.log(jnp.maximum(lb_ref[...] + oml_ref[...] * sig, F32_FLOOR))
        qf = _silu(proj[:, 0:w])
        v_ref[...] = proj[:, 2 * w:3 * w].astype(BF16)
        staged.append((qf, key, logf, _silu(proj[:, 3 * w:4 * w])))
    for blk in range(BLOCKS_PER_STEP):
        rows = slice(blk * T_BLK, (blk + 1) * T_BLK)
        ql_ref, kl_ref, rhs_ref, v_ref = scratch[4 * blk:4 * blk + 4]
        qf, key, logf, gs = staged[blk]

        def emit(h, o, rows=rows, gs=gs):
            cols = slice(h * HEAD_V, (h + 1) * HEAD_V)
            o_ref[rows, cols] = (_rmsnorm(o, hnw_ref[...]) * gs[:, cols]).astype(o_ref.dtype)

        _gla_chunk(qf, key, logf, v_ref, st_ref, ql_ref, kl_ref, rhs_ref, dm_ref, ids_ref, 1, HGRN_BASE, emit)


def _gla_kernel(bps, xn_ref, w_ref, gb_ref, gnw_ref, dm_ref, ids_ref, o_ref, st_ref, *scratch):
    _step_start(bps, st_ref)
    kd, w = GLA_KDIM, BRANCH_WIDTH
    for blk in range(BLOCKS_PER_STEP):
        rows = slice(blk * T_BLK, (blk + 1) * T_BLK)
        ql_ref, kl_ref, rhs_ref, v_ref = scratch[4 * blk:4 * blk + 4]
        proj = _dot(xn_ref[rows, :], w_ref[:, 0:GLA_PROJ])
        logf = _log_sigmoid(proj[:, 0:kd] + gb_ref[...]) * (1.0 / GLA_GATE_NORMALIZER)
        proj = proj[:, kd:]
        q = proj[:, 0:kd] * (GLA_HEAD_K ** -0.5)
        k = proj[:, kd:2 * kd]
        v_ref[...] = proj[:, 2 * kd:2 * kd + w].astype(BF16)
        gs = _silu(proj[:, 2 * kd + w:2 * kd + 2 * w])

        def emit(h, o, rows=rows, gs=gs):
            cols = slice(h * HEAD_V, (h + 1) * HEAD_V)
            o_ref[rows, cols] = (_rmsnorm(o, gnw_ref[...]) * gs[:, cols]).astype(o_ref.dtype)

        _gla_chunk(q, k, logf, v_ref, st_ref, ql_ref, kl_ref, rhs_ref, dm_ref, ids_ref, 2, GLA_BASE, emit)


def _ssd_kernel(bps, xn_ref, w_ref, cw_ref, cb_ref, dtb_ref, aneg_ref, dsk_ref, snw_ref, tril_ref,
                exp_ref, o_ref, buf_ref, st_ref, *y_refs):
    _step_start(bps, st_ref, buf_ref)
    for blk in range(BLOCKS_PER_STEP):
        _ssd_block(blk, xn_ref, w_ref, cw_ref, cb_ref, dtb_ref, aneg_ref, dsk_ref, snw_ref, tril_ref,
                   exp_ref, o_ref, buf_ref, st_ref, y_refs[blk])
    _conv_carry(buf_ref)


def _ssd_block(blk, xn_ref, w_ref, cw_ref, cb_ref, dtb_ref, aneg_ref, dsk_ref, snw_ref, tril_ref,
               exp_ref, o_ref, buf_ref, st_ref, y_ref):
    t = T_BLK
    rows = slice(blk * t, (blk + 1) * t)
    w = BRANCH_WIDTH
    gs = SSD_GROUPS * SSD_STATE
    proj = _dot(xn_ref[rows, :], w_ref[:, 0:SSD_PROJ])
    z = proj[:, 0:w]
    xbc = _silu(_causal_conv(buf_ref, proj[:, w:w + SSD_CONV_DIM], cw_ref, cb_ref, blk * t))
    xs = xbc[:, 0:w]
    bm = xbc[:, w:w + gs].astype(BF16)
    cm = xbc[:, w + gs:w + 2 * gs]
    dt = _softplus(proj[:, w + SSD_CONV_DIM:] + dtb_ref[...])
    acs = _sel_dot(tril_ref[...], dt * aneg_ref[...])
    dt_e = _dot_sel(dt, exp_ref[...])
    acs_e = _dot_sel(acs, exp_ref[...])
    alast_e = acs_e[t - 1:t, :]
    xdt = xs * dt_e
    xdt_bf = xdt.astype(BF16)

    row = lax.broadcasted_iota(jnp.int32, (t, t), 0)
    col = lax.broadcasted_iota(jnp.int32, (t, t), 1)
    causal = col <= row
    lane = lax.broadcasted_iota(jnp.int32, (1, LANES), 1)
    acs_t = acs.T
    gmats = []
    for g in range(SSD_GROUPS):
        cg = jnp.where((lane // SSD_STATE) == g, cm, 0.0).astype(BF16)
        gmats.append(_dot_nt(cg, bm))
    hpg = SSD_HEADS // SSD_GROUPS
    for pair in range(SSD_HEADS // 2):
        tl = slice(pair * LANES, (pair + 1) * LANES)
        ys = []
        for sub in range(2):
            h = 2 * pair + sub
            seg = acs[:, h:h + 1] - acs_t[h:h + 1, :]
            m = jnp.where(causal, gmats[h // hpg] * jnp.exp(seg), 0.0).astype(BF16)
            ys.append(_dot(m, xdt_bf[:, tl]))
        y_diag = jnp.where(lane < SSD_HEADDIM, ys[0], ys[1])
        y_off = _dot(cm.astype(BF16), st_ref[:, tl].astype(BF16)) * jnp.exp(acs_e[:, tl])
        y = y_diag + y_off + xs[:, tl] * dsk_ref[:, tl]
        y_ref[:, tl] = y * _silu(z[:, tl])
    xdec = (xdt * jnp.exp(alast_e - acs_e)).astype(BF16)
    upd = _dot_tn(bm, xdec)
    srow = lax.broadcasted_iota(jnp.int32, (gs, w), 0) // SSD_STATE
    scol = lax.broadcasted_iota(jnp.int32, (gs, w), 1) // (hpg * SSD_HEADDIM)
    st_ref[...] = st_ref[...] * jnp.exp(alast_e) + jnp.where(srow == scol, upd, 0.0)
    gw = w // SSD_GROUPS
    for g in range(SSD_GROUPS):
        yg = y_ref[:, g * gw:(g + 1) * gw]
        o_ref[rows, g * gw:(g + 1) * gw] = (
            yg * lax.rsqrt(jnp.mean(yg * yg, axis=-1, keepdims=True) + NORM_EPS) * snw_ref[:, g * gw:(g + 1) * gw]
        ).astype(o_ref.dtype)


def _sqrt_one_minus_exp2x(x):
    th = jnp.tanh(x)
    y = (-2.0 * th) / (1.0 - th)
    return y * lax.rsqrt(jnp.maximum(y, 1e-30))


def _doubling_scan(a, b, pos, n, axis):
    off = 1
    while off < n:
        keep = pos >= off
        a_sh = jnp.where(keep, pltpu.roll(a, off, axis), 1.0)
        b_sh = jnp.where(keep, pltpu.roll(b, off, axis), 0.0)
        b = a * b_sh + b
        a = a * a_sh
        off *= 2
    return a, b


def _scan_rows(a, b, hc_ref, ab_ref, cg_ref):
    t, n = a.shape
    g = t // SUBLANES
    nt = n // LANES
    a8, b8 = _doubling_scan(a.reshape(g, SUBLANES, n), b.reshape(g, SUBLANES, n),
                            lax.broadcasted_iota(jnp.int32, (1, SUBLANES, 1), 1), SUBLANES, 1)
    a8 = a8.reshape(t, n)
    b8 = b8.reshape(t, n)
    for i in range(nt):
        ab_ref[i] = a8[:, i * LANES:(i + 1) * LANES]
        ab_ref[nt + i] = b8[:, i * LANES:(i + 1) * LANES]
    last = pl.ds(SUBLANES - 1, g, stride=SUBLANES)
    gidx = lax.broadcasted_iota(jnp.int32, (g, 1), 0)
    ag = jnp.concatenate([ab_ref[i, last, :] for i in range(nt)], axis=1)
    bg = jnp.concatenate([ab_ref[nt + i, last, :] for i in range(nt)], axis=1)
    ag, bg = _doubling_scan(ag, bg, gidx, g, 0)
    state = bg + ag * hc_ref[...]
    cg_ref[...] = jnp.where(gidx >= 1, pltpu.roll(state, 1, 0), hc_ref[...])
    hc_ref[...] = state[g - 1:g, :]
    out = []
    for i in range(nt):
        cg = cg_ref[:, i * LANES:(i + 1) * LANES].reshape(g, 1, LANES)
        hi = ab_ref[nt + i].reshape(g, SUBLANES, LANES) + ab_ref[i].reshape(g, SUBLANES, LANES) * cg
        out.append(hi.reshape(t, LANES))
    return jnp.concatenate(out, axis=1)


def _lru_kernel(bps, xn_ref, w_ref, cw_ref, cb_ref, wg_ref, bg_ref, lam_ref, o_ref, buf_ref, hc_ref,
                *scratch):
    _step_start(bps, hc_ref, buf_ref)
    for blk in range(BLOCKS_PER_STEP):
        _lru_block(blk, xn_ref, w_ref, cw_ref, cb_ref, wg_ref, bg_ref, lam_ref, o_ref, buf_ref, hc_ref,
                   *scratch[2 * blk:2 * blk + 2])
    _conv_carry(buf_ref)


def _lru_block(blk, xn_ref, w_ref, cw_ref, cb_ref, wg_ref, bg_ref, lam_ref, o_ref, buf_ref, hc_ref,
               ab_ref, cg_ref):
    t = T_BLK
    rows = slice(blk * t, (blk + 1) * t)
    w = BRANCH_WIDTH
    proj = _dot(xn_ref[rows, :], w_ref[...])
    u = _causal_conv(buf_ref, proj[:, 0:w], cw_ref, cb_ref, blk * t)
    rg = _dot(u.astype(BF16), wg_ref[...]) + bg_ref[...]
    r = _sigmoid(rg[:, 0:w])
    ig = _sigmoid(rg[:, w:2 * w])
    log_a = (-LRU_C) * r * _softplus(-lam_ref[...])
    a = jnp.exp(log_a)
    b = _sqrt_one_minus_exp2x(log_a) * (ig * u)
    hseq = _scan_rows(a, b, hc_ref, ab_ref, cg_ref)
    gate = proj[:, w:2 * w]
    gelu = 0.5 * gate * (1.0 + jnp.tanh(math.sqrt(2.0 / math.pi) * (gate + 0.044715 * gate * gate * gate)))
    o_ref[rows, :] = (hseq * gelu).astype(o_ref.dtype)


def _merge_kernel(h_ref, xn_ref, ya_ref, yb_ref, yc_ref, yd_ref, wmg_ref, wbr_ref, wout_ref, o_ref):
    xn = xn_ref[...]
    merged = None
    for n, y_ref in enumerate((ya_ref, yb_ref, yc_ref, yd_ref)):
        gate = _sigmoid(_dot(xn, wmg_ref[:, n * D_MODEL:(n + 1) * D_MODEL]))
        term = gate * _dot(y_ref[...], wbr_ref[n])
        merged = term if merged is None else merged + term
    o_ref[...] = h_ref[...] + _dot(merged.astype(BF16), wout_ref[...])


def _ffn_kernel(final, h_ref, nw_ref, win_ref, wout_ref, nnw_ref, o_ref, *xn_out):
    hres = h_ref[...]
    xn = _rmsnorm(hres, nw_ref[...]).astype(BF16)
    gu = _dot(xn, win_ref[...])
    act = (_silu(gu[:, 0:D_FF]) * gu[:, D_FF:2 * D_FF]).astype(BF16)
    out = hres + _dot(act, wout_ref[...])
    if final:
        o_ref[...] = _rmsnorm(out, nnw_ref[...])
    else:
        o_ref[...] = out
        xn_out[0][...] = _rmsnorm(out, nnw_ref[...]).astype(BF16)


def _norm_kernel(x_ref, nw_ref, o_ref):
    o_ref[...] = _rmsnorm(x_ref[...], nw_ref[...]).astype(o_ref.dtype)


def _const_spec(shape):
    nd = len(shape)
    return pl.BlockSpec(shape, lambda i, _nd=nd: (0,) * _nd, pipeline_mode=pl.Buffered(1))


def _row_spec(rows, width):
    return pl.BlockSpec((rows, width), lambda i: (i, 0))


class _LayerWindow:
    def __init__(self, array, layer, width=None, col=0):
        self.array, self.layer, self.width, self.col = array, layer, width or array.shape[-1], col

    def spec(self):
        shape = self.array.shape
        mid = (0,) * (len(shape) - 2)
        layer, col = self.layer, self.col
        return pl.BlockSpec((None,) + shape[1:-1] + (self.width,), lambda i: (layer,) + mid + (col,),
                            pipeline_mode=pl.Buffered(1))


def _pcall(kernel, name, n_tokens, rows, operands, row_flags, outs, scratch):
    in_specs = [a.spec() if isinstance(a, _LayerWindow) else
                (_row_spec(rows, a.shape[1]) if is_row else _const_spec(a.shape))
                for a, is_row in zip(operands, row_flags)]
    operands = [a.array if isinstance(a, _LayerWindow) else a for a in operands]
    many = isinstance(outs, list)
    out_specs = [_row_spec(rows, wd) for wd, _ in (outs if many else [outs])]
    out_shape = [jax.ShapeDtypeStruct((n_tokens, wd), dt) for wd, dt in (outs if many else [outs])]
    return pl.pallas_call(
        kernel,
        name=name,
        grid=(n_tokens // rows,),
        in_specs=in_specs,
        out_specs=out_specs if many else out_specs[0],
        out_shape=out_shape if many else out_shape[0],
        scratch_shapes=scratch,
        compiler_params=pltpu.CompilerParams(dimension_semantics=("arbitrary",), vmem_limit_bytes=VMEM_LIMIT),
    )(*operands)


def _tril_const(t):
    return jnp.asarray(np.tril(np.ones((t, t), np.float32)), BF16)


def _gla_scratch(nk):
    ntile = nk // LANES
    per_block = [
        pltpu.VMEM((GLA_PARTS, T_BLK, nk), BF16),
        pltpu.VMEM((GLA_PARTS, T_BLK, nk), BF16),
        pltpu.VMEM((2 * T_BLK, nk), BF16),
        pltpu.VMEM((T_BLK, BRANCH_WIDTH), BF16),
    ]
    return [pltpu.VMEM((ntile, HEAD_V, LANES), F32)] + per_block * BLOCKS_PER_STEP


def _row(x):
    return x.reshape(1, -1).astype(F32)


PACK_MERGE = (4096, 0)
PACK_HGRN = (2048, 2)
PACK_SSD = (2048, 3)
PACK_GLA = (2048, 4)
PACK_LRU = (1024, 10)


PACK_COLS = 11 * 1024
PACK_ROWS = 128
IN_COLS = [0] + np.cumsum([512, 512, 512, 512, 512, 768, 8, 256, 256, 512, 512, 16, 512, 512, 4096]).tolist()


def _pack_kernel(w_ref, wg_ref, o_ref):
    c = IN_COLS

    def put(dst, lo, hi):
        o_ref[:, dst:dst + hi - lo] = w_ref[:, lo:hi].astype(BF16)
        return dst + hi - lo

    def pad(lo, hi):
        o_ref[:, lo:hi] = jnp.zeros((o_ref.shape[0], hi - lo), BF16)

    put(PACK_MERGE[0] * PACK_MERGE[1], c[14], c[15])
    put(PACK_HGRN[0] * PACK_HGRN[1], c[0], c[4])
    start = PACK_SSD[0] * PACK_SSD[1]
    pad(put(start, c[4], c[7]), start + PACK_SSD[0])
    start = PACK_GLA[0] * PACK_GLA[1]
    o_ref[:, start:start + GLA_KDIM] = wg_ref[...].astype(BF16)
    pad(put(start + GLA_KDIM, c[7], c[11]), start + PACK_GLA[0])
    put(PACK_LRU[0] * PACK_LRU[1], c[12], c[14])


def _pack_in_proj(w_in, gla_gate_w):
    depth, rows, cols = w_in.shape
    w_gate = jnp.einsum("lkr,lrn->lkn", w_in[:, :, IN_COLS[11]:IN_COLS[12]].astype(F32), gla_gate_w.astype(F32),
                        precision=lax.Precision.HIGHEST)
    cols = pl.cdiv(cols, LANES) * LANES
    w_in = jnp.pad(w_in.astype(BF16), ((0, 0), (0, 0), (0, cols - w_in.shape[2])))
    return pl.pallas_call(
        _pack_kernel,
        name="pack",
        grid=(depth, rows // PACK_ROWS),
        in_specs=[pl.BlockSpec((None, PACK_ROWS, cols), lambda l, i: (l, i, 0)),
                  pl.BlockSpec((None, PACK_ROWS, GLA_KDIM), lambda l, i: (l, i, 0))],
        out_specs=pl.BlockSpec((None, PACK_ROWS, PACK_COLS), lambda l, i: (l, i, 0)),
        out_shape=jax.ShapeDtypeStruct((depth, rows, PACK_COLS), BF16),
        compiler_params=pltpu.CompilerParams(dimension_semantics=("arbitrary", "arbitrary"),
                                             vmem_limit_bytes=VMEM_LIMIT),
    )(w_in, w_gate)


def _layer(h, xn, seq, l, wts, p):
    n_tokens = h.shape[0]
    bps = seq // T_BLK
    assert seq % T_STEP == 0 and n_tokens % seq == 0
    tril = _tril_const(T_BLK)
    dmat = _gla_decay_matrix(T_BLK)
    part_ids = _gla_part_ids()
    w_all, w_br, w_o, w_fi, w_fo = wts
    win = lambda pack: _LayerWindow(w_all, l, *pack)

    lb = p["lb"].astype(F32)
    ya = _pcall(
        functools.partial(_hgrn_kernel, bps), "hgrn2", n_tokens, T_STEP,
        [xn, win(PACK_HGRN), _row(lb), _row(1.0 - lb), _row(p["hgrn_norm_w"]), dmat, part_ids],
        [True] + [False] * 6, (BRANCH_WIDTH, BF16), _gla_scratch(BRANCH_WIDTH))

    pad8 = lambda v: jnp.concatenate([v.astype(F32), jnp.zeros((LANES - SSD_HEADS,), F32)]).reshape(1, LANES)
    expand = jnp.asarray(np.arange(LANES)[:, None] == (np.arange(BRANCH_WIDTH)[None, :] // SSD_HEADDIM), BF16)
    yb = _pcall(
        functools.partial(_ssd_kernel, bps), "ssd", n_tokens, T_STEP,
        [xn, win(PACK_SSD), p["ssd_conv_w"].astype(F32), _row(p["ssd_conv_b"]), pad8(p["ssd_dt_bias"]),
         pad8(-jnp.exp(p["ssd_a_log"].astype(F32))), _row(jnp.repeat(p["ssd_d"], SSD_HEADDIM)), _row(p["ssd_norm_w"]),
         tril, expand],
        [True] + [False] * 9, (BRANCH_WIDTH, BF16),
        [pltpu.VMEM((T_STEP + SUBLANES, SSD_CONV_DIM), F32), pltpu.VMEM((SSD_GROUPS * SSD_STATE, BRANCH_WIDTH), F32)]
        + [pltpu.VMEM((T_BLK, BRANCH_WIDTH), F32)] * BLOCKS_PER_STEP)

    yc = _pcall(
        functools.partial(_gla_kernel, bps), "gla", n_tokens, T_STEP,
        [xn, win(PACK_GLA), _row(p["gla_gate_b"]), _row(p["gla_norm_w"]), dmat, part_ids],
        [True] + [False] * 5, (BRANCH_WIDTH, BF16), _gla_scratch(GLA_KDIM))

    eye = jnp.eye(LRU_BLOCKS, dtype=F32)
    bdiag = lambda wt: jnp.einsum("kde,kl->kdle", wt.astype(F32), eye).reshape(BRANCH_WIDTH, BRANCH_WIDTH)
    w_gates = jnp.concatenate([bdiag(p["lru_wa"]), bdiag(p["lru_wx"])], axis=1).astype(BF16)
    b_gates = _row(jnp.concatenate([p["lru_ba"], p["lru_bx"]]))
    yd = _pcall(
        functools.partial(_lru_kernel, bps), "rglru", n_tokens, T_STEP,
        [xn, win(PACK_LRU), p["lru_conv_w"].astype(F32), _row(p["lru_conv_b"]), w_gates,
         b_gates, _row(p["lru_lambda"])],
        [True] + [False] * 6, (BRANCH_WIDTH, BF16),
        [pltpu.VMEM((T_STEP + SUBLANES, BRANCH_WIDTH), F32), pltpu.VMEM((1, BRANCH_WIDTH), F32)]
        + [pltpu.VMEM((2 * BRANCH_WIDTH // LANES, T_BLK, LANES), F32),
           pltpu.VMEM((T_BLK // SUBLANES, BRANCH_WIDTH), F32)] * BLOCKS_PER_STEP)

    h = _pcall(
        _merge_kernel, "merge", n_tokens, T_STEP,
        [h, xn, ya, yb, yc, yd, win(PACK_MERGE), _LayerWindow(w_br, l), _LayerWindow(w_o, l)],
        [True, True, True, True, True, True, False, False, False], (D_MODEL, F32), [])

    outs = (D_MODEL, F32) if p["final"] else [(D_MODEL, F32), (D_MODEL, BF16)]
    res = _pcall(
        functools.partial(_ffn_kernel, p["final"]), "swiglu", n_tokens, T_STEP,
        [h, _row(p["norm_ffn_w"]), _LayerWindow(w_fi, l), _LayerWindow(w_fo, l), _row(p["next_norm_w"])],
        [True, False, False, False, False], outs, [])
    return (res, None) if p["final"] else tuple(res)


def kernel(x, norm_mix_w, w_in, hgrn_lower_bounds, hgrn_norm_w, ssd_conv_w, ssd_conv_b, ssd_dt_bias, ssd_a_log, ssd_d, ssd_norm_w, gla_gate_w, gla_gate_b, gla_norm_w, lru_conv_w, lru_conv_b, lru_wa, lru_ba, lru_wx, lru_bx, lru_lambda, w_branch, w_out, norm_ffn_w, w_ffn_in, w_ffn_out, norm_f_w):
    bsz, seq, _ = x.shape
    depth = w_in.shape[0]
    assert (bsz * seq) % NORM_ROWS == 0
    lb_all = jnp.cumsum(jax.nn.softmax(hgrn_lower_bounds.astype(F32), axis=0), axis=0)
    lb_all = lb_all - lb_all[0:1]
    wts = (_pack_in_proj(w_in, gla_gate_w), w_branch.astype(BF16), w_out.astype(BF16), w_ffn_in.astype(BF16),
           w_ffn_out.astype(BF16))
    h = x.reshape(bsz * seq, D_MODEL)
    xn = _pcall(_norm_kernel, "norm", bsz * seq, NORM_ROWS, [h, _row(norm_mix_w[0])], [True, False], (D_MODEL, BF16), [])
    for l in range(depth):
        p = dict(
            lb=lb_all[l], hgrn_norm_w=hgrn_norm_w[l],
            ssd_conv_w=ssd_conv_w[l], ssd_conv_b=ssd_conv_b[l], ssd_dt_bias=ssd_dt_bias[l], ssd_a_log=ssd_a_log[l],
            ssd_d=ssd_d[l], ssd_norm_w=ssd_norm_w[l], gla_gate_b=gla_gate_b[l],
            gla_norm_w=gla_norm_w[l], lru_conv_w=lru_conv_w[l], lru_conv_b=lru_conv_b[l], lru_wa=lru_wa[l],
            lru_ba=lru_ba[l], lru_wx=lru_wx[l], lru_bx=lru_bx[l], lru_lambda=lru_lambda[l],
            norm_ffn_w=norm_ffn_w[l],
            next_norm_w=(norm_f_w if l == depth - 1 else norm_mix_w[l + 1]), final=(l == depth - 1))
        h, xn = _layer(h, xn, seq, l, wts, p)
    return h.reshape(bsz, seq, D_MODEL)
```

```python
import functools
import math

import jax
import jax.numpy as jnp
import numpy as np
from jax import lax
from jax.experimental import pallas as pl
from jax.experimental.pallas import tpu as pltpu

F32 = jnp.float32
BF16 = jnp.bfloat16

D_MODEL = 1024
BRANCH_WIDTH = 512
N_BRANCHES = 4
CONV_WIDTH = 4
NORM_EPS = 1e-6
HGRN_HEADS = 4
SSD_HEADS = 8
SSD_HEADDIM = 64
SSD_GROUPS = 2
SSD_STATE = 64
SSD_CONV_DIM = 768
GLA_HEADS = 4
GLA_KDIM = 256
GLA_HEAD_K = 64
GLA_GATE_RANK = 16
GLA_GATE_NORMALIZER = 16.0
LRU_BLOCKS = 8
LRU_BLOCK = 64
LRU_C = 8.0
D_FF = 2816

LANES = 128
SUBLANES = 8
T_BLK = 256
BLOCKS_PER_STEP = 2
T_STEP = BLOCKS_PER_STEP * T_BLK
NORM_ROWS = 2048
HEAD_V = 128
VMEM_LIMIT = 48 * 1024 * 1024
F32_FLOOR = 1e-37
SSD_PROJ = BRANCH_WIDTH + SSD_CONV_DIM + LANES
GLA_PROJ = 3 * GLA_KDIM + 2 * BRANCH_WIDTH


def _dot(a, b):
    return jnp.dot(a, b, preferred_element_type=F32)


def _dot_nt(a, b):
    return lax.dot_general(a, b, (((1,), (1,)), ((), ())), preferred_element_type=F32)


def _dot_tn(a, b):
    return lax.dot_general(a, b, (((0,), (0,)), ((), ())), preferred_element_type=F32)


def _split3(x):
    hi = x.astype(BF16)
    r1 = x - hi.astype(F32)
    mid = r1.astype(BF16)
    lo = (r1 - mid.astype(F32)).astype(BF16)
    return hi, mid, lo


def _sel_dot(mat, x):
    hi, mid, lo = _split3(x)
    return (_dot(mat, lo) + _dot(mat, mid)) + _dot(mat, hi)


def _dot_sel(x, mat):
    hi, mid, lo = _split3(x)
    return (_dot(lo, mat) + _dot(mid, mat)) + _dot(hi, mat)


def _sigmoid(x):
    return jax.nn.sigmoid(x)


def _silu(x):
    return x * _sigmoid(x)


def _softplus(x):
    return jnp.maximum(x, 0.0) + jnp.log1p(jnp.exp(-jnp.abs(x)))


def _log_sigmoid(x):
    return jnp.minimum(x, 0.0) - jnp.log1p(jnp.exp(-jnp.abs(x)))


def _rmsnorm(x, w):
    return x * lax.rsqrt(jnp.mean(x * x, axis=-1, keepdims=True) + NORM_EPS) * w


GLA_SMALL = (1, 2, 4)
GLA_MID = (8, 16, 32, 64)
GLA_PARTS = 2 + len(GLA_SMALL) + len(GLA_MID) + 1
LOG2E = 1.4426950408889634
GLA_SPAN_LIMIT = 48.0
HGRN_BASE = 32
GLA_BASE = 128


def _gla_part_ids():
    row = np.arange(LANES)[:, None]
    col = np.arange(LANES)[None, :]
    x = row ^ col
    ids = np.where(x == 0, 0, np.floor(np.log2(np.maximum(x, 1))).astype(np.int64) + 1)
    return jnp.asarray(np.where(col <= row, ids, -1), jnp.int32)


def _gla_decay_matrix(t):
    u = np.arange(t)[None, :]
    r = np.arange(t)[:, None]
    blocks = [u <= r]
    for s in GLA_SMALL:
        mid = (r // (2 * s)) * (2 * s) + s - 1
        blocks.append(((u > mid) & (u <= r)) | ((u > r) & (u <= mid)))
    m = np.concatenate(blocks, axis=0)
    return jnp.asarray(np.concatenate([m, m], axis=1), BF16)


def _store_decay_pieces(rhs_ref, logf):
    t = logf.shape[0]
    lf2 = logf * LOG2E
    hi = lf2.astype(BF16)
    rhs_ref[0:t, :] = hi
    rhs_ref[t:2 * t, :] = (lf2 - hi.astype(F32)).astype(BF16)


def _gla_chunk(q, k, logf, v_ref, st_ref, ql_ref, kl_ref, rhs_ref, dm_ref, ids_ref, hpt, base, emit):
    t, nk = q.shape
    assert t == 2 * LANES
    ntile = nk // LANES
    dk = LANES // hpt
    levels = GLA_SMALL + GLA_MID

    _store_decay_pieces(rhs_ref, logf)
    bcum = _dot(dm_ref[0:t, :], rhs_ref[...])
    blast = bcum[t - 1:t, :]
    qb = q.astype(BF16)
    kb = k.astype(BF16)
    ql_ref[0] = qb * jnp.exp2(bcum).astype(BF16)
    kl_ref[0] = kb * jnp.exp2(blast - bcum).astype(BF16)

    def level_part(part, s):
        bc3 = bcum.reshape(t // (2 * s), 2 * s, nk)
        ref = bc3[:, s - 1:s, :]
        e = jnp.exp2(jnp.concatenate([ref - bc3[:, 0:s, :], bc3[:, s:2 * s, :] - ref], axis=1))
        e = e.reshape(t, nk).astype(BF16)
        ql_ref[part] = qb * e
        kl_ref[part] = kb * e

    part_of = {}
    part = 1
    for s in levels + (LANES,):
        if s >= base:
            level_part(part, s)
            part_of[s] = part
            part += 1
    free = part
    p_top = part_of[LANES]

    bcb = bcum.reshape(t // base, base, nk)
    span = bcb - bcb[:, base // 2 - 1:base // 2, :]
    fits = jnp.max(jnp.abs(span)) <= GLA_SPAN_LIMIT

    ids = ids_ref[...]
    lane = lax.broadcasted_iota(jnp.int32, (1, LANES), 1)

    def own(x, sub):
        return x if hpt == 1 else jnp.where((lane // dk) == sub, x, jnp.zeros_like(x))

    def stacked(x):
        return x if hpt == 1 else jnp.concatenate([own(x, sub) for sub in range(hpt)], axis=0)

    ids_st = ids if hpt == 1 else jnp.concatenate([ids] * hpt, axis=0)

    def finish(first, overrides):
        intra = []
        for tau in range(ntile):
            tl = slice(tau * LANES, (tau + 1) * LANES)
            diag = []
            for c in range(2):
                rows = slice(c * LANES, (c + 1) * LANES)
                sc = jnp.where(ids_st >= 0, _dot_nt(stacked(ql_ref[first, rows, tl]), kl_ref[first, rows, tl]), 0.0)
                for pid, p in overrides:
                    sc = jnp.where(ids_st == pid, _dot_nt(stacked(ql_ref[p, rows, tl]), kl_ref[p, rows, tl]), sc)
                diag.append(sc.astype(BF16))
            low = _dot_nt(stacked(ql_ref[p_top, LANES:t, tl]), kl_ref[p_top, 0:LANES, tl]).astype(BF16)
            for sub in range(hpt):
                hv = slice((tau * hpt + sub) * HEAD_V, (tau * hpt + sub + 1) * HEAD_V)
                hr = slice(sub * LANES, (sub + 1) * LANES)
                o_top = _dot(diag[0][hr], v_ref[0:LANES, hv])
                o_bot = _dot(jnp.concatenate([low[hr], diag[1][hr]], axis=1), v_ref[:, hv])
                intra.append(jnp.concatenate([o_top, o_bot], axis=0))
            yield
        for tau in range(ntile):
            tl = slice(tau * LANES, (tau + 1) * LANES)
            inter = _dot_nt(stacked(ql_ref[0, :, tl]), st_ref[tau].astype(BF16))
            for sub in range(hpt):
                h = tau * hpt + sub
                emit(h, intra[h] + inter[sub * t:(sub + 1) * t])
        for tau in range(ntile):
            tl = slice(tau * LANES, (tau + 1) * LANES)
            kd = kl_ref[0, :, tl]
            upd = _dot_tn(v_ref[:, tau * hpt * HEAD_V:(tau * hpt + 1) * HEAD_V], kd)
            for sub in range(1, hpt):
                h = tau * hpt + sub
                upd = jnp.where((lane // dk) == sub, _dot_tn(v_ref[:, h * HEAD_V:(h + 1) * HEAD_V], kd), upd)
            st_ref[tau] = st_ref[tau] * jnp.exp2(blast[:, tl]) + upd

    def part_id(s):
        return 1 + levels.index(s)

    upper = [(part_id(s), part_of[s]) for s in levels if s >= base]

    def one_reference():
        ql_ref[free] = qb * jnp.exp2(span).reshape(t, nk).astype(BF16)
        kl_ref[free] = kb * jnp.exp2(-span).reshape(t, nk).astype(BF16)
        yield from finish(free, upper)

    def safe_parts():
        d3 = _dot(dm_ref[t:(1 + len(GLA_SMALL)) * t, :], rhs_ref[...])
        lower = []
        part = free
        for i, s in enumerate(GLA_SMALL):
            if s < base:
                e = jnp.exp2(d3[i * t:(i + 1) * t]).astype(BF16)
                ql_ref[part] = qb * e
                kl_ref[part] = kb * e
                lower.append((part_id(s), part))
                part += 1
        for s in GLA_MID:
            if s < base:
                level_part(part, s)
                lower.append((part_id(s), part))
                part += 1
        ql_ref[part] = qb
        kl_ref[part] = kb
        yield from finish(part, lower + upper)

    return fits, one_reference, safe_parts


def _interleave(chains):
    live = list(chains)
    while live:
        for c in list(live):
            try:
                next(c)
            except StopIteration:
                live.remove(c)


def _gla_finish(chunks):
    fits = functools.reduce(jnp.logical_and, [c[0] for c in chunks])

    @pl.when(fits)
    def _():
        _interleave([c[1]() for c in chunks])

    @pl.when(jnp.logical_not(fits))
    def _():
        _interleave([c[2]() for c in chunks])


def _causal_conv(buf_ref, x, w_ref, b_ref, base):
    t = x.shape[0]
    buf_ref[SUBLANES + base:SUBLANES + base + t, :] = x
    acc = b_ref[...] + w_ref[CONV_WIDTH - 1:CONV_WIDTH, :] * x
    for s in range(1, CONV_WIDTH):
        acc = acc + w_ref[CONV_WIDTH - 1 - s:CONV_WIDTH - s, :] * buf_ref[pl.ds(SUBLANES + base - s, t), :]
    return acc


def _step_start(bps, *carry_refs):
    @pl.when((pl.program_id(0) * BLOCKS_PER_STEP) % bps == 0)
    def _():
        for ref in carry_refs:
            ref[...] = jnp.zeros_like(ref)


def _conv_carry(buf_ref):
    buf_ref[0:SUBLANES, :] = buf_ref[T_STEP:T_STEP + SUBLANES, :]


def _hgrn_kernel(bps, xn_ref, w_ref, lb_ref, oml_ref, hnw_ref, dm_ref, ids_ref,
                 o_ref, st_ref, *scratch):
    _step_start(bps, st_ref)
    w = BRANCH_WIDTH
    staged = []
    for blk in range(BLOCKS_PER_STEP):
        rows = slice(blk * T_BLK, (blk + 1) * T_BLK)
        ql_ref, kl_ref, rhs_ref, v_ref = scratch[4 * blk:4 * blk + 4]
        proj = _dot(xn_ref[rows, :], w_ref[...])
        fp = proj[:, w:2 * w]
        u = jnp.exp(-jnp.abs(fp))
        r = 1.0 / (1.0 + u)
        pos = fp >= 0.0
        sig = jnp.where(pos, r, u * r)
        key = oml_ref[...] * jnp.where(pos, u * r, r)
        logf = jnp.log(jnp.maximum(lb_ref[...] + oml_ref[...] * sig, F32_FLOOR))
        qf = _silu(proj[:, 0:w])
        v_ref[...] = proj[:, 2 * w:3 * w].astype(BF16)
        staged.append((qf, key, logf, _silu(proj[:, 3 * w:4 * w])))
    chunks = []
    for blk in range(BLOCKS_PER_STEP):
        rows = slice(blk * T_BLK, (blk + 1) * T_BLK)
        ql_ref, kl_ref, rhs_ref, v_ref = scratch[4 * blk:4 * blk + 4]
        qf, key, logf, gs = staged[blk]

        def emit(h, o, rows=rows, gs=gs):
            cols = slice(h * HEAD_V, (h + 1) * HEAD_V)
            o_ref[rows, cols] = (_rmsnorm(o, hnw_ref[...]) * gs[:, cols]).astype(o_ref.dtype)

        chunks.append(_gla_chunk(qf, key, logf, v_ref, st_ref, ql_ref, kl_ref, rhs_ref, dm_ref, ids_ref, 1,
                                 HGRN_BASE, emit))
    _gla_finish(chunks)


def _gla_kernel(bps, xn_ref, w_ref, gb_ref, gnw_ref, dm_ref, ids_ref, o_ref, st_ref, *scratch):
    _step_start(bps, st_ref)
    kd, w = GLA_KDIM, BRANCH_WIDTH
    chunks = []
    for blk in range(BLOCKS_PER_STEP):
        rows = slice(blk * T_BLK, (blk + 1) * T_BLK)
        ql_ref, kl_ref, rhs_ref, v_ref = scratch[4 * blk:4 * blk + 4]
        proj = _dot(xn_ref[rows, :], w_ref[:, 0:GLA_PROJ])
        logf = _log_sigmoid(proj[:, 0:kd] + gb_ref[...]) * (1.0 / GLA_GATE_NORMALIZER)
        proj = proj[:, kd:]
        q = proj[:, 0:kd] * (GLA_HEAD_K ** -0.5)
        k = proj[:, kd:2 * kd]
        v_ref[...] = proj[:, 2 * kd:2 * kd + w].astype(BF16)
        gs = _silu(proj[:, 2 * kd + w:2 * kd + 2 * w])

        def emit(h, o, rows=rows, gs=gs):
            cols = slice(h * HEAD_V, (h + 1) * HEAD_V)
            o_ref[rows, cols] = (_rmsnorm(o, gnw_ref[...]) * gs[:, cols]).astype(o_ref.dtype)

        chunks.append(_gla_chunk(q, k, logf, v_ref, st_ref, ql_ref, kl_ref, rhs_ref, dm_ref, ids_ref, 2,
                                 GLA_BASE, emit))
    _gla_finish(chunks)


def _ssd_kernel(bps, xn_ref, w_ref, cw_ref, cb_ref, dtb_ref, aneg_ref, dsk_ref, snw_ref, tril_ref,
                exp_ref, o_ref, buf_ref, st_ref, *y_refs):
    _step_start(bps, st_ref, buf_ref)
    for blk in range(BLOCKS_PER_STEP):
        _ssd_block(blk, xn_ref, w_ref, cw_ref, cb_ref, dtb_ref, aneg_ref, dsk_ref, snw_ref, tril_ref,
                   exp_ref, o_ref, buf_ref, st_ref, y_refs[blk])
    _conv_carry(buf_ref)


def _ssd_block(blk, xn_ref, w_ref, cw_ref, cb_ref, dtb_ref, aneg_ref, dsk_ref, snw_ref, tril_ref,
               exp_ref, o_ref, buf_ref, st_ref, y_ref):
    t = T_BLK
    rows = slice(blk * t, (blk + 1) * t)
    w = BRANCH_WIDTH
    gs = SSD_GROUPS * SSD_STATE
    proj = _dot(xn_ref[rows, :], w_ref[:, 0:SSD_PROJ])
    z = proj[:, 0:w]
    xbc = _silu(_causal_conv(buf_ref, proj[:, w:w + SSD_CONV_DIM], cw_ref, cb_ref, blk * t))
    xs = xbc[:, 0:w]
    bm = xbc[:, w:w + gs].astype(BF16)
    cm = xbc[:, w + gs:w + 2 * gs]
    dt = _softplus(proj[:, w + SSD_CONV_DIM:] + dtb_ref[...])
    acs = _sel_dot(tril_ref[...], dt * aneg_ref[...])
    dt_e = _dot_sel(dt, exp_ref[...])
    acs_e = _dot_sel(acs, exp_ref[...])
    alast_e = acs_e[t - 1:t, :]
    xdt = xs * dt_e
    xdt_bf = xdt.astype(BF16)

    row = lax.broadcasted_iota(jnp.int32, (t, t), 0)
    col = lax.broadcasted_iota(jnp.int32, (t, t), 1)
    causal = col <= row
    lane = lax.broadcasted_iota(jnp.int32, (1, LANES), 1)
    acs_t = acs.T
    gmats = []
    for g in range(SSD_GROUPS):
        cg = jnp.where((lane // SSD_STATE) == g, cm, 0.0).astype(BF16)
        gmats.append(_dot_nt(cg, bm))
    hpg = SSD_HEADS // SSD_GROUPS
    for pair in range(SSD_HEADS // 2):
        tl = slice(pair * LANES, (pair + 1) * LANES)
        ys = []
        for sub in range(2):
            h = 2 * pair + sub
            seg = acs[:, h:h + 1] - acs_t[h:h + 1, :]
            m = jnp.where(causal, gmats[h // hpg] * jnp.exp(seg), 0.0).astype(BF16)
            ys.append(_dot(m, xdt_bf[:, tl]))
        y_diag = jnp.where(lane < SSD_HEADDIM, ys[0], ys[1])
        y_off = _dot(cm.astype(BF16), st_ref[:, tl].astype(BF16)) * jnp.exp(acs_e[:, tl])
        y = y_diag + y_off + xs[:, tl] * dsk_ref[:, tl]
        y_ref[:, tl] = y * _silu(z[:, tl])
    xdec = (xdt * jnp.exp(alast_e - acs_e)).astype(BF16)
    upd = _dot_tn(bm, xdec)
    srow = lax.broadcasted_iota(jnp.int32, (gs, w), 0) // SSD_STATE
    scol = lax.broadcasted_iota(jnp.int32, (gs, w), 1) // (hpg * SSD_HEADDIM)
    st_ref[...] = st_ref[...] * jnp.exp(alast_e) + jnp.where(srow == scol, upd, 0.0)
    gw = w // SSD_GROUPS
    for g in range(SSD_GROUPS):
        yg = y_ref[:, g * gw:(g + 1) * gw]
        o_ref[rows, g * gw:(g + 1) * gw] = (
            yg * lax.rsqrt(jnp.mean(yg * yg, axis=-1, keepdims=True) + NORM_EPS) * snw_ref[:, g * gw:(g + 1) * gw]
        ).astype(o_ref.dtype)


def _sqrt_one_minus_exp2x(x):
    th = jnp.tanh(x)
    y = (-2.0 * th) / (1.0 - th)
    return y * lax.rsqrt(jnp.maximum(y, 1e-30))


def _doubling_scan(a, b, pos, n, axis):
    off = 1
    while off < n:
        keep = pos >= off
        a_sh = jnp.where(keep, pltpu.roll(a, off, axis), 1.0)
        b_sh = jnp.where(keep, pltpu.roll(b, off, axis), 0.0)
        b = a * b_sh + b
        a = a * a_sh
        off *= 2
    return a, b


def _scan_rows(a, b, hc_ref, ab_ref, cg_ref):
    t, n = a.shape
    g = t // SUBLANES
    nt = n // LANES
    a8, b8 = _doubling_scan(a.reshape(g, SUBLANES, n), b.reshape(g, SUBLANES, n),
                            lax.broadcasted_iota(jnp.int32, (1, SUBLANES, 1), 1), SUBLANES, 1)
    a8 = a8.reshape(t, n)
    b8 = b8.reshape(t, n)
    for i in range(nt):
        ab_ref[i] = a8[:, i * LANES:(i + 1) * LANES]
        ab_ref[nt + i] = b8[:, i * LANES:(i + 1) * LANES]
    last = pl.ds(SUBLANES - 1, g, stride=SUBLANES)
    gidx = lax.broadcasted_iota(jnp.int32, (g, 1), 0)
    ag = jnp.concatenate([ab_ref[i, last, :] for i in range(nt)], axis=1)
    bg = jnp.concatenate([ab_ref[nt + i, last, :] for i in range(nt)], axis=1)
    ag, bg = _doubling_scan(ag, bg, gidx, g, 0)
    state = bg + ag * hc_ref[...]
    cg_ref[...] = jnp.where(gidx >= 1, pltpu.roll(state, 1, 0), hc_ref[...])
    hc_ref[...] = state[g - 1:g, :]
    out = []
    for i in range(nt):
        cg = cg_ref[:, i * LANES:(i + 1) * LANES].reshape(g, 1, LANES)
        hi = ab_ref[nt + i].reshape(g, SUBLANES, LANES) + ab_ref[i].reshape(g, SUBLANES, LANES) * cg
        out.append(hi.reshape(t, LANES))
    return jnp.concatenate(out, axis=1)


def _lru_kernel(bps, xn_ref, w_ref, cw_ref, cb_ref, wg_ref, bg_ref, lam_ref, o_ref, buf_ref, hc_ref,
                *scratch):
    _step_start(bps, hc_ref, buf_ref)
    for blk in range(BLOCKS_PER_STEP):
        _lru_block(blk, xn_ref, w_ref, cw_ref, cb_ref, wg_ref, bg_ref, lam_ref, o_ref, buf_ref, hc_ref,
                   *scratch[2 * blk:2 * blk + 2])
    _conv_carry(buf_ref)


def _lru_block(blk, xn_ref, w_ref, cw_ref, cb_ref, wg_ref, bg_ref, lam_ref, o_ref, buf_ref, hc_ref,
               ab_ref, cg_ref):
    t = T_BLK
    rows = slice(blk * t, (blk + 1) * t)
    w = BRANCH_WIDTH
    proj = _dot(xn_ref[rows, :], w_ref[...])
    u = _causal_conv(buf_ref, proj[:, 0:w], cw_ref, cb_ref, blk * t)
    rg = _dot(u.astype(BF16), wg_ref[...]) + bg_ref[...]
    r = _sigmoid(rg[:, 0:w])
    ig = _sigmoid(rg[:, w:2 * w])
    log_a = (-LRU_C) * r * _softplus(-lam_ref[...])
    a = jnp.exp(log_a)
    b = _sqrt_one_minus_exp2x(log_a) * (ig * u)
    hseq = _scan_rows(a, b, hc_ref, ab_ref, cg_ref)
    gate = proj[:, w:2 * w]
    gelu = 0.5 * gate * (1.0 + jnp.tanh(math.sqrt(2.0 / math.pi) * (gate + 0.044715 * gate * gate * gate)))
    o_ref[rows, :] = (hseq * gelu).astype(o_ref.dtype)


def _merge_kernel(h_ref, xn_ref, ya_ref, yb_ref, yc_ref, yd_ref, wmg_ref, wbr_ref, wout_ref, o_ref):
    xn = xn_ref[...]
    merged = None
    for n, y_ref in enumerate((ya_ref, yb_ref, yc_ref, yd_ref)):
        gate = _sigmoid(_dot(xn, wmg_ref[:, n * D_MODEL:(n + 1) * D_MODEL]))
        term = gate * _dot(y_ref[...], wbr_ref[n])
        merged = term if merged is None else merged + term
    o_ref[...] = h_ref[...] + _dot(merged.astype(BF16), wout_ref[...])


def _ffn_kernel(final, h_ref, nw_ref, win_ref, wout_ref, nnw_ref, o_ref, *xn_out):
    hres = h_ref[...]
    xn = _rmsnorm(hres, nw_ref[...]).astype(BF16)
    gu = _dot(xn, win_ref[...])
    act = (_silu(gu[:, 0:D_FF]) * gu[:, D_FF:2 * D_FF]).astype(BF16)
    out = hres + _dot(act, wout_ref[...])
    if final:
        o_ref[...] = _rmsnorm(out, nnw_ref[...])
    else:
        o_ref[...] = out
        xn_out[0][...] = _rmsnorm(out, nnw_ref[...]).astype(BF16)


def _norm_kernel(x_ref, nw_ref, o_ref):
    o_ref[...] = _rmsnorm(x_ref[...], nw_ref[...]).astype(o_ref.dtype)


def _const_spec(shape):
    nd = len(shape)
    return pl.BlockSpec(shape, lambda i, _nd=nd: (0,) * _nd, pipeline_mode=pl.Buffered(1))


def _row_spec(rows, width):
    return pl.BlockSpec((rows, width), lambda i: (i, 0))


class _LayerWindow:
    def __init__(self, array, layer, width=None, col=0):
        self.array, self.layer, self.width, self.col = array, layer, width or array.shape[-1], col

    def spec(self):
        shape = self.array.shape
        mid = (0,) * (len(shape) - 2)
        layer, col = self.layer, self.col
        return pl.BlockSpec((None,) + shape[1:-1] + (self.width,), lambda i: (layer,) + mid + (col,),
                            pipeline_mode=pl.Buffered(1))


def _pcall(kernel, name, n_tokens, rows, operands, row_flags, outs, scratch):
    in_specs = [a.spec() if isinstance(a, _LayerWindow) else
                (_row_spec(rows, a.shape[1]) if is_row else _const_spec(a.shape))
                for a, is_row in zip(operands, row_flags)]
    operands = [a.array if isinstance(a, _LayerWindow) else a for a in operands]
    many = isinstance(outs, list)
    out_specs = [_row_spec(rows, wd) for wd, _ in (outs if many else [outs])]
    out_shape = [jax.ShapeDtypeStruct((n_tokens, wd), dt) for wd, dt in (outs if many else [outs])]
    return pl.pallas_call(
        kernel,
        name=name,
        grid=(n_tokens // rows,),
        in_specs=in_specs,
        out_specs=out_specs if many else out_specs[0],
        out_shape=out_shape if many else out_shape[0],
        scratch_shapes=scratch,
        compiler_params=pltpu.CompilerParams(dimension_semantics=("arbitrary",), vmem_limit_bytes=VMEM_LIMIT),
    )(*operands)


def _tril_const(t):
    return jnp.asarray(np.tril(np.ones((t, t), np.float32)), BF16)


def _gla_scratch(nk):
    ntile = nk // LANES
    per_block = [
        pltpu.VMEM((GLA_PARTS, T_BLK, nk), BF16),
        pltpu.VMEM((GLA_PARTS, T_BLK, nk), BF16),
        pltpu.VMEM((2 * T_BLK, nk), BF16),
        pltpu.VMEM((T_BLK, BRANCH_WIDTH), BF16),
    ]
    return [pltpu.VMEM((ntile, HEAD_V, LANES), F32)] + per_block * BLOCKS_PER_STEP


def _row(x):
    return x.reshape(1, -1).astype(F32)


PACK_MERGE = (4096, 0)
PACK_HGRN = (2048, 2)
PACK_SSD = (2048, 3)
PACK_GLA = (2048, 4)
PACK_LRU = (1024, 10)


PACK_COLS = 11 * 1024
PACK_ROWS = 128
IN_COLS = [0] + np.cumsum([512, 512, 512, 512, 512, 768, 8, 256, 256, 512, 512, 16, 512, 512, 4096]).tolist()


def _pack_kernel(w_ref, wg_ref, o_ref):
    c = IN_COLS

    def put(dst, lo, hi):
        o_ref[:, dst:dst + hi - lo] = w_ref[:, lo:hi].astype(BF16)
        return dst + hi - lo

    def pad(lo, hi):
        o_ref[:, lo:hi] = jnp.zeros((o_ref.shape[0], hi - lo), BF16)

    put(PACK_MERGE[0] * PACK_MERGE[1], c[14], c[15])
    put(PACK_HGRN[0] * PACK_HGRN[1], c[0], c[4])
    start = PACK_SSD[0] * PACK_SSD[1]
    pad(put(start, c[4], c[7]), start + PACK_SSD[0])
    start = PACK_GLA[0] * PACK_GLA[1]
    o_ref[:, start:start + GLA_KDIM] = wg_ref[...].astype(BF16)
    pad(put(start + GLA_KDIM, c[7], c[11]), start + PACK_GLA[0])
    put(PACK_LRU[0] * PACK_LRU[1], c[12], c[14])


def _pack_in_proj(w_in, gla_gate_w):
    depth, rows, cols = w_in.shape
    w_gate = jnp.einsum("lkr,lrn->lkn", w_in[:, :, IN_COLS[11]:IN_COLS[12]].astype(F32), gla_gate_w.astype(F32),
                        precision=lax.Precision.HIGHEST)
    cols = pl.cdiv(cols, LANES) * LANES
    w_in = jnp.pad(w_in.astype(BF16), ((0, 0), (0, 0), (0, cols - w_in.shape[2])))
    return pl.pallas_call(
        _pack_kernel,
        name="pack",
        grid=(depth, rows // PACK_ROWS),
        in_specs=[pl.BlockSpec((None, PACK_ROWS, cols), lambda l, i: (l, i, 0)),
                  pl.BlockSpec((None, PACK_ROWS, GLA_KDIM), lambda l, i: (l, i, 0))],
        out_specs=pl.BlockSpec((None, PACK_ROWS, PACK_COLS), lambda l, i: (l, i, 0)),
        out_shape=jax.ShapeDtypeStruct((depth, rows, PACK_COLS), BF16),
        compiler_params=pltpu.CompilerParams(dimension_semantics=("arbitrary", "arbitrary"),
                                             vmem_limit_bytes=VMEM_LIMIT),
    )(w_in, w_gate)


def _layer(h, xn, seq, l, wts, p):
    n_tokens = h.shape[0]
    bps = seq // T_BLK
    assert seq % T_STEP == 0 and n_tokens % seq == 0
    tril = _tril_const(T_BLK)
    dmat = _gla_decay_matrix(T_BLK)
    part_ids = _gla_part_ids()
    w_all, w_br, w_o, w_fi, w_fo = wts
    win = lambda pack: _LayerWindow(w_all, l, *pack)

    lb = p["lb"].astype(F32)
    ya = _pcall(
        functools.partial(_hgrn_kernel, bps), "hgrn2", n_tokens, T_STEP,
        [xn, win(PACK_HGRN), _row(lb), _row(1.0 - lb), _row(p["hgrn_norm_w"]), dmat, part_ids],
        [True] + [False] * 6, (BRANCH_WIDTH, BF16), _gla_scratch(BRANCH_WIDTH))

    pad8 = lambda v: jnp.concatenate([v.astype(F32), jnp.zeros((LANES - SSD_HEADS,), F32)]).reshape(1, LANES)
    expand = jnp.asarray(np.arange(LANES)[:, None] == (np.arange(BRANCH_WIDTH)[None, :] // SSD_HEADDIM), BF16)
    yb = _pcall(
        functools.partial(_ssd_kernel, bps), "ssd", n_tokens, T_STEP,
        [xn, win(PACK_SSD), p["ssd_conv_w"].astype(F32), _row(p["ssd_conv_b"]), pad8(p["ssd_dt_bias"]),
         pad8(-jnp.exp(p["ssd_a_log"].astype(F32))), _row(jnp.repeat(p["ssd_d"], SSD_HEADDIM)), _row(p["ssd_norm_w"]),
         tril, expand],
        [True] + [False] * 9, (BRANCH_WIDTH, BF16),
        [pltpu.VMEM((T_STEP + SUBLANES, SSD_CONV_DIM), F32), pltpu.VMEM((SSD_GROUPS * SSD_STATE, BRANCH_WIDTH), F32)]
        + [pltpu.VMEM((T_BLK, BRANCH_WIDTH), F32)] * BLOCKS_PER_STEP)

    yc = _pcall(
        functools.partial(_gla_kernel, bps), "gla", n_tokens, T_STEP,
        [xn, win(PACK_GLA), _row(p["gla_gate_b"]), _row(p["gla_norm_w"]), dmat, part_ids],
        [True] + [False] * 5, (BRANCH_WIDTH, BF16), _gla_scratch(GLA_KDIM))

    eye = jnp.eye(LRU_BLOCKS, dtype=F32)
    bdiag = lambda wt: jnp.einsum("kde,kl->kdle", wt.astype(F32), eye).reshape(BRANCH_WIDTH, BRANCH_WIDTH)
    w_gates = jnp.concatenate([bdiag(p["lru_wa"]), bdiag(p["lru_wx"])], axis=1).astype(BF16)
    b_gates = _row(jnp.concatenate([p["lru_ba"], p["lru_bx"]]))
    yd = _pcall(
        functools.partial(_lru_kernel, bps), "rglru", n_tokens, T_STEP,
        [xn, win(PACK_LRU), p["lru_conv_w"].astype(F32), _row(p["lru_conv_b"]), w_gates,
         b_gates, _row(p["lru_lambda"])],
        [True] + [False] * 6, (BRANCH_WIDTH, BF16),
        [pltpu.VMEM((T_STEP + SUBLANES, BRANCH_WIDTH), F32), pltpu.VMEM((1, BRANCH_WIDTH), F32)]
        + [pltpu.VMEM((2 * BRANCH_WIDTH // LANES, T_BLK, LANES), F32),
           pltpu.VMEM((T_BLK // SUBLANES, BRANCH_WIDTH), F32)] * BLOCKS_PER_STEP)

    h = _pcall(
        _merge_kernel, "merge", n_tokens, T_STEP,
        [h, xn, ya, yb, yc, yd, win(PACK_MERGE), _LayerWindow(w_br, l), _LayerWindow(w_o, l)],
        [True, True, True, True, True, True, False, False, False], (D_MODEL, F32), [])

    outs = (D_MODEL, F32) if p["final"] else [(D_MODEL, F32), (D_MODEL, BF16)]
    res = _pcall(
        functools.partial(_ffn_kernel, p["final"]), "swiglu", n_tokens, T_STEP,
        [h, _row(p["norm_ffn_w"]), _LayerWindow(w_fi, l), _LayerWindow(w_fo, l), _row(p["next_norm_w"])],
        [True, False, False, False, False], outs, [])
    return (res, None) if p["final"] else tuple(res)


def kernel(x, norm_mix_w, w_in, hgrn_lower_bounds, hgrn_norm_w, ssd_conv_w, ssd_conv_b, ssd_dt_bias, ssd_a_log, ssd_d, ssd_norm_w, gla_gate_w, gla_gate_b, gla_norm_w, lru_conv_w, lru_conv_b, lru_wa, lru_ba, lru_wx, lru_bx, lru_lambda, w_branch, w_out, norm_ffn_w, w_ffn_in, w_ffn_out, norm_f_w):
    bsz, seq, _ = x.shape
    depth = w_in.shape[0]
    assert (bsz * seq) % NORM_ROWS == 0
    lb_all = jnp.cumsum(jax.nn.softmax(hgrn_lower_bounds.astype(F32), axis=0), axis=0)
    lb_all = lb_all - lb_all[0:1]
    wts = (_pack_in_proj(w_in, gla_gate_w), w_branch.astype(BF16), w_out.astype(BF16), w_ffn_in.astype(BF16),
           w_ffn_out.astype(BF16))
    h = x.reshape(bsz * seq, D_MODEL)
    xn = _pcall(_norm_kernel, "norm", bsz * seq, NORM_ROWS, [h, _row(norm_mix_w[0])], [True, False], (D_MODEL, BF16), [])
    for l in range(depth):
        p = dict(
            lb=lb_all[l], hgrn_norm_w=hgrn_norm_w[l],
            ssd_conv_w=ssd_conv_w[l], ssd_conv_b=ssd_conv_b[l], ssd_dt_bias=ssd_dt_bias[l], ssd_a_log=ssd_a_log[l],
            ssd_d=ssd_d[l], ssd_norm_w=ssd_norm_w[l], gla_gate_b=gla_gate_b[l],
            gla_norm_w=gla_norm_w[l], lru_conv_w=lru_conv_w[l], lru_conv_b=lru_conv_b[l], lru_wa=lru_wa[l],
            lru_ba=lru_ba[l], lru_wx=lru_wx[l], lru_bx=lru_bx[l], lru_lambda=lru_lambda[l],
            norm_ffn_w=norm_ffn_w[l],
            next_norm_w=(norm_f_w if l == depth - 1 else norm_mix_w[l + 1]), final=(l == depth - 1))
        h, xn = _layer(h, xn, seq, l, wts, p)
    return h.reshape(bsz, seq, D_MODEL)
```

```python
import functools
import math

import jax
import jax.numpy as jnp
import numpy as np
from jax import lax
from jax.experimental import pallas as pl
from jax.experimental.pallas import tpu as pltpu

F32 = jnp.float32
BF16 = jnp.bfloat16

D_MODEL = 1024
BRANCH_WIDTH = 512
N_BRANCHES = 4
CONV_WIDTH = 4
NORM_EPS = 1e-6
HGRN_HEADS = 4
SSD_HEADS = 8
SSD_HEADDIM = 64
SSD_GROUPS = 2
SSD_STATE = 64
SSD_CONV_DIM = 768
GLA_HEADS = 4
GLA_KDIM = 256
GLA_HEAD_K = 64
GLA_GATE_RANK = 16
GLA_GATE_NORMALIZER = 16.0
LRU_BLOCKS = 8
LRU_BLOCK = 64
LRU_C = 8.0
D_FF = 2816

LANES = 128
SUBLANES = 8
T_BLK = 256
BLOCKS_PER_STEP = 2
T_STEP = BLOCKS_PER_STEP * T_BLK
NORM_ROWS = 2048
HEAD_V = 128
VMEM_LIMIT = 48 * 1024 * 1024
F32_FLOOR = 1e-37
SSD_PROJ = BRANCH_WIDTH + SSD_CONV_DIM + LANES
GLA_PROJ = 3 * GLA_KDIM + 2 * BRANCH_WIDTH


def _dot(a, b):
    return jnp.dot(a, b, preferred_element_type=F32)


def _dot_nt(a, b):
    return lax.dot_general(a, b, (((1,), (1,)), ((), ())), preferred_element_type=F32)


def _dot_tn(a, b):
    return lax.dot_general(a, b, (((0,), (0,)), ((), ())), preferred_element_type=F32)


def _split3(x):
    hi = x.astype(BF16)
    r1 = x - hi.astype(F32)
    mid = r1.astype(BF16)
    lo = (r1 - mid.astype(F32)).astype(BF16)
    return hi, mid, lo


def _sel_dot(mat, x):
    hi, mid, lo = _split3(x)
    return (_dot(mat, lo) + _dot(mat, mid)) + _dot(mat, hi)


def _dot_sel(x, mat):
    hi, mid, lo = _split3(x)
    return (_dot(lo, mat) + _dot(mid, mat)) + _dot(hi, mat)


def _sigmoid(x):
    return jax.nn.sigmoid(x)


def _silu(x):
    return x * _sigmoid(x)


def _softplus(x):
    return jnp.maximum(x, 0.0) + jnp.log1p(jnp.exp(-jnp.abs(x)))


def _log_sigmoid(x):
    return jnp.minimum(x, 0.0) - jnp.log1p(jnp.exp(-jnp.abs(x)))


def _rmsnorm(x, w):
    return x * lax.rsqrt(jnp.mean(x * x, axis=-1, keepdims=True) + NORM_EPS) * w


GLA_SMALL = (1, 2, 4)
GLA_MID = (8, 16, 32, 64)
GLA_PARTS = 2 + len(GLA_SMALL) + len(GLA_MID) + 1
LOG2E = 1.4426950408889634
GLA_SPAN_LIMIT = 48.0
HGRN_BASE = 32
GLA_BASE = 128


def _gla_part_ids():
    row = np.arange(LANES)[:, None]
    col = np.arange(LANES)[None, :]
    x = row ^ col
    ids = np.where(x == 0, 0, np.floor(np.log2(np.maximum(x, 1))).astype(np.int64) + 1)
    return jnp.asarray(np.where(col <= row, ids, -1), jnp.int32)


def _gla_decay_matrix(t):
    u = np.arange(t)[None, :]
    r = np.arange(t)[:, None]
    blocks = [u <= r]
    for s in GLA_SMALL:
        mid = (r // (2 * s)) * (2 * s) + s - 1
        blocks.append(((u > mid) & (u <= r)) | ((u > r) & (u <= mid)))
    m = np.concatenate(blocks, axis=0)
    return jnp.asarray(np.concatenate([m, m], axis=1), BF16)


def _store_decay_pieces(rhs_ref, logf):
    t = logf.shape[0]
    lf2 = logf * LOG2E
    hi = lf2.astype(BF16)
    rhs_ref[0:t, :] = hi
    rhs_ref[t:2 * t, :] = (lf2 - hi.astype(F32)).astype(BF16)


def _gla_chunk(q, k, logf, v_ref, st_ref, ql_ref, kl_ref, rhs_ref, dm_ref, ids_ref, hpt, base, emit):
    t, nk = q.shape
    assert t == 2 * LANES
    ntile = nk // LANES
    dk = LANES // hpt
    levels = GLA_SMALL + GLA_MID

    _store_decay_pieces(rhs_ref, logf)
    bcum = _dot(dm_ref[0:t, :], rhs_ref[...])
    blast = bcum[t - 1:t, :]
    qb = q.astype(BF16)
    kb = k.astype(BF16)
    ql_ref[0] = qb * jnp.exp2(bcum).astype(BF16)
    kl_ref[0] = kb * jnp.exp2(blast - bcum).astype(BF16)

    def level_part(part, s):
        bc3 = bcum.reshape(t // (2 * s), 2 * s, nk)
        ref = bc3[:, s - 1:s, :]
        e = jnp.exp2(jnp.concatenate([ref - bc3[:, 0:s, :], bc3[:, s:2 * s, :] - ref], axis=1))
        e = e.reshape(t, nk).astype(BF16)
        ql_ref[part] = qb * e
        kl_ref[part] = kb * e

    part_of = {}
    part = 1
    for s in levels + (LANES,):
        if s >= base:
            level_part(part, s)
            part_of[s] = part
            part += 1
    free = part
    p_top = part_of[LANES]

    bcb = bcum.reshape(t // base, base, nk)
    span = bcb - bcb[:, base // 2 - 1:base // 2, :]
    fits = jnp.max(jnp.abs(span)) <= GLA_SPAN_LIMIT

    ids = ids_ref[...]
    lane = lax.broadcasted_iota(jnp.int32, (1, LANES), 1)

    def own(x, sub):
        return x if hpt == 1 else jnp.where((lane // dk) == sub, x, jnp.zeros_like(x))

    def stacked(x):
        return x if hpt == 1 else jnp.concatenate([own(x, sub) for sub in range(hpt)], axis=0)

    ids_st = ids if hpt == 1 else jnp.concatenate([ids] * hpt, axis=0)

    def finish(first, overrides):
        intra = []
        for tau in range(ntile):
            tl = slice(tau * LANES, (tau + 1) * LANES)
            diag = []
            for c in range(2):
                rows = slice(c * LANES, (c + 1) * LANES)
                sc = jnp.where(ids_st >= 0, _dot_nt(stacked(ql_ref[first, rows, tl]), kl_ref[first, rows, tl]), 0.0)
                for pid, p in overrides:
                    sc = jnp.where(ids_st == pid, _dot_nt(stacked(ql_ref[p, rows, tl]), kl_ref[p, rows, tl]), sc)
                diag.append(sc.astype(BF16))
            low = _dot_nt(stacked(ql_ref[p_top, LANES:t, tl]), kl_ref[p_top, 0:LANES, tl]).astype(BF16)
            for sub in range(hpt):
                hv = slice((tau * hpt + sub) * HEAD_V, (tau * hpt + sub + 1) * HEAD_V)
                hr = slice(sub * LANES, (sub + 1) * LANES)
                o_top = _dot(diag[0][hr], v_ref[0:LANES, hv])
                o_bot = _dot(jnp.concatenate([low[hr], diag[1][hr]], axis=1), v_ref[:, hv])
                intra.append(jnp.concatenate([o_top, o_bot], axis=0))
            yield
        for tau in range(ntile):
            tl = slice(tau * LANES, (tau + 1) * LANES)
            inter = _dot_nt(stacked(ql_ref[0, :, tl]), st_ref[tau].astype(BF16))
            for sub in range(hpt):
                h = tau * hpt + sub
                emit(h, intra[h] + inter[sub * t:(sub + 1) * t])
        for tau in range(ntile):
            tl = slice(tau * LANES, (tau + 1) * LANES)
            kd = kl_ref[0, :, tl]
            upd = _dot_tn(v_ref[:, tau * hpt * HEAD_V:(tau * hpt + 1) * HEAD_V], kd)
            for sub in range(1, hpt):
                h = tau * hpt + sub
                upd = jnp.where((lane // dk) == sub, _dot_tn(v_ref[:, h * HEAD_V:(h + 1) * HEAD_V], kd), upd)
            st_ref[tau] = st_ref[tau] * jnp.exp2(blast[:, tl]) + upd

    def part_id(s):
        return 1 + levels.index(s)

    upper = [(part_id(s), part_of[s]) for s in levels if s >= base]

    def one_reference():
        ql_ref[free] = qb * jnp.exp2(span).reshape(t, nk).astype(BF16)
        kl_ref[free] = kb * jnp.exp2(-span).reshape(t, nk).astype(BF16)
        yield from finish(free, upper)

    def safe_parts():
        d3 = _dot(dm_ref[t:(1 + len(GLA_SMALL)) * t, :], rhs_ref[...])
        lower = []
        part = free
        for i, s in enumerate(GLA_SMALL):
            if s < base:
                e = jnp.exp2(d3[i * t:(i + 1) * t]).astype(BF16)
                ql_ref[part] = qb * e
                kl_ref[part] = kb * e
                lower.append((part_id(s), part))
                part += 1
        for s in GLA_MID:
            if s < base:
                level_part(part, s)
                lower.append((part_id(s), part))
                part += 1
        ql_ref[part] = qb
        kl_ref[part] = kb
        yield from finish(part, lower + upper)

    return fits, one_reference, safe_parts


def _interleave(chains):
    live = list(chains)
    while live:
        for c in list(live):
            try:
                next(c)
            except StopIteration:
                live.remove(c)


def _gla_finish(chunks):
    fits = functools.reduce(jnp.logical_and, [c[0] for c in chunks])

    @pl.when(fits)
    def _():
        _interleave([c[1]() for c in chunks])

    @pl.when(jnp.logical_not(fits))
    def _():
        _interleave([c[2]() for c in chunks])


def _causal_conv(buf_ref, x, w_ref, b_ref, base):
    t = x.shape[0]
    buf_ref[SUBLANES + base:SUBLANES + base + t, :] = x
    acc = b_ref[...] + w_ref[CONV_WIDTH - 1:CONV_WIDTH, :] * x
    for s in range(1, CONV_WIDTH):
        acc = acc + w_ref[CONV_WIDTH - 1 - s:CONV_WIDTH - s, :] * buf_ref[pl.ds(SUBLANES + base - s, t), :]
    return acc


def _step_start(bps, *carry_refs):
    @pl.when((pl.program_id(0) * BLOCKS_PER_STEP) % bps == 0)
    def _():
        for ref in carry_refs:
            ref[...] = jnp.zeros_like(ref)


def _conv_carry(buf_ref):
    buf_ref[0:SUBLANES, :] = buf_ref[T_STEP:T_STEP + SUBLANES, :]


def _hgrn_kernel(bps, xn_ref, w_ref, lb_ref, oml_ref, hnw_ref, dm_ref, ids_ref,
                 o_ref, st_ref, *scratch):
    _step_start(bps, st_ref)
    w = BRANCH_WIDTH
    staged = []
    for blk in range(BLOCKS_PER_STEP):
        rows = slice(blk * T_BLK, (blk + 1) * T_BLK)
        ql_ref, kl_ref, rhs_ref, v_ref = scratch[4 * blk:4 * blk + 4]
        proj = _dot(xn_ref[rows, :], w_ref[...])
        fp = proj[:, w:2 * w]
        u = jnp.exp(-jnp.abs(fp))
        r = 1.0 / (1.0 + u)
        pos = fp >= 0.0
        sig = jnp.where(pos, r, u * r)
        key = oml_ref[...] * jnp.where(pos, u * r, r)
        logf = jnp.log(jnp.maximum(lb_ref[...] + oml_ref[...] * sig, F32_FLOOR))
        qf = _silu(proj[:, 0:w])
        v_ref[...] = proj[:, 2 * w:3 * w].astype(BF16)
        staged.append((qf, key, logf, _silu(proj[:, 3 * w:4 * w])))
    chunks = []
    for blk in range(BLOCKS_PER_STEP):
        rows = slice(blk * T_BLK, (blk + 1) * T_BLK)
        ql_ref, kl_ref, rhs_ref, v_ref = scratch[4 * blk:4 * blk + 4]
        qf, key, logf, gs = staged[blk]

        def emit(h, o, rows=rows, gs=gs):
            cols = slice(h * HEAD_V, (h + 1) * HEAD_V)
            o_ref[rows, cols] = (_rmsnorm(o, hnw_ref[...]) * gs[:, cols]).astype(o_ref.dtype)

        chunks.append(_gla_chunk(qf, key, logf, v_ref, st_ref, ql_ref, kl_ref, rhs_ref, dm_ref, ids_ref, 1,
                                 HGRN_BASE, emit))
    _gla_finish(chunks)


def _gla_kernel(bps, xn_ref, w_ref, gb_ref, gnw_ref, dm_ref, ids_ref, o_ref, st_ref, *scratch):
    _step_start(bps, st_ref)
    kd, w = GLA_KDIM, BRANCH_WIDTH
    chunks = []
    for blk in range(BLOCKS_PER_STEP):
        rows = slice(blk * T_BLK, (blk + 1) * T_BLK)
        ql_ref, kl_ref, rhs_ref, v_ref = scratch[4 * blk:4 * blk + 4]
        proj = _dot(xn_ref[rows, :], w_ref[:, 0:GLA_PROJ])
        logf = _log_sigmoid(proj[:, 0:kd] + gb_ref[...]) * (1.0 / GLA_GATE_NORMALIZER)
        proj = proj[:, kd:]
        q = proj[:, 0:kd] * (GLA_HEAD_K ** -0.5)
        k = proj[:, kd:2 * kd]
        v_ref[...] = proj[:, 2 * kd:2 * kd + w].astype(BF16)
        gs = _silu(proj[:, 2 * kd + w:2 * kd + 2 * w])

        def emit(h, o, rows=rows, gs=gs):
            cols = slice(h * HEAD_V, (h + 1) * HEAD_V)
            o_ref[rows, cols] = (_rmsnorm(o, gnw_ref[...]) * gs[:, cols]).astype(o_ref.dtype)

        chunks.append(_gla_chunk(q, k, logf, v_ref, st_ref, ql_ref, kl_ref, rhs_ref, dm_ref, ids_ref, 2,
                                 GLA_BASE, emit))
    _gla_finish(chunks)


def _ssd_kernel(bps, xn_ref, w_ref, cw_ref, cb_ref, dtb_ref, aneg_ref, dsk_ref, snw_ref, tril_ref,
                exp_ref, o_ref, buf_ref, st_ref, *y_refs):
    _step_start(bps, st_ref, buf_ref)
    _interleave([_ssd_block(blk, xn_ref, w_ref, cw_ref, cb_ref, dtb_ref, aneg_ref, dsk_ref, snw_ref, tril_ref,
                            exp_ref, o_ref, buf_ref, st_ref, y_refs[blk]) for blk in range(BLOCKS_PER_STEP)])
    _conv_carry(buf_ref)


def _ssd_block(blk, xn_ref, w_ref, cw_ref, cb_ref, dtb_ref, aneg_ref, dsk_ref, snw_ref, tril_ref,
               exp_ref, o_ref, buf_ref, st_ref, y_ref):
    t = T_BLK
    rows = slice(blk * t, (blk + 1) * t)
    w = BRANCH_WIDTH
    gs = SSD_GROUPS * SSD_STATE
    proj = _dot(xn_ref[rows, :], w_ref[:, 0:SSD_PROJ])
    z = proj[:, 0:w]
    xbc = _silu(_causal_conv(buf_ref, proj[:, w:w + SSD_CONV_DIM], cw_ref, cb_ref, blk * t))
    xs = xbc[:, 0:w]
    bm = xbc[:, w:w + gs].astype(BF16)
    cm = xbc[:, w + gs:w + 2 * gs]
    yield
    dt = _softplus(proj[:, w + SSD_CONV_DIM:] + dtb_ref[...])
    acs = _sel_dot(tril_ref[...], dt * aneg_ref[...])
    dt_e = _dot_sel(dt, exp_ref[...])
    acs_e = _dot_sel(acs, exp_ref[...])
    alast_e = acs_e[t - 1:t, :]
    xdt = xs * dt_e
    xdt_bf = xdt.astype(BF16)
    yield

    row = lax.broadcasted_iota(jnp.int32, (t, t), 0)
    col = lax.broadcasted_iota(jnp.int32, (t, t), 1)
    causal = col <= row
    lane = lax.broadcasted_iota(jnp.int32, (1, LANES), 1)
    acs_t = acs.T
    gmats = []
    for g in range(SSD_GROUPS):
        cg = jnp.where((lane // SSD_STATE) == g, cm, 0.0).astype(BF16)
        gmats.append(_dot_nt(cg, bm))
    hpg = SSD_HEADS // SSD_GROUPS
    y_diag = []
    for pair in range(SSD_HEADS // 2):
        tl = slice(pair * LANES, (pair + 1) * LANES)
        ys = []
        for sub in range(2):
            h = 2 * pair + sub
            seg = acs[:, h:h + 1] - acs_t[h:h + 1, :]
            m = jnp.where(causal, gmats[h // hpg] * jnp.exp(seg), 0.0).astype(BF16)
            ys.append(_dot(m, xdt_bf[:, tl]))
        y_diag.append(jnp.where(lane < SSD_HEADDIM, ys[0], ys[1]))
        yield
    for pair in range(SSD_HEADS // 2):
        tl = slice(pair * LANES, (pair + 1) * LANES)
        y_off = _dot(cm.astype(BF16), st_ref[:, tl].astype(BF16)) * jnp.exp(acs_e[:, tl])
        y = y_diag[pair] + y_off + xs[:, tl] * dsk_ref[:, tl]
        y_ref[:, tl] = y * _silu(z[:, tl])
    xdec = (xdt * jnp.exp(alast_e - acs_e)).astype(BF16)
    upd = _dot_tn(bm, xdec)
    srow = lax.broadcasted_iota(jnp.int32, (gs, w), 0) // SSD_STATE
    scol = lax.broadcasted_iota(jnp.int32, (gs, w), 1) // (hpg * SSD_HEADDIM)
    st_ref[...] = st_ref[...] * jnp.exp(alast_e) + jnp.where(srow == scol, upd, 0.0)
    gw = w // SSD_GROUPS
    for g in range(SSD_GROUPS):
        yg = y_ref[:, g * gw:(g + 1) * gw]
        o_ref[rows, g * gw:(g + 1) * gw] = (
            yg * lax.rsqrt(jnp.mean(yg * yg, axis=-1, keepdims=True) + NORM_EPS) * snw_ref[:, g * gw:(g + 1) * gw]
        ).astype(o_ref.dtype)


def _sqrt_one_minus_exp2x(x):
    th = jnp.tanh(x)
    y = (-2.0 * th) / (1.0 - th)
    return y * lax.rsqrt(jnp.maximum(y, 1e-30))


def _doubling_scan(a, b, pos, n, axis):
    off = 1
    while off < n:
        keep = pos >= off
        a_sh = jnp.where(keep, pltpu.roll(a, off, axis), 1.0)
        b_sh = jnp.where(keep, pltpu.roll(b, off, axis), 0.0)
        b = a * b_sh + b
        a = a * a_sh
        off *= 2
    return a, b


def _scan_rows(a, b, hc_ref, ab_ref, cg_ref):
    t, n = a.shape
    g = t // SUBLANES
    nt = n // LANES
    a8, b8 = _doubling_scan(a.reshape(g, SUBLANES, n), b.reshape(g, SUBLANES, n),
                            lax.broadcasted_iota(jnp.int32, (1, SUBLANES, 1), 1), SUBLANES, 1)
    a8 = a8.reshape(t, n)
    b8 = b8.reshape(t, n)
    for i in range(nt):
        ab_ref[i] = a8[:, i * LANES:(i + 1) * LANES]
        ab_ref[nt + i] = b8[:, i * LANES:(i + 1) * LANES]
    last = pl.ds(SUBLANES - 1, g, stride=SUBLANES)
    gidx = lax.broadcasted_iota(jnp.int32, (g, 1), 0)
    ag = jnp.concatenate([ab_ref[i, last, :] for i in range(nt)], axis=1)
    bg = jnp.concatenate([ab_ref[nt + i, last, :] for i in range(nt)], axis=1)
    ag, bg = _doubling_scan(ag, bg, gidx, g, 0)
    state = bg + ag * hc_ref[...]
    cg_ref[...] = jnp.where(gidx >= 1, pltpu.roll(state, 1, 0), hc_ref[...])
    hc_ref[...] = state[g - 1:g, :]
    out = []
    for i in range(nt):
        cg = cg_ref[:, i * LANES:(i + 1) * LANES].reshape(g, 1, LANES)
        hi = ab_ref[nt + i].reshape(g, SUBLANES, LANES) + ab_ref[i].reshape(g, SUBLANES, LANES) * cg
        out.append(hi.reshape(t, LANES))
    return jnp.concatenate(out, axis=1)


def _lru_kernel(bps, xn_ref, w_ref, cw_ref, cb_ref, wg_ref, bg_ref, lam_ref, o_ref, buf_ref, hc_ref,
                *scratch):
    _step_start(bps, hc_ref, buf_ref)
    for blk in range(BLOCKS_PER_STEP):
        _lru_block(blk, xn_ref, w_ref, cw_ref, cb_ref, wg_ref, bg_ref, lam_ref, o_ref, buf_ref, hc_ref,
                   *scratch[2 * blk:2 * blk + 2])
    _conv_carry(buf_ref)


def _lru_block(blk, xn_ref, w_ref, cw_ref, cb_ref, wg_ref, bg_ref, lam_ref, o_ref, buf_ref, hc_ref,
               ab_ref, cg_ref):
    t = T_BLK
    rows = slice(blk * t, (blk + 1) * t)
    w = BRANCH_WIDTH
    proj = _dot(xn_ref[rows, :], w_ref[...])
    u = _causal_conv(buf_ref, proj[:, 0:w], cw_ref, cb_ref, blk * t)
    rg = _dot(u.astype(BF16), wg_ref[...]) + bg_ref[...]
    r = _sigmoid(rg[:, 0:w])
    ig = _sigmoid(rg[:, w:2 * w])
    log_a = (-LRU_C) * r * _softplus(-lam_ref[...])
    a = jnp.exp(log_a)
    b = _sqrt_one_minus_exp2x(log_a) * (ig * u)
    hseq = _scan_rows(a, b, hc_ref, ab_ref, cg_ref)
    gate = proj[:, w:2 * w]
    gelu = 0.5 * gate * (1.0 + jnp.tanh(math.sqrt(2.0 / math.pi) * (gate + 0.044715 * gate * gate * gate)))
    o_ref[rows, :] = (hseq * gelu).astype(o_ref.dtype)


def _merge_kernel(h_ref, xn_ref, ya_ref, yb_ref, yc_ref, yd_ref, wmg_ref, wbr_ref, wout_ref, o_ref):
    xn = xn_ref[...]
    merged = None
    for n, y_ref in enumerate((ya_ref, yb_ref, yc_ref, yd_ref)):
        gate = _sigmoid(_dot(xn, wmg_ref[:, n * D_MODEL:(n + 1) * D_MODEL]))
        term = gate * _dot(y_ref[...], wbr_ref[n])
        merged = term if merged is None else merged + term
    o_ref[...] = h_ref[...] + _dot(merged.astype(BF16), wout_ref[...])


def _ffn_kernel(final, h_ref, nw_ref, win_ref, wout_ref, nnw_ref, o_ref, *xn_out):
    hres = h_ref[...]
    xn = _rmsnorm(hres, nw_ref[...]).astype(BF16)
    gu = _dot(xn, win_ref[...])
    act = (_silu(gu[:, 0:D_FF]) * gu[:, D_FF:2 * D_FF]).astype(BF16)
    out = hres + _dot(act, wout_ref[...])
    if final:
        o_ref[...] = _rmsnorm(out, nnw_ref[...])
    else:
        o_ref[...] = out
        xn_out[0][...] = _rmsnorm(out, nnw_ref[...]).astype(BF16)


def _norm_kernel(x_ref, nw_ref, o_ref):
    o_ref[...] = _rmsnorm(x_ref[...], nw_ref[...]).astype(o_ref.dtype)


def _const_spec(shape):
    nd = len(shape)
    return pl.BlockSpec(shape, lambda i, _nd=nd: (0,) * _nd, pipeline_mode=pl.Buffered(1))


def _row_spec(rows, width):
    return pl.BlockSpec((rows, width), lambda i: (i, 0))


class _LayerWindow:
    def __init__(self, array, layer, width=None, col=0):
        self.array, self.layer, self.width, self.col = array, layer, width or array.shape[-1], col

    def spec(self):
        shape = self.array.shape
        mid = (0,) * (len(shape) - 2)
        layer, col = self.layer, self.col
        return pl.BlockSpec((None,) + shape[1:-1] + (self.width,), lambda i: (layer,) + mid + (col,),
                            pipeline_mode=pl.Buffered(1))


def _pcall(kernel, name, n_tokens, rows, operands, row_flags, outs, scratch):
    in_specs = [a.spec() if isinstance(a, _LayerWindow) else
                (_row_spec(rows, a.shape[1]) if is_row else _const_spec(a.shape))
                for a, is_row in zip(operands, row_flags)]
    operands = [a.array if isinstance(a, _LayerWindow) else a for a in operands]
    many = isinstance(outs, list)
    out_specs = [_row_spec(rows, wd) for wd, _ in (outs if many else [outs])]
    out_shape = [jax.ShapeDtypeStruct((n_tokens, wd), dt) for wd, dt in (outs if many else [outs])]
    return pl.pallas_call(
        kernel,
        name=name,
        grid=(n_tokens // rows,),
        in_specs=in_specs,
        out_specs=out_specs if many else out_specs[0],
        out_shape=out_shape if many else out_shape[0],
        scratch_shapes=scratch,
        compiler_params=pltpu.CompilerParams(dimension_semantics=("arbitrary",), vmem_limit_bytes=VMEM_LIMIT),
    )(*operands)


def _tril_const(t):
    return jnp.asarray(np.tril(np.ones((t, t), np.float32)), BF16)


def _gla_scratch(nk):
    ntile = nk // LANES
    per_block = [
        pltpu.VMEM((GLA_PARTS, T_BLK, nk), BF16),
        pltpu.VMEM((GLA_PARTS, T_BLK, nk), BF16),
        pltpu.VMEM((2 * T_BLK, nk), BF16),
        pltpu.VMEM((T_BLK, BRANCH_WIDTH), BF16),
    ]
    return [pltpu.VMEM((ntile, HEAD_V, LANES), F32)] + per_block * BLOCKS_PER_STEP


def _row(x):
    return x.reshape(1, -1).astype(F32)


PACK_MERGE = (4096, 0)
PACK_HGRN = (2048, 2)
PACK_SSD = (2048, 3)
PACK_GLA = (2048, 4)
PACK_LRU = (1024, 10)


PACK_COLS = 11 * 1024
PACK_ROWS = 128
IN_COLS = [0] + np.cumsum([512, 512, 512, 512, 512, 768, 8, 256, 256, 512, 512, 16, 512, 512, 4096]).tolist()


def _pack_kernel(w_ref, wg_ref, o_ref):
    c = IN_COLS

    def put(dst, lo, hi):
        o_ref[:, dst:dst + hi - lo] = w_ref[:, lo:hi].astype(BF16)
        return dst + hi - lo

    def pad(lo, hi):
        o_ref[:, lo:hi] = jnp.zeros((o_ref.shape[0], hi - lo), BF16)

    put(PACK_MERGE[0] * PACK_MERGE[1], c[14], c[15])
    put(PACK_HGRN[0] * PACK_HGRN[1], c[0], c[4])
    start = PACK_SSD[0] * PACK_SSD[1]
    pad(put(start, c[4], c[7]), start + PACK_SSD[0])
    start = PACK_GLA[0] * PACK_GLA[1]
    o_ref[:, start:start + GLA_KDIM] = wg_ref[...].astype(BF16)
    pad(put(start + GLA_KDIM, c[7], c[11]), start + PACK_GLA[0])
    put(PACK_LRU[0] * PACK_LRU[1], c[12], c[14])


def _pack_in_proj(w_in, gla_gate_w):
    depth, rows, cols = w_in.shape
    w_gate = jnp.einsum("lkr,lrn->lkn", w_in[:, :, IN_COLS[11]:IN_COLS[12]].astype(F32), gla_gate_w.astype(F32),
                        precision=lax.Precision.HIGHEST)
    w_in = w_in.astype(BF16)
    return pl.pallas_call(
        _pack_kernel,
        name="pack",
        grid=(depth, rows // PACK_ROWS),
        in_specs=[pl.BlockSpec((None, PACK_ROWS, cols), lambda l, i: (l, i, 0)),
                  pl.BlockSpec((None, PACK_ROWS, GLA_KDIM), lambda l, i: (l, i, 0))],
        out_specs=pl.BlockSpec((None, PACK_ROWS, PACK_COLS), lambda l, i: (l, i, 0)),
        out_shape=jax.ShapeDtypeStruct((depth, rows, PACK_COLS), BF16),
        compiler_params=pltpu.CompilerParams(dimension_semantics=("arbitrary", "arbitrary"),
                                             vmem_limit_bytes=VMEM_LIMIT),
    )(w_in, w_gate)


def _layer(h, xn, seq, l, wts, p):
    n_tokens = h.shape[0]
    bps = seq // T_BLK
    assert seq % T_STEP == 0 and n_tokens % seq == 0
    tril = _tril_const(T_BLK)
    dmat = _gla_decay_matrix(T_BLK)
    part_ids = _gla_part_ids()
    w_all, w_br, w_o, w_fi, w_fo = wts
    win = lambda pack: _LayerWindow(w_all, l, *pack)

    lb = p["lb"].astype(F32)
    ya = _pcall(
        functools.partial(_hgrn_kernel, bps), "hgrn2", n_tokens, T_STEP,
        [xn, win(PACK_HGRN), _row(lb), _row(1.0 - lb), _row(p["hgrn_norm_w"]), dmat, part_ids],
        [True] + [False] * 6, (BRANCH_WIDTH, BF16), _gla_scratch(BRANCH_WIDTH))

    pad8 = lambda v: jnp.concatenate([v.astype(F32), jnp.zeros((LANES - SSD_HEADS,), F32)]).reshape(1, LANES)
    expand = jnp.asarray(np.arange(LANES)[:, None] == (np.arange(BRANCH_WIDTH)[None, :] // SSD_HEADDIM), BF16)
    yb = _pcall(
        functools.partial(_ssd_kernel, bps), "ssd", n_tokens, T_STEP,
        [xn, win(PACK_SSD), p["ssd_conv_w"].astype(F32), _row(p["ssd_conv_b"]), pad8(p["ssd_dt_bias"]),
         pad8(-jnp.exp(p["ssd_a_log"].astype(F32))), _row(jnp.repeat(p["ssd_d"], SSD_HEADDIM)), _row(p["ssd_norm_w"]),
         tril, expand],
        [True] + [False] * 9, (BRANCH_WIDTH, BF16),
        [pltpu.VMEM((T_STEP + SUBLANES, SSD_CONV_DIM), F32), pltpu.VMEM((SSD_GROUPS * SSD_STATE, BRANCH_WIDTH), F32)]
        + [pltpu.VMEM((T_BLK, BRANCH_WIDTH), F32)] * BLOCKS_PER_STEP)

    yc = _pcall(
        functools.partial(_gla_kernel, bps), "gla", n_tokens, T_STEP,
        [xn, win(PACK_GLA), _row(p["gla_gate_b"]), _row(p["gla_norm_w"]), dmat, part_ids],
        [True] + [False] * 5, (BRANCH_WIDTH, BF16), _gla_scratch(GLA_KDIM))

    eye = jnp.eye(LRU_BLOCKS, dtype=F32)
    bdiag = lambda wt: jnp.einsum("kde,kl->kdle", wt.astype(F32), eye).reshape(BRANCH_WIDTH, BRANCH_WIDTH)
    w_gates = jnp.concatenate([bdiag(p["lru_wa"]), bdiag(p["lru_wx"])], axis=1).astype(BF16)
    b_gates = _row(jnp.concatenate([p["lru_ba"], p["lru_bx"]]))
    yd = _pcall(
        functools.partial(_lru_kernel, bps), "rglru", n_tokens, T_STEP,
        [xn, win(PACK_LRU), p["lru_conv_w"].astype(F32), _row(p["lru_conv_b"]), w_gates,
         b_gates, _row(p["lru_lambda"])],
        [True] + [False] * 6, (BRANCH_WIDTH, BF16),
        [pltpu.VMEM((T_STEP + SUBLANES, BRANCH_WIDTH), F32), pltpu.VMEM((1, BRANCH_WIDTH), F32)]
        + [pltpu.VMEM((2 * BRANCH_WIDTH // LANES, T_BLK, LANES), F32),
           pltpu.VMEM((T_BLK // SUBLANES, BRANCH_WIDTH), F32)] * BLOCKS_PER_STEP)

    h = _pcall(
        _merge_kernel, "merge", n_tokens, T_STEP,
        [h, xn, ya, yb, yc, yd, win(PACK_MERGE), _LayerWindow(w_br, l), _LayerWindow(w_o, l)],
        [True, True, True, True, True, True, False, False, False], (D_MODEL, F32), [])

    outs = (D_MODEL, F32) if p["final"] else [(D_MODEL, F32), (D_MODEL, BF16)]
    res = _pcall(
        functools.partial(_ffn_kernel, p["final"]), "swiglu", n_tokens, T_STEP,
        [h, _row(p["norm_ffn_w"]), _LayerWindow(w_fi, l), _LayerWindow(w_fo, l), _row(p["next_norm_w"])],
        [True, False, False, False, False], outs, [])
    return (res, None) if p["final"] else tuple(res)


def kernel(x, norm_mix_w, w_in, hgrn_lower_bounds, hgrn_norm_w, ssd_conv_w, ssd_conv_b, ssd_dt_bias, ssd_a_log, ssd_d, ssd_norm_w, gla_gate_w, gla_gate_b, gla_norm_w, lru_conv_w, lru_conv_b, lru_wa, lru_ba, lru_wx, lru_bx, lru_lambda, w_branch, w_out, norm_ffn_w, w_ffn_in, w_ffn_out, norm_f_w):
    bsz, seq, _ = x.shape
    depth = w_in.shape[0]
    assert (bsz * seq) % NORM_ROWS == 0
    lb_all = jnp.cumsum(jax.nn.softmax(hgrn_lower_bounds.astype(F32), axis=0), axis=0)
    lb_all = lb_all - lb_all[0:1]
    wts = (_pack_in_proj(w_in, gla_gate_w), w_branch.astype(BF16), w_out.astype(BF16), w_ffn_in.astype(BF16),
           w_ffn_out.astype(BF16))
    h = x.reshape(bsz * seq, D_MODEL)
    xn = _pcall(_norm_kernel, "norm", bsz * seq, NORM_ROWS, [h, _row(norm_mix_w[0])], [True, False], (D_MODEL, BF16), [])
    for l in range(depth):
        p = dict(
            lb=lb_all[l], hgrn_norm_w=hgrn_norm_w[l],
            ssd_conv_w=ssd_conv_w[l], ssd_conv_b=ssd_conv_b[l], ssd_dt_bias=ssd_dt_bias[l], ssd_a_log=ssd_a_log[l],
            ssd_d=ssd_d[l], ssd_norm_w=ssd_norm_w[l], gla_gate_b=gla_gate_b[l],
            gla_norm_w=gla_norm_w[l], lru_conv_w=lru_conv_w[l], lru_conv_b=lru_conv_b[l], lru_wa=lru_wa[l],
            lru_ba=lru_ba[l], lru_wx=lru_wx[l], lru_bx=lru_bx[l], lru_lambda=lru_lambda[l],
            norm_ffn_w=norm_ffn_w[l],
            next_norm_w=(norm_f_w if l == depth - 1 else norm_mix_w[l + 1]), final=(l == depth - 1))
        h, xn = _layer(h, xn, seq, l, wts, p)
    return h.reshape(bsz, seq, D_MODEL)
```

```python
import functools
import math

import jax
import jax.numpy as jnp
import numpy as np
from jax import lax
from jax.experimental import pallas as pl
from jax.experimental.pallas import tpu as pltpu

F32 = jnp.float32
BF16 = jnp.bfloat16

D_MODEL = 1024
BRANCH_WIDTH = 512
N_BRANCHES = 4
CONV_WIDTH = 4
NORM_EPS = 1e-6
HGRN_HEADS = 4
SSD_HEADS = 8
SSD_HEADDIM = 64
SSD_GROUPS = 2
SSD_STATE = 64
SSD_CONV_DIM = 768
GLA_HEADS = 4
GLA_KDIM = 256
GLA_HEAD_K = 64
GLA_GATE_RANK = 16
GLA_GATE_NORMALIZER = 16.0
LRU_BLOCKS = 8
LRU_BLOCK = 64
LRU_C = 8.0
D_FF = 2816

LANES = 128
SUBLANES = 8
T_BLK = 256
T_STEP = 2 * T_BLK
HGRN_CHUNKS, GLA_CHUNKS, SSD_CHUNKS, LRU_CHUNKS = 2, 4, 4, 4
NORM_ROWS = 2048
HEAD_V = 128
VMEM_LIMIT = 48 * 1024 * 1024
F32_FLOOR = 1e-37
SSD_PROJ = BRANCH_WIDTH + SSD_CONV_DIM + LANES
GLA_PROJ = 3 * GLA_KDIM + 2 * BRANCH_WIDTH


def _dot(a, b):
    return jnp.dot(a, b, preferred_element_type=F32)


def _dot_nt(a, b):
    return lax.dot_general(a, b, (((1,), (1,)), ((), ())), preferred_element_type=F32)


def _dot_tn(a, b):
    return lax.dot_general(a, b, (((0,), (0,)), ((), ())), preferred_element_type=F32)


def _split3(x):
    hi = x.astype(BF16)
    r1 = x - hi.astype(F32)
    mid = r1.astype(BF16)
    lo = (r1 - mid.astype(F32)).astype(BF16)
    return hi, mid, lo


def _sel_dot(mat, x):
    hi, mid, lo = _split3(x)
    return (_dot(mat, lo) + _dot(mat, mid)) + _dot(mat, hi)


def _dot_sel(x, mat):
    hi, mid, lo = _split3(x)
    return (_dot(lo, mat) + _dot(mid, mat)) + _dot(hi, mat)


def _sigmoid(x):
    return jax.nn.sigmoid(x)


def _silu(x):
    return x * _sigmoid(x)


def _softplus(x):
    return jnp.maximum(x, 0.0) + jnp.log1p(jnp.exp(-jnp.abs(x)))


def _log_sigmoid(x):
    return jnp.minimum(x, 0.0) - jnp.log1p(jnp.exp(-jnp.abs(x)))


def _rmsnorm(x, w):
    return x * lax.rsqrt(jnp.mean(x * x, axis=-1, keepdims=True) + NORM_EPS) * w


GLA_SMALL = (1, 2, 4)
GLA_MID = (8, 16, 32, 64)
GLA_PARTS = 2 + len(GLA_SMALL) + len(GLA_MID) + 1
LOG2E = 1.4426950408889634
GLA_SPAN_LIMIT = 48.0
HGRN_BASE = 32
GLA_BASE = 128


def _gla_part_ids():
    row = np.arange(LANES)[:, None]
    col = np.arange(LANES)[None, :]
    x = row ^ col
    ids = np.where(x == 0, 0, np.floor(np.log2(np.maximum(x, 1))).astype(np.int64) + 1)
    return jnp.asarray(np.where(col <= row, ids, -1), jnp.int32)


def _gla_decay_matrix(t):
    u = np.arange(t)[None, :]
    r = np.arange(t)[:, None]
    blocks = [u <= r]
    for s in GLA_SMALL:
        mid = (r // (2 * s)) * (2 * s) + s - 1
        blocks.append(((u > mid) & (u <= r)) | ((u > r) & (u <= mid)))
    m = np.concatenate(blocks, axis=0)
    return jnp.asarray(np.concatenate([m, m], axis=1), BF16)


def _store_decay_pieces(rhs_ref, logf):
    t = logf.shape[0]
    lf2 = logf * LOG2E
    hi = lf2.astype(BF16)
    rhs_ref[0:t, :] = hi
    rhs_ref[t:2 * t, :] = (lf2 - hi.astype(F32)).astype(BF16)


def _gla_chunk(q, k, logf, v_ref, st_ref, ql_ref, kl_ref, rhs_ref, dm_ref, ids_ref, hpt, base, emit):
    t, nk = q.shape
    assert t == 2 * LANES
    ntile = nk // LANES
    dk = LANES // hpt
    levels = GLA_SMALL + GLA_MID

    _store_decay_pieces(rhs_ref, logf)
    bcum = _dot(dm_ref[0:t, :], rhs_ref[...])
    blast = bcum[t - 1:t, :]
    qb = q.astype(BF16)
    kb = k.astype(BF16)
    ql_ref[0] = qb * jnp.exp2(bcum).astype(BF16)
    kl_ref[0] = kb * jnp.exp2(blast - bcum).astype(BF16)

    def level_part(part, s):
        bc3 = bcum.reshape(t // (2 * s), 2 * s, nk)
        ref = bc3[:, s - 1:s, :]
        e = jnp.exp2(jnp.concatenate([ref - bc3[:, 0:s, :], bc3[:, s:2 * s, :] - ref], axis=1))
        e = e.reshape(t, nk).astype(BF16)
        ql_ref[part] = qb * e
        kl_ref[part] = kb * e

    part_of = {}
    part = 1
    for s in levels + (LANES,):
        if s >= base:
            level_part(part, s)
            part_of[s] = part
            part += 1
    free = part
    p_top = part_of[LANES]

    bcb = bcum.reshape(t // base, base, nk)
    span = bcb - bcb[:, base // 2 - 1:base // 2, :]
    fits = jnp.max(jnp.abs(span)) <= GLA_SPAN_LIMIT

    ids = ids_ref[...]
    lane = lax.broadcasted_iota(jnp.int32, (1, LANES), 1)

    def own(x, sub):
        return x if hpt == 1 else jnp.where((lane // dk) == sub, x, jnp.zeros_like(x))

    def stacked(x):
        return x if hpt == 1 else jnp.concatenate([own(x, sub) for sub in range(hpt)], axis=0)

    ids_st = ids if hpt == 1 else jnp.concatenate([ids] * hpt, axis=0)

    def finish(first, overrides):
        intra = []
        for tau in range(ntile):
            tl = slice(tau * LANES, (tau + 1) * LANES)
            diag = []
            for c in range(2):
                rows = slice(c * LANES, (c + 1) * LANES)
                sc = jnp.where(ids_st >= 0, _dot_nt(stacked(ql_ref[first, rows, tl]), kl_ref[first, rows, tl]), 0.0)
                for pid, p in overrides:
                    sc = jnp.where(ids_st == pid, _dot_nt(stacked(ql_ref[p, rows, tl]), kl_ref[p, rows, tl]), sc)
                diag.append(sc.astype(BF16))
            low = _dot_nt(stacked(ql_ref[p_top, LANES:t, tl]), kl_ref[p_top, 0:LANES, tl]).astype(BF16)
            for sub in range(hpt):
                hv = slice((tau * hpt + sub) * HEAD_V, (tau * hpt + sub + 1) * HEAD_V)
                hr = slice(sub * LANES, (sub + 1) * LANES)
                o_top = _dot(diag[0][hr], v_ref[0:LANES, hv])
                o_bot = _dot(jnp.concatenate([low[hr], diag[1][hr]], axis=1), v_ref[:, hv])
                intra.append(jnp.concatenate([o_top, o_bot], axis=0))
            yield
        for tau in range(ntile):
            tl = slice(tau * LANES, (tau + 1) * LANES)
            inter = _dot_nt(stacked(ql_ref[0, :, tl]), st_ref[tau].astype(BF16))
            for sub in range(hpt):
                h = tau * hpt + sub
                emit(h, intra[h] + inter[sub * t:(sub + 1) * t])
        for tau in range(ntile):
            tl = slice(tau * LANES, (tau + 1) * LANES)
            kd = kl_ref[0, :, tl]
            upd = _dot_tn(v_ref[:, tau * hpt * HEAD_V:(tau * hpt + 1) * HEAD_V], kd)
            for sub in range(1, hpt):
                h = tau * hpt + sub
                upd = jnp.where((lane // dk) == sub, _dot_tn(v_ref[:, h * HEAD_V:(h + 1) * HEAD_V], kd), upd)
            st_ref[tau] = st_ref[tau] * jnp.exp2(blast[:, tl]) + upd

    def part_id(s):
        return 1 + levels.index(s)

    upper = [(part_id(s), part_of[s]) for s in levels if s >= base]

    def one_reference():
        ql_ref[free] = qb * jnp.exp2(span).reshape(t, nk).astype(BF16)
        kl_ref[free] = kb * jnp.exp2(-span).reshape(t, nk).astype(BF16)
        yield from finish(free, upper)

    def safe_parts():
        d3 = _dot(dm_ref[t:(1 + len(GLA_SMALL)) * t, :], rhs_ref[...])
        lower = []
        part = free
        for i, s in enumerate(GLA_SMALL):
            if s < base:
                e = jnp.exp2(d3[i * t:(i + 1) * t]).astype(BF16)
                ql_ref[part] = qb * e
                kl_ref[part] = kb * e
                lower.append((part_id(s), part))
                part += 1
        for s in GLA_MID:
            if s < base:
                level_part(part, s)
                lower.append((part_id(s), part))
                part += 1
        ql_ref[part] = qb
        kl_ref[part] = kb
        yield from finish(part, lower + upper)

    return fits, one_reference, safe_parts


def _interleave(chains):
    live = list(chains)
    while live:
        for c in list(live):
            try:
                next(c)
            except StopIteration:
                live.remove(c)


def _gla_finish(chunks):
    fits = functools.reduce(jnp.logical_and, [c[0] for c in chunks])

    @pl.when(fits)
    def _():
        _interleave([c[1]() for c in chunks])

    @pl.when(jnp.logical_not(fits))
    def _():
        _interleave([c[2]() for c in chunks])


def _causal_conv(buf_ref, x, w_ref, b_ref, base):
    t = x.shape[0]
    buf_ref[SUBLANES + base:SUBLANES + base + t, :] = x
    acc = b_ref[...] + w_ref[CONV_WIDTH - 1:CONV_WIDTH, :] * x
    for s in range(1, CONV_WIDTH):
        acc = acc + w_ref[CONV_WIDTH - 1 - s:CONV_WIDTH - s, :] * buf_ref[pl.ds(SUBLANES + base - s, t), :]
    return acc


def _step_start(bps, nb, *carry_refs):
    @pl.when((pl.program_id(0) * nb) % bps == 0)
    def _():
        for ref in carry_refs:
            ref[...] = jnp.zeros_like(ref)


def _conv_carry(buf_ref):
    rows = buf_ref.shape[0] - SUBLANES
    buf_ref[0:SUBLANES, :] = buf_ref[rows:rows + SUBLANES, :]


def _hgrn_kernel(bps, xn_ref, w_ref, lb_ref, oml_ref, hnw_ref, dm_ref, ids_ref,
                 o_ref, st_ref, *scratch):
    nb = xn_ref.shape[0] // T_BLK
    _step_start(bps, nb, st_ref)
    w = BRANCH_WIDTH
    staged = []
    for blk in range(nb):
        rows = slice(blk * T_BLK, (blk + 1) * T_BLK)
        ql_ref, kl_ref, rhs_ref, v_ref = scratch[4 * blk:4 * blk + 4]
        proj = _dot(xn_ref[rows, :], w_ref[...])
        fp = proj[:, w:2 * w]
        u = jnp.exp(-jnp.abs(fp))
        r = 1.0 / (1.0 + u)
        pos = fp >= 0.0
        sig = jnp.where(pos, r, u * r)
        key = oml_ref[...] * jnp.where(pos, u * r, r)
        logf = jnp.log(jnp.maximum(lb_ref[...] + oml_ref[...] * sig, F32_FLOOR))
        qf = _silu(proj[:, 0:w])
        v_ref[...] = proj[:, 2 * w:3 * w].astype(BF16)
        staged.append((qf, key, logf, _silu(proj[:, 3 * w:4 * w])))
    chunks = []
    for blk in range(nb):
        rows = slice(blk * T_BLK, (blk + 1) * T_BLK)
        ql_ref, kl_ref, rhs_ref, v_ref = scratch[4 * blk:4 * blk + 4]
        qf, key, logf, gs = staged[blk]

        def emit(h, o, rows=rows, gs=gs):
            cols = slice(h * HEAD_V, (h + 1) * HEAD_V)
            o_ref[rows, cols] = (_rmsnorm(o, hnw_ref[...]) * gs[:, cols]).astype(o_ref.dtype)

        chunks.append(_gla_chunk(qf, key, logf, v_ref, st_ref, ql_ref, kl_ref, rhs_ref, dm_ref, ids_ref, 1,
                                 HGRN_BASE, emit))
    _gla_finish(chunks)


def _gla_kernel(bps, xn_ref, w_ref, gb_ref, gnw_ref, dm_ref, ids_ref, o_ref, st_ref, *scratch):
    nb = xn_ref.shape[0] // T_BLK
    _step_start(bps, nb, st_ref)
    kd, w = GLA_KDIM, BRANCH_WIDTH
    chunks = []
    for blk in range(nb):
        rows = slice(blk * T_BLK, (blk + 1) * T_BLK)
        ql_ref, kl_ref, rhs_ref, v_ref = scratch[4 * blk:4 * blk + 4]
        proj = _dot(xn_ref[rows, :], w_ref[:, 0:GLA_PROJ])
        logf = _log_sigmoid(proj[:, 0:kd] + gb_ref[...]) * (1.0 / GLA_GATE_NORMALIZER)
        proj = proj[:, kd:]
        q = proj[:, 0:kd] * (GLA_HEAD_K ** -0.5)
        k = proj[:, kd:2 * kd]
        v_ref[...] = proj[:, 2 * kd:2 * kd + w].astype(BF16)
        gs = _silu(proj[:, 2 * kd + w:2 * kd + 2 * w])

        def emit(h, o, rows=rows, gs=gs):
            cols = slice(h * HEAD_V, (h + 1) * HEAD_V)
            o_ref[rows, cols] = (_rmsnorm(o, gnw_ref[...]) * gs[:, cols]).astype(o_ref.dtype)

        chunks.append(_gla_chunk(q, k, logf, v_ref, st_ref, ql_ref, kl_ref, rhs_ref, dm_ref, ids_ref, 2,
                                 GLA_BASE, emit))
    _gla_finish(chunks)


def _ssd_kernel(bps, xn_ref, w_ref, cw_ref, cb_ref, dtb_ref, aneg_ref, dsk_ref, snw_ref, tril_ref,
                exp_ref, o_ref, buf_ref, st_ref, *y_refs):
    nb = xn_ref.shape[0] // T_BLK
    _step_start(bps, nb, st_ref, buf_ref)
    for blk in range(nb):
        _ssd_block(blk, xn_ref, w_ref, cw_ref, cb_ref, dtb_ref, aneg_ref, dsk_ref, snw_ref, tril_ref,
                   exp_ref, o_ref, buf_ref, st_ref, y_refs[blk])
    _conv_carry(buf_ref)


def _ssd_block(blk, xn_ref, w_ref, cw_ref, cb_ref, dtb_ref, aneg_ref, dsk_ref, snw_ref, tril_ref,
               exp_ref, o_ref, buf_ref, st_ref, y_ref):
    t = T_BLK
    rows = slice(blk * t, (blk + 1) * t)
    w = BRANCH_WIDTH
    gs = SSD_GROUPS * SSD_STATE
    proj = _dot(xn_ref[rows, :], w_ref[:, 0:SSD_PROJ])
    z = proj[:, 0:w]
    xbc = _silu(_causal_conv(buf_ref, proj[:, w:w + SSD_CONV_DIM], cw_ref, cb_ref, blk * t))
    xs = xbc[:, 0:w]
    bm = xbc[:, w:w + gs].astype(BF16)
    cm = xbc[:, w + gs:w + 2 * gs]
    dt = _softplus(proj[:, w + SSD_CONV_DIM:] + dtb_ref[...])
    acs = _sel_dot(tril_ref[...], dt * aneg_ref[...])
    dt_e = _dot_sel(dt, exp_ref[...])
    acs_e = _dot_sel(acs, exp_ref[...])
    alast_e = acs_e[t - 1:t, :]
    xdt = xs * dt_e
    xdt_bf = xdt.astype(BF16)

    row = lax.broadcasted_iota(jnp.int32, (t, t), 0)
    col = lax.broadcasted_iota(jnp.int32, (t, t), 1)
    causal = col <= row
    lane = lax.broadcasted_iota(jnp.int32, (1, LANES), 1)
    acs_t = acs.T
    gmats = []
    for g in range(SSD_GROUPS):
        cg = jnp.where((lane // SSD_STATE) == g, cm, 0.0).astype(BF16)
        gmats.append(_dot_nt(cg, bm))
    hpg = SSD_HEADS // SSD_GROUPS
    for pair in range(SSD_HEADS // 2):
        tl = slice(pair * LANES, (pair + 1) * LANES)
        ys = []
        for sub in range(2):
            h = 2 * pair + sub
            seg = acs[:, h:h + 1] - acs_t[h:h + 1, :]
            m = jnp.where(causal, gmats[h // hpg] * jnp.exp2(seg), 0.0).astype(BF16)
            ys.append(_dot(m, xdt_bf[:, tl]))
        y_diag = jnp.where(lane < SSD_HEADDIM, ys[0], ys[1])
        y_off = _dot(cm.astype(BF16), st_ref[:, tl].astype(BF16)) * jnp.exp2(acs_e[:, tl])
        y = y_diag + y_off + xs[:, tl] * dsk_ref[:, tl]
        y_ref[:, tl] = y * _silu(z[:, tl])
    xdec = (xdt * jnp.exp2(alast_e - acs_e)).astype(BF16)
    upd = _dot_tn(bm, xdec)
    srow = lax.broadcasted_iota(jnp.int32, (gs, w), 0) // SSD_STATE
    scol = lax.broadcasted_iota(jnp.int32, (gs, w), 1) // (hpg * SSD_HEADDIM)
    st_ref[...] = st_ref[...] * jnp.exp2(alast_e) + jnp.where(srow == scol, upd, 0.0)
    gw = w // SSD_GROUPS
    for g in range(SSD_GROUPS):
        yg = y_ref[:, g * gw:(g + 1) * gw]
        o_ref[rows, g * gw:(g + 1) * gw] = (
            yg * lax.rsqrt(jnp.mean(yg * yg, axis=-1, keepdims=True) + NORM_EPS) * snw_ref[:, g * gw:(g + 1) * gw]
        ).astype(o_ref.dtype)


def _sqrt_one_minus_exp2x(x):
    th = jnp.tanh(x)
    y = (-2.0 * th) / (1.0 - th)
    return y * lax.rsqrt(jnp.maximum(y, 1e-30))


def _doubling_scan(a, b, pos, n, axis):
    off = 1
    while off < n:
        keep = pos >= off
        a_sh = jnp.where(keep, pltpu.roll(a, off, axis), 1.0)
        b_sh = jnp.where(keep, pltpu.roll(b, off, axis), 0.0)
        b = a * b_sh + b
        a = a * a_sh
        off *= 2
    return a, b


def _scan_rows(a, b, hc_ref, ab_ref, cg_ref):
    t, n = a.shape
    g = t // SUBLANES
    nt = n // LANES
    a8, b8 = _doubling_scan(a.reshape(g, SUBLANES, n), b.reshape(g, SUBLANES, n),
                            lax.broadcasted_iota(jnp.int32, (1, SUBLANES, 1), 1), SUBLANES, 1)
    a8 = a8.reshape(t, n)
    b8 = b8.reshape(t, n)
    for i in range(nt):
        ab_ref[i] = a8[:, i * LANES:(i + 1) * LANES]
        ab_ref[nt + i] = b8[:, i * LANES:(i + 1) * LANES]
    last = pl.ds(SUBLANES - 1, g, stride=SUBLANES)
    gidx = lax.broadcasted_iota(jnp.int32, (g, 1), 0)
    ag = jnp.concatenate([ab_ref[i, last, :] for i in range(nt)], axis=1)
    bg = jnp.concatenate([ab_ref[nt + i, last, :] for i in range(nt)], axis=1)
    ag, bg = _doubling_scan(ag, bg, gidx, g, 0)
    state = bg + ag * hc_ref[...]
    cg_ref[...] = jnp.where(gidx >= 1, pltpu.roll(state, 1, 0), hc_ref[...])
    hc_ref[...] = state[g - 1:g, :]
    out = []
    for i in range(nt):
        cg = cg_ref[:, i * LANES:(i + 1) * LANES].reshape(g, 1, LANES)
        hi = ab_ref[nt + i].reshape(g, SUBLANES, LANES) + ab_ref[i].reshape(g, SUBLANES, LANES) * cg
        out.append(hi.reshape(t, LANES))
    return jnp.concatenate(out, axis=1)


def _lru_kernel(bps, xn_ref, w_ref, cw_ref, cb_ref, wg_ref, bg_ref, lam_ref, o_ref, buf_ref, hc_ref,
                *scratch):
    nb = xn_ref.shape[0] // T_BLK
    _step_start(bps, nb, hc_ref, buf_ref)
    for blk in range(nb):
        _lru_block(blk, xn_ref, w_ref, cw_ref, cb_ref, wg_ref, bg_ref, lam_ref, o_ref, buf_ref, hc_ref,
                   *scratch[2 * blk:2 * blk + 2])
    _conv_carry(buf_ref)


def _lru_block(blk, xn_ref, w_ref, cw_ref, cb_ref, wg_ref, bg_ref, lam_ref, o_ref, buf_ref, hc_ref,
               ab_ref, cg_ref):
    t = T_BLK
    rows = slice(blk * t, (blk + 1) * t)
    w = BRANCH_WIDTH
    proj = _dot(xn_ref[rows, :], w_ref[...])
    u = _causal_conv(buf_ref, proj[:, 0:w], cw_ref, cb_ref, blk * t)
    rg = _dot(u.astype(BF16), wg_ref[...]) + bg_ref[...]
    r = _sigmoid(rg[:, 0:w])
    ig = _sigmoid(rg[:, w:2 * w])
    log_a = (-LRU_C) * r * _softplus(-lam_ref[...])
    a = jnp.exp(log_a)
    b = _sqrt_one_minus_exp2x(log_a) * (ig * u)
    hseq = _scan_rows(a, b, hc_ref, ab_ref, cg_ref)
    gate = proj[:, w:2 * w]
    gelu = 0.5 * gate * (1.0 + jnp.tanh(math.sqrt(2.0 / math.pi) * (gate + 0.044715 * gate * gate * gate)))
    o_ref[rows, :] = (hseq * gelu).astype(o_ref.dtype)


def _merge_kernel(h_ref, xn_ref, ya_ref, yb_ref, yc_ref, yd_ref, wmg_ref, wbr_ref, wout_ref, o_ref):
    xn = xn_ref[...]
    merged = None
    for n, y_ref in enumerate((ya_ref, yb_ref, yc_ref, yd_ref)):
        gate = _sigmoid(_dot(xn, wmg_ref[:, n * D_MODEL:(n + 1) * D_MODEL]))
        term = gate * _dot(y_ref[...], wbr_ref[n])
        merged = term if merged is None else merged + term
    o_ref[...] = h_ref[...] + _dot(merged.astype(BF16), wout_ref[...])


def _ffn_kernel(final, h_ref, nw_ref, win_ref, wout_ref, nnw_ref, o_ref, *xn_out):
    hres = h_ref[...]
    xn = _rmsnorm(hres, nw_ref[...]).astype(BF16)
    gu = _dot(xn, win_ref[...])
    act = (_silu(gu[:, 0:D_FF]) * gu[:, D_FF:2 * D_FF]).astype(BF16)
    out = hres + _dot(act, wout_ref[...])
    if final:
        o_ref[...] = _rmsnorm(out, nnw_ref[...])
    else:
        o_ref[...] = out
        xn_out[0][...] = _rmsnorm(out, nnw_ref[...]).astype(BF16)


def _norm_kernel(x_ref, nw_ref, o_ref):
    o_ref[...] = _rmsnorm(x_ref[...], nw_ref[...]).astype(o_ref.dtype)


def _const_spec(shape):
    nd = len(shape)
    return pl.BlockSpec(shape, lambda i, _nd=nd: (0,) * _nd, pipeline_mode=pl.Buffered(1))


def _row_spec(rows, width):
    return pl.BlockSpec((rows, width), lambda i: (i, 0))


class _LayerWindow:
    def __init__(self, array, layer, width=None, col=0):
        self.array, self.layer, self.width, self.col = array, layer, width or array.shape[-1], col

    def spec(self):
        shape = self.array.shape
        mid = (0,) * (len(shape) - 2)
        layer, col = self.layer, self.col
        return pl.BlockSpec((None,) + shape[1:-1] + (self.width,), lambda i: (layer,) + mid + (col,),
                            pipeline_mode=pl.Buffered(1))


def _pcall(kernel, name, n_tokens, rows, operands, row_flags, outs, scratch):
    in_specs = [a.spec() if isinstance(a, _LayerWindow) else
                (_row_spec(rows, a.shape[1]) if is_row else _const_spec(a.shape))
                for a, is_row in zip(operands, row_flags)]
    operands = [a.array if isinstance(a, _LayerWindow) else a for a in operands]
    many = isinstance(outs, list)
    out_specs = [_row_spec(rows, wd) for wd, _ in (outs if many else [outs])]
    out_shape = [jax.ShapeDtypeStruct((n_tokens, wd), dt) for wd, dt in (outs if many else [outs])]
    return pl.pallas_call(
        kernel,
        name=name,
        grid=(n_tokens // rows,),
        in_specs=in_specs,
        out_specs=out_specs if many else out_specs[0],
        out_shape=out_shape if many else out_shape[0],
        scratch_shapes=scratch,
        compiler_params=pltpu.CompilerParams(dimension_semantics=("arbitrary",), vmem_limit_bytes=VMEM_LIMIT),
    )(*operands)


def _tril_const(t):
    return jnp.asarray(np.tril(np.ones((t, t), np.float32)), BF16)


def _gla_scratch(nk, nb):
    ntile = nk // LANES
    per_block = [
        pltpu.VMEM((GLA_PARTS, T_BLK, nk), BF16),
        pltpu.VMEM((GLA_PARTS, T_BLK, nk), BF16),
        pltpu.VMEM((2 * T_BLK, nk), BF16),
        pltpu.VMEM((T_BLK, BRANCH_WIDTH), BF16),
    ]
    return [pltpu.VMEM((ntile, HEAD_V, LANES), F32)] + per_block * nb


def _row(x):
    return x.reshape(1, -1).astype(F32)


PACK_MERGE = (4096, 0)
PACK_HGRN = (2048, 2)
PACK_SSD = (2048, 3)
PACK_GLA = (2048, 4)
PACK_LRU = (1024, 10)


PACK_COLS = 11 * 1024
PACK_ROWS = 128
IN_COLS = [0] + np.cumsum([512, 512, 512, 512, 512, 768, 8, 256, 256, 512, 512, 16, 512, 512, 4096]).tolist()


def _pack_kernel(w_ref, wg_ref, o_ref):
    c = IN_COLS

    def put(dst, lo, hi):
        o_ref[:, dst:dst + hi - lo] = w_ref[:, lo:hi].astype(BF16)
        return dst + hi - lo

    def pad(lo, hi):
        o_ref[:, lo:hi] = jnp.zeros((o_ref.shape[0], hi - lo), BF16)

    put(PACK_MERGE[0] * PACK_MERGE[1], c[14], c[15])
    put(PACK_HGRN[0] * PACK_HGRN[1], c[0], c[4])
    start = PACK_SSD[0] * PACK_SSD[1]
    pad(put(start, c[4], c[7]), start + PACK_SSD[0])
    start = PACK_GLA[0] * PACK_GLA[1]
    o_ref[:, start:start + GLA_KDIM] = wg_ref[...].astype(BF16)
    pad(put(start + GLA_KDIM, c[7], c[11]), start + PACK_GLA[0])
    put(PACK_LRU[0] * PACK_LRU[1], c[12], c[14])


def _pack_in_proj(w_in, gla_gate_w):
    depth, rows, cols = w_in.shape
    w_gate = jnp.einsum("lkr,lrn->lkn", w_in[:, :, IN_COLS[11]:IN_COLS[12]].astype(F32), gla_gate_w.astype(F32),
                        precision=lax.Precision.HIGHEST)
    cols = pl.cdiv(cols, LANES) * LANES
    w_in = jnp.pad(w_in.astype(BF16), ((0, 0), (0, 0), (0, cols - w_in.shape[2])))
    return pl.pallas_call(
        _pack_kernel,
        name="pack",
        grid=(depth, rows // PACK_ROWS),
        in_specs=[pl.BlockSpec((None, PACK_ROWS, cols), lambda l, i: (l, i, 0)),
                  pl.BlockSpec((None, PACK_ROWS, GLA_KDIM), lambda l, i: (l, i, 0))],
        out_specs=pl.BlockSpec((None, PACK_ROWS, PACK_COLS), lambda l, i: (l, i, 0)),
        out_shape=jax.ShapeDtypeStruct((depth, rows, PACK_COLS), BF16),
        compiler_params=pltpu.CompilerParams(dimension_semantics=("arbitrary", "arbitrary"),
                                             vmem_limit_bytes=VMEM_LIMIT),
    )(w_in, w_gate)


def _layer(h, xn, seq, l, wts, p):
    n_tokens = h.shape[0]
    bps = seq // T_BLK
    assert n_tokens % seq == 0 and seq % T_STEP == 0
    assert all(seq % (nb * T_BLK) == 0 for nb in (HGRN_CHUNKS, GLA_CHUNKS, SSD_CHUNKS, LRU_CHUNKS))
    tril = _tril_const(T_BLK)
    dmat = _gla_decay_matrix(T_BLK)
    part_ids = _gla_part_ids()
    w_all, w_br, w_o, w_fi, w_fo = wts
    win = lambda pack: _LayerWindow(w_all, l, *pack)

    lb = p["lb"].astype(F32)
    ya = _pcall(
        functools.partial(_hgrn_kernel, bps), "hgrn2", n_tokens, HGRN_CHUNKS * T_BLK,
        [xn, win(PACK_HGRN), _row(lb), _row(1.0 - lb), _row(p["hgrn_norm_w"]), dmat, part_ids],
        [True] + [False] * 6, (BRANCH_WIDTH, BF16), _gla_scratch(BRANCH_WIDTH, HGRN_CHUNKS))

    pad8 = lambda v: jnp.concatenate([v.astype(F32), jnp.zeros((LANES - SSD_HEADS,), F32)]).reshape(1, LANES)
    expand = jnp.asarray(np.arange(LANES)[:, None] == (np.arange(BRANCH_WIDTH)[None, :] // SSD_HEADDIM), BF16)
    yb = _pcall(
        functools.partial(_ssd_kernel, bps), "ssd", n_tokens, SSD_CHUNKS * T_BLK,
        [xn, win(PACK_SSD), p["ssd_conv_w"].astype(F32), _row(p["ssd_conv_b"]), pad8(p["ssd_dt_bias"]),
         pad8(-LOG2E * jnp.exp(p["ssd_a_log"].astype(F32))), _row(jnp.repeat(p["ssd_d"], SSD_HEADDIM)), _row(p["ssd_norm_w"]),
         tril, expand],
        [True] + [False] * 9, (BRANCH_WIDTH, BF16),
        [pltpu.VMEM((SSD_CHUNKS * T_BLK + SUBLANES, SSD_CONV_DIM), F32),
         pltpu.VMEM((SSD_GROUPS * SSD_STATE, BRANCH_WIDTH), F32)]
        + [pltpu.VMEM((T_BLK, BRANCH_WIDTH), F32)] * SSD_CHUNKS)

    yc = _pcall(
        functools.partial(_gla_kernel, bps), "gla", n_tokens, GLA_CHUNKS * T_BLK,
        [xn, win(PACK_GLA), _row(p["gla_gate_b"]), _row(p["gla_norm_w"]), dmat, part_ids],
        [True] + [False] * 5, (BRANCH_WIDTH, BF16), _gla_scratch(GLA_KDIM, GLA_CHUNKS))

    eye = jnp.eye(LRU_BLOCKS, dtype=F32)
    bdiag = lambda wt: jnp.einsum("kde,kl->kdle", wt.astype(F32), eye).reshape(BRANCH_WIDTH, BRANCH_WIDTH)
    w_gates = jnp.concatenate([bdiag(p["lru_wa"]), bdiag(p["lru_wx"])], axis=1).astype(BF16)
    b_gates = _row(jnp.concatenate([p["lru_ba"], p["lru_bx"]]))
    yd = _pcall(
        functools.partial(_lru_kernel, bps), "rglru", n_tokens, LRU_CHUNKS * T_BLK,
        [xn, win(PACK_LRU), p["lru_conv_w"].astype(F32), _row(p["lru_conv_b"]), w_gates,
         b_gates, _row(p["lru_lambda"])],
        [True] + [False] * 6, (BRANCH_WIDTH, BF16),
        [pltpu.VMEM((LRU_CHUNKS * T_BLK + SUBLANES, BRANCH_WIDTH), F32), pltpu.VMEM((1, BRANCH_WIDTH), F32)]
        + [pltpu.VMEM((2 * BRANCH_WIDTH // LANES, T_BLK, LANES), F32),
           pltpu.VMEM((T_BLK // SUBLANES, BRANCH_WIDTH), F32)] * LRU_CHUNKS)

    h = _pcall(
        _merge_kernel, "merge", n_tokens, T_STEP,
        [h, xn, ya, yb, yc, yd, win(PACK_MERGE), _LayerWindow(w_br, l), _LayerWindow(w_o, l)],
        [True, True, True, True, True, True, False, False, False], (D_MODEL, F32), [])

    outs = (D_MODEL, F32) if p["final"] else [(D_MODEL, F32), (D_MODEL, BF16)]
    res = _pcall(
        functools.partial(_ffn_kernel, p["final"]), "swiglu", n_tokens, T_STEP,
        [h, _row(p["norm_ffn_w"]), _LayerWindow(w_fi, l), _LayerWindow(w_fo, l), _row(p["next_norm_w"])],
        [True, False, False, False, False], outs, [])
    return (res, None) if p["final"] else tuple(res)


def kernel(x, norm_mix_w, w_in, hgrn_lower_bounds, hgrn_norm_w, ssd_conv_w, ssd_conv_b, ssd_dt_bias, ssd_a_log, ssd_d, ssd_norm_w, gla_gate_w, gla_gate_b, gla_norm_w, lru_conv_w, lru_conv_b, lru_wa, lru_ba, lru_wx, lru_bx, lru_lambda, w_branch, w_out, norm_ffn_w, w_ffn_in, w_ffn_out, norm_f_w):
    bsz, seq, _ = x.shape
    depth = w_in.shape[0]
    assert (bsz * seq) % NORM_ROWS == 0
    lb_all = jnp.cumsum(jax.nn.softmax(hgrn_lower_bounds.astype(F32), axis=0), axis=0)
    lb_all = lb_all - lb_all[0:1]
    wts = (_pack_in_proj(w_in, gla_gate_w), w_branch.astype(BF16), w_out.astype(BF16), w_ffn_in.astype(BF16),
           w_ffn_out.astype(BF16))
    h = x.reshape(bsz * seq, D_MODEL)
    xn = _pcall(_norm_kernel, "norm", bsz * seq, NORM_ROWS, [h, _row(norm_mix_w[0])], [True, False], (D_MODEL, BF16), [])
    for l in range(depth):
        p = dict(
            lb=lb_all[l], hgrn_norm_w=hgrn_norm_w[l],
            ssd_conv_w=ssd_conv_w[l], ssd_conv_b=ssd_conv_b[l], ssd_dt_bias=ssd_dt_bias[l], ssd_a_log=ssd_a_log[l],
            ssd_d=ssd_d[l], ssd_norm_w=ssd_norm_w[l], gla_gate_b=gla_gate_b[l],
            gla_norm_w=gla_norm_w[l], lru_conv_w=lru_conv_w[l], lru_conv_b=lru_conv_b[l], lru_wa=lru_wa[l],
            lru_ba=lru_ba[l], lru_wx=lru_wx[l], lru_bx=lru_bx[l], lru_lambda=lru_lambda[l],
            norm_ffn_w=norm_ffn_w[l],
            next_norm_w=(norm_f_w if l == depth - 1 else norm_mix_w[l + 1]), final=(l == depth - 1))
        h, xn = _layer(h, xn, seq, l, wts, p)
    return h.reshape(bsz, seq, D_MODEL)
```

```python
import functools
import math

import jax
import jax.numpy as jnp
import numpy as np
from jax import lax
from jax.experimental import pallas as pl
from jax.experimental.pallas import tpu as pltpu

F32 = jnp.float32
BF16 = jnp.bfloat16

D_MODEL = 1024
BRANCH_WIDTH = 512
N_BRANCHES = 4
CONV_WIDTH = 4
NORM_EPS = 1e-6
HGRN_HEADS = 4
SSD_HEADS = 8
SSD_HEADDIM = 64
SSD_GROUPS = 2
SSD_STATE = 64
SSD_CONV_DIM = 768
GLA_HEADS = 4
GLA_KDIM = 256
GLA_HEAD_K = 64
GLA_GATE_RANK = 16
GLA_GATE_NORMALIZER = 16.0
LRU_BLOCKS = 8
LRU_BLOCK = 64
LRU_C = 8.0
D_FF = 2816

LANES = 128
SUBLANES = 8
T_BLK = 256
T_STEP = 2 * T_BLK
HGRN_CHUNKS, GLA_CHUNKS, SSD_CHUNKS, LRU_CHUNKS = 4, 4, 4, 4
NORM_ROWS = 2048
HEAD_V = 128
VMEM_LIMIT = 48 * 1024 * 1024
HGRN_VMEM_LIMIT = 56 * 1024 * 1024
F32_FLOOR = 1e-37
SSD_PROJ = BRANCH_WIDTH + SSD_CONV_DIM + LANES
GLA_PROJ = 3 * GLA_KDIM + 2 * BRANCH_WIDTH


def _dot(a, b):
    return jnp.dot(a, b, preferred_element_type=F32)


def _dot_nt(a, b):
    return lax.dot_general(a, b, (((1,), (1,)), ((), ())), preferred_element_type=F32)


def _dot_tn(a, b):
    return lax.dot_general(a, b, (((0,), (0,)), ((), ())), preferred_element_type=F32)


def _split3(x):
    hi = x.astype(BF16)
    r1 = x - hi.astype(F32)
    mid = r1.astype(BF16)
    lo = (r1 - mid.astype(F32)).astype(BF16)
    return hi, mid, lo


def _sel_dot(mat, x):
    hi, mid, lo = _split3(x)
    return (_dot(mat, lo) + _dot(mat, mid)) + _dot(mat, hi)


def _dot_sel(x, mat):
    hi, mid, lo = _split3(x)
    return (_dot(lo, mat) + _dot(mid, mat)) + _dot(hi, mat)


def _sigmoid(x):
    return jax.nn.sigmoid(x)


def _silu(x):
    return x * _sigmoid(x)


def _softplus(x):
    return jnp.maximum(x, 0.0) + jnp.log1p(jnp.exp(-jnp.abs(x)))


def _log_sigmoid(x):
    return jnp.minimum(x, 0.0) - jnp.log1p(jnp.exp(-jnp.abs(x)))


def _rmsnorm(x, w):
    return x * lax.rsqrt(jnp.mean(x * x, axis=-1, keepdims=True) + NORM_EPS) * w


GLA_SMALL = (1, 2, 4)
GLA_MID = (8, 16, 32, 64)
GLA_PARTS = 2 + len(GLA_SMALL) + len(GLA_MID) + 1
LOG2E = 1.4426950408889634
GLA_SPAN_LIMIT = 48.0
HGRN_BASE = 32
GLA_BASE = 128


def _gla_part_ids():
    row = np.arange(LANES)[:, None]
    col = np.arange(LANES)[None, :]
    x = row ^ col
    ids = np.where(x == 0, 0, np.floor(np.log2(np.maximum(x, 1))).astype(np.int64) + 1)
    return jnp.asarray(np.where(col <= row, ids, -1), jnp.int32)


def _gla_decay_matrix(t):
    u = np.arange(t)[None, :]
    r = np.arange(t)[:, None]
    blocks = [u <= r]
    for s in GLA_SMALL:
        mid = (r // (2 * s)) * (2 * s) + s - 1
        blocks.append(((u > mid) & (u <= r)) | ((u > r) & (u <= mid)))
    m = np.concatenate(blocks, axis=0)
    return jnp.asarray(np.concatenate([m, m], axis=1), BF16)


def _store_decay_pieces(rhs_ref, logf):
    t = logf.shape[0]
    lf2 = logf * LOG2E
    hi = lf2.astype(BF16)
    rhs_ref[0:t, :] = hi
    rhs_ref[t:2 * t, :] = (lf2 - hi.astype(F32)).astype(BF16)


def _gla_chunk(q, k, logf, v_ref, st_ref, ql_ref, kl_ref, rhs_ref, dm_ref, ids_ref, hpt, base, emit):
    t, nk = q.shape
    assert t == 2 * LANES
    ntile = nk // LANES
    dk = LANES // hpt
    levels = GLA_SMALL + GLA_MID

    _store_decay_pieces(rhs_ref, logf)
    bcum = _dot(dm_ref[0:t, :], rhs_ref[...])
    blast = bcum[t - 1:t, :]
    qb = q.astype(BF16)
    kb = k.astype(BF16)
    ql_ref[0] = qb * jnp.exp2(bcum).astype(BF16)
    kl_ref[0] = kb * jnp.exp2(blast - bcum).astype(BF16)

    def level_part(part, s):
        bc3 = bcum.reshape(t // (2 * s), 2 * s, nk)
        ref = bc3[:, s - 1:s, :]
        e = jnp.exp2(jnp.concatenate([ref - bc3[:, 0:s, :], bc3[:, s:2 * s, :] - ref], axis=1))
        e = e.reshape(t, nk).astype(BF16)
        ql_ref[part] = qb * e
        kl_ref[part] = kb * e

    part_of = {}
    part = 1
    for s in levels + (LANES,):
        if s >= base:
            level_part(part, s)
            part_of[s] = part
            part += 1
    free = part
    p_top = part_of[LANES]

    bcb = bcum.reshape(t // base, base, nk)
    span = bcb - bcb[:, base // 2 - 1:base // 2, :]
    fits = jnp.max(jnp.abs(span)) <= GLA_SPAN_LIMIT

    ids = ids_ref[...]
    lane = lax.broadcasted_iota(jnp.int32, (1, LANES), 1)

    def own(x, sub):
        return x if hpt == 1 else jnp.where((lane // dk) == sub, x, jnp.zeros_like(x))

    def stacked(x):
        return x if hpt == 1 else jnp.concatenate([own(x, sub) for sub in range(hpt)], axis=0)

    ids_st = ids if hpt == 1 else jnp.concatenate([ids] * hpt, axis=0)

    def finish(first, overrides):
        intra = []
        for tau in range(ntile):
            tl = slice(tau * LANES, (tau + 1) * LANES)
            diag = []
            for c in range(2):
                rows = slice(c * LANES, (c + 1) * LANES)
                sc = jnp.where(ids_st >= 0, _dot_nt(stacked(ql_ref[first, rows, tl]), kl_ref[first, rows, tl]), 0.0)
                for pid, p in overrides:
                    sc = jnp.where(ids_st == pid, _dot_nt(stacked(ql_ref[p, rows, tl]), kl_ref[p, rows, tl]), sc)
                diag.append(sc.astype(BF16))
            low = _dot_nt(stacked(ql_ref[p_top, LANES:t, tl]), kl_ref[p_top, 0:LANES, tl]).astype(BF16)
            for sub in range(hpt):
                hv = slice((tau * hpt + sub) * HEAD_V, (tau * hpt + sub + 1) * HEAD_V)
                hr = slice(sub * LANES, (sub + 1) * LANES)
                o_top = _dot(diag[0][hr], v_ref[0:LANES, hv])
                o_bot = _dot(jnp.concatenate([low[hr], diag[1][hr]], axis=1), v_ref[:, hv])
                intra.append(jnp.concatenate([o_top, o_bot], axis=0))
            yield
        for tau in range(ntile):
            tl = slice(tau * LANES, (tau + 1) * LANES)
            inter = _dot_nt(stacked(ql_ref[0, :, tl]), st_ref[tau].astype(BF16))
            for sub in range(hpt):
                h = tau * hpt + sub
                emit(h, intra[h] + inter[sub * t:(sub + 1) * t])
        for tau in range(ntile):
            tl = slice(tau * LANES, (tau + 1) * LANES)
            kd = kl_ref[0, :, tl]
            upd = _dot_tn(v_ref[:, tau * hpt * HEAD_V:(tau * hpt + 1) * HEAD_V], kd)
            for sub in range(1, hpt):
                h = tau * hpt + sub
                upd = jnp.where((lane // dk) == sub, _dot_tn(v_ref[:, h * HEAD_V:(h + 1) * HEAD_V], kd), upd)
            st_ref[tau] = st_ref[tau] * jnp.exp2(blast[:, tl]) + upd

    def part_id(s):
        return 1 + levels.index(s)

    upper = [(part_id(s), part_of[s]) for s in levels if s >= base]

    def one_reference():
        ql_ref[free] = qb * jnp.exp2(span).reshape(t, nk).astype(BF16)
        kl_ref[free] = kb * jnp.exp2(-span).reshape(t, nk).astype(BF16)
        yield from finish(free, upper)

    def safe_parts():
        d3 = _dot(dm_ref[t:(1 + len(GLA_SMALL)) * t, :], rhs_ref[...])
        lower = []
        part = free
        for i, s in enumerate(GLA_SMALL):
            if s < base:
                e = jnp.exp2(d3[i * t:(i + 1) * t]).astype(BF16)
                ql_ref[part] = qb * e
                kl_ref[part] = kb * e
                lower.append((part_id(s), part))
                part += 1
        for s in GLA_MID:
            if s < base:
                level_part(part, s)
                lower.append((part_id(s), part))
                part += 1
        ql_ref[part] = qb
        kl_ref[part] = kb
        yield from finish(part, lower + upper)

    return fits, one_reference, safe_parts


def _interleave(chains):
    live = list(chains)
    while live:
        for c in list(live):
            try:
                next(c)
            except StopIteration:
                live.remove(c)


def _gla_finish(chunks):
    fits = functools.reduce(jnp.logical_and, [c[0] for c in chunks])

    @pl.when(fits)
    def _():
        _interleave([c[1]() for c in chunks])

    @pl.when(jnp.logical_not(fits))
    def _():
        _interleave([c[2]() for c in chunks])


def _causal_conv(buf_ref, x, w_ref, b_ref, base):
    t = x.shape[0]
    buf_ref[SUBLANES + base:SUBLANES + base + t, :] = x
    acc = b_ref[...] + w_ref[CONV_WIDTH - 1:CONV_WIDTH, :] * x
    for s in range(1, CONV_WIDTH):
        acc = acc + w_ref[CONV_WIDTH - 1 - s:CONV_WIDTH - s, :] * buf_ref[pl.ds(SUBLANES + base - s, t), :]
    return acc


def _step_start(bps, nb, *carry_refs):
    @pl.when((pl.program_id(0) * nb) % bps == 0)
    def _():
        for ref in carry_refs:
            ref[...] = jnp.zeros_like(ref)


def _conv_carry(buf_ref):
    rows = buf_ref.shape[0] - SUBLANES
    buf_ref[0:SUBLANES, :] = buf_ref[rows:rows + SUBLANES, :]


def _hgrn_kernel(bps, xn_ref, w_ref, lb_ref, oml_ref, hnw_ref, dm_ref, ids_ref,
                 o_ref, st_ref, *scratch):
    nb = xn_ref.shape[0] // T_BLK
    _step_start(bps, nb, st_ref)
    w = BRANCH_WIDTH
    staged = []
    for blk in range(nb):
        rows = slice(blk * T_BLK, (blk + 1) * T_BLK)
        ql_ref, kl_ref, rhs_ref, v_ref = scratch[4 * blk:4 * blk + 4]
        proj = _dot(xn_ref[rows, :], w_ref[...])
        fp = proj[:, w:2 * w]
        u = jnp.exp(-jnp.abs(fp))
        r = 1.0 / (1.0 + u)
        pos = fp >= 0.0
        sig = jnp.where(pos, r, u * r)
        key = oml_ref[...] * jnp.where(pos, u * r, r)
        logf = jnp.log(jnp.maximum(lb_ref[...] + oml_ref[...] * sig, F32_FLOOR))
        qf = _silu(proj[:, 0:w])
        v_ref[...] = proj[:, 2 * w:3 * w].astype(BF16)
        staged.append((qf, key, logf, _silu(proj[:, 3 * w:4 * w])))
    chunks = []
    for blk in range(nb):
        rows = slice(blk * T_BLK, (blk + 1) * T_BLK)
        ql_ref, kl_ref, rhs_ref, v_ref = scratch[4 * blk:4 * blk + 4]
        qf, key, logf, gs = staged[blk]

        def emit(h, o, rows=rows, gs=gs):
            cols = slice(h * HEAD_V, (h + 1) * HEAD_V)
            o_ref[rows, cols] = (_rmsnorm(o, hnw_ref[...]) * gs[:, cols]).astype(o_ref.dtype)

        chunks.append(_gla_chunk(qf, key, logf, v_ref, st_ref, ql_ref, kl_ref, rhs_ref, dm_ref, ids_ref, 1,
                                 HGRN_BASE, emit))
    _gla_finish(chunks)


def _gla_kernel(bps, xn_ref, w_ref, gb_ref, gnw_ref, dm_ref, ids_ref, o_ref, st_ref, *scratch):
    nb = xn_ref.shape[0] // T_BLK
    _step_start(bps, nb, st_ref)
    kd, w = GLA_KDIM, BRANCH_WIDTH
    chunks = []
    for blk in range(nb):
        rows = slice(blk * T_BLK, (blk + 1) * T_BLK)
        ql_ref, kl_ref, rhs_ref, v_ref = scratch[4 * blk:4 * blk + 4]
        proj = _dot(xn_ref[rows, :], w_ref[:, 0:GLA_PROJ])
        logf = _log_sigmoid(proj[:, 0:kd] + gb_ref[...]) * (1.0 / GLA_GATE_NORMALIZER)
        proj = proj[:, kd:]
        q = proj[:, 0:kd] * (GLA_HEAD_K ** -0.5)
        k = proj[:, kd:2 * kd]
        v_ref[...] = proj[:, 2 * kd:2 * kd + w].astype(BF16)
        gs = _silu(proj[:, 2 * kd + w:2 * kd + 2 * w])

        def emit(h, o, rows=rows, gs=gs):
            cols = slice(h * HEAD_V, (h + 1) * HEAD_V)
            o_ref[rows, cols] = (_rmsnorm(o, gnw_ref[...]) * gs[:, cols]).astype(o_ref.dtype)

        chunks.append(_gla_chunk(q, k, logf, v_ref, st_ref, ql_ref, kl_ref, rhs_ref, dm_ref, ids_ref, 2,
                                 GLA_BASE, emit))
    _gla_finish(chunks)


def _ssd_kernel(bps, xn_ref, w_ref, cw_ref, cb_ref, dtb_ref, aneg_ref, dsk_ref, snw_ref, tril_ref,
                exp_ref, o_ref, buf_ref, st_ref, *y_refs):
    nb = xn_ref.shape[0] // T_BLK
    _step_start(bps, nb, st_ref, buf_ref)
    for blk in range(nb):
        _ssd_block(blk, xn_ref, w_ref, cw_ref, cb_ref, dtb_ref, aneg_ref, dsk_ref, snw_ref, tril_ref,
                   exp_ref, o_ref, buf_ref, st_ref, y_refs[blk])
    _conv_carry(buf_ref)


def _ssd_block(blk, xn_ref, w_ref, cw_ref, cb_ref, dtb_ref, aneg_ref, dsk_ref, snw_ref, tril_ref,
               exp_ref, o_ref, buf_ref, st_ref, y_ref):
    t = T_BLK
    rows = slice(blk * t, (blk + 1) * t)
    w = BRANCH_WIDTH
    gs = SSD_GROUPS * SSD_STATE
    proj = _dot(xn_ref[rows, :], w_ref[:, 0:SSD_PROJ])
    z = proj[:, 0:w]
    xbc = _silu(_causal_conv(buf_ref, proj[:, w:w + SSD_CONV_DIM], cw_ref, cb_ref, blk * t))
    xs = xbc[:, 0:w]
    bm = xbc[:, w:w + gs].astype(BF16)
    cm = xbc[:, w + gs:w + 2 * gs]
    dt = _softplus(proj[:, w + SSD_CONV_DIM:] + dtb_ref[...])
    acs = _sel_dot(tril_ref[...], dt * aneg_ref[...])
    dt_e = _dot_sel(dt, exp_ref[...])
    acs_e = _dot_sel(acs, exp_ref[...])
    alast_e = acs_e[t - 1:t, :]
    xdt = xs * dt_e
    xdt_bf = xdt.astype(BF16)

    row = lax.broadcasted_iota(jnp.int32, (t, t), 0)
    col = lax.broadcasted_iota(jnp.int32, (t, t), 1)
    causal = col <= row
    lane = lax.broadcasted_iota(jnp.int32, (1, LANES), 1)
    acs_t = acs.T
    gmats = []
    for g in range(SSD_GROUPS):
        cg = jnp.where((lane // SSD_STATE) == g, cm, 0.0).astype(BF16)
        gmats.append(_dot_nt(cg, bm))
    hpg = SSD_HEADS // SSD_GROUPS
    for pair in range(SSD_HEADS // 2):
        tl = slice(pair * LANES, (pair + 1) * LANES)
        ys = []
        for sub in range(2):
            h = 2 * pair + sub
            seg = acs[:, h:h + 1] - acs_t[h:h + 1, :]
            m = jnp.where(causal, gmats[h // hpg] * jnp.exp2(seg), 0.0).astype(BF16)
            ys.append(_dot(m, xdt_bf[:, tl]))
        y_diag = jnp.where(lane < SSD_HEADDIM, ys[0], ys[1])
        y_off = _dot(cm.astype(BF16), st_ref[:, tl].astype(BF16)) * jnp.exp2(acs_e[:, tl])
        y = y_diag + y_off + xs[:, tl] * dsk_ref[:, tl]
        y_ref[:, tl] = y * _silu(z[:, tl])
    xdec = (xdt * jnp.exp2(alast_e - acs_e)).astype(BF16)
    upd = _dot_tn(bm, xdec)
    srow = lax.broadcasted_iota(jnp.int32, (gs, w), 0) // SSD_STATE
    scol = lax.broadcasted_iota(jnp.int32, (gs, w), 1) // (hpg * SSD_HEADDIM)
    st_ref[...] = st_ref[...] * jnp.exp2(alast_e) + jnp.where(srow == scol, upd, 0.0)
    gw = w // SSD_GROUPS
    for g in range(SSD_GROUPS):
        yg = y_ref[:, g * gw:(g + 1) * gw]
        o_ref[rows, g * gw:(g + 1) * gw] = (
            yg * lax.rsqrt(jnp.mean(yg * yg, axis=-1, keepdims=True) + NORM_EPS) * snw_ref[:, g * gw:(g + 1) * gw]
        ).astype(o_ref.dtype)


def _sqrt_one_minus_exp2x(x):
    th = jnp.tanh(x)
    y = (-2.0 * th) / (1.0 - th)
    return y * lax.rsqrt(jnp.maximum(y, 1e-30))


def _doubling_scan(a, b, pos, n, axis):
    off = 1
    while off < n:
        keep = pos >= off
        a_sh = jnp.where(keep, pltpu.roll(a, off, axis), 1.0)
        b_sh = jnp.where(keep, pltpu.roll(b, off, axis), 0.0)
        b = a * b_sh + b
        a = a * a_sh
        off *= 2
    return a, b


def _scan_rows(a, b, hc_ref, ab_ref, cg_ref):
    t, n = a.shape
    g = t // SUBLANES
    nt = n // LANES
    a8, b8 = _doubling_scan(a.reshape(g, SUBLANES, n), b.reshape(g, SUBLANES, n),
                            lax.broadcasted_iota(jnp.int32, (1, SUBLANES, 1), 1), SUBLANES, 1)
    a8 = a8.reshape(t, n)
    b8 = b8.reshape(t, n)
    for i in range(nt):
        ab_ref[i] = a8[:, i * LANES:(i + 1) * LANES]
        ab_ref[nt + i] = b8[:, i * LANES:(i + 1) * LANES]
    last = pl.ds(SUBLANES - 1, g, stride=SUBLANES)
    gidx = lax.broadcasted_iota(jnp.int32, (g, 1), 0)
    ag = jnp.concatenate([ab_ref[i, last, :] for i in range(nt)], axis=1)
    bg = jnp.concatenate([ab_ref[nt + i, last, :] for i in range(nt)], axis=1)
    ag, bg = _doubling_scan(ag, bg, gidx, g, 0)
    state = bg + ag * hc_ref[...]
    cg_ref[...] = jnp.where(gidx >= 1, pltpu.roll(state, 1, 0), hc_ref[...])
    hc_ref[...] = state[g - 1:g, :]
    out = []
    for i in range(nt):
        cg = cg_ref[:, i * LANES:(i + 1) * LANES].reshape(g, 1, LANES)
        hi = ab_ref[nt + i].reshape(g, SUBLANES, LANES) + ab_ref[i].reshape(g, SUBLANES, LANES) * cg
        out.append(hi.reshape(t, LANES))
    return jnp.concatenate(out, axis=1)


def _lru_kernel(bps, xn_ref, w_ref, cw_ref, cb_ref, wg_ref, bg_ref, lam_ref, o_ref, buf_ref, hc_ref,
                *scratch):
    nb = xn_ref.shape[0] // T_BLK
    _step_start(bps, nb, hc_ref, buf_ref)
    for blk in range(nb):
        _lru_block(blk, xn_ref, w_ref, cw_ref, cb_ref, wg_ref, bg_ref, lam_ref, o_ref, buf_ref, hc_ref,
                   *scratch[2 * blk:2 * blk + 2])
    _conv_carry(buf_ref)


def _lru_block(blk, xn_ref, w_ref, cw_ref, cb_ref, wg_ref, bg_ref, lam_ref, o_ref, buf_ref, hc_ref,
               ab_ref, cg_ref):
    t = T_BLK
    rows = slice(blk * t, (blk + 1) * t)
    w = BRANCH_WIDTH
    proj = _dot(xn_ref[rows, :], w_ref[...])
    u = _causal_conv(buf_ref, proj[:, 0:w], cw_ref, cb_ref, blk * t)
    rg = _dot(u.astype(BF16), wg_ref[...]) + bg_ref[...]
    r = _sigmoid(rg[:, 0:w])
    ig = _sigmoid(rg[:, w:2 * w])
    log_a = (-LRU_C) * r * _softplus(-lam_ref[...])
    a = jnp.exp(log_a)
    b = _sqrt_one_minus_exp2x(log_a) * (ig * u)
    hseq = _scan_rows(a, b, hc_ref, ab_ref, cg_ref)
    gate = proj[:, w:2 * w]
    gelu = 0.5 * gate * (1.0 + jnp.tanh(math.sqrt(2.0 / math.pi) * (gate + 0.044715 * gate * gate * gate)))
    o_ref[rows, :] = (hseq * gelu).astype(o_ref.dtype)


def _merge_kernel(h_ref, xn_ref, ya_ref, yb_ref, yc_ref, yd_ref, wmg_ref, wbr_ref, wout_ref, o_ref):
    xn = xn_ref[...]
    merged = None
    for n, y_ref in enumerate((ya_ref, yb_ref, yc_ref, yd_ref)):
        gate = _sigmoid(_dot(xn, wmg_ref[:, n * D_MODEL:(n + 1) * D_MODEL]))
        term = gate * _dot(y_ref[...], wbr_ref[n])
        merged = term if merged is None else merged + term
    o_ref[...] = h_ref[...] + _dot(merged.astype(BF16), wout_ref[...])


def _ffn_kernel(final, h_ref, nw_ref, win_ref, wout_ref, nnw_ref, o_ref, *xn_out):
    hres = h_ref[...]
    xn = _rmsnorm(hres, nw_ref[...]).astype(BF16)
    gu = _dot(xn, win_ref[...])
    act = (_silu(gu[:, 0:D_FF]) * gu[:, D_FF:2 * D_FF]).astype(BF16)
    out = hres + _dot(act, wout_ref[...])
    if final:
        o_ref[...] = _rmsnorm(out, nnw_ref[...])
    else:
        o_ref[...] = out
        xn_out[0][...] = _rmsnorm(out, nnw_ref[...]).astype(BF16)


def _norm_kernel(x_ref, nw_ref, o_ref):
    o_ref[...] = _rmsnorm(x_ref[...], nw_ref[...]).astype(o_ref.dtype)


def _const_spec(shape):
    nd = len(shape)
    return pl.BlockSpec(shape, lambda i, _nd=nd: (0,) * _nd, pipeline_mode=pl.Buffered(1))


def _row_spec(rows, width):
    return pl.BlockSpec((rows, width), lambda i: (i, 0))


class _LayerWindow:
    def __init__(self, array, layer, width=None, col=0):
        self.array, self.layer, self.width, self.col = array, layer, width or array.shape[-1], col

    def spec(self):
        shape = self.array.shape
        mid = (0,) * (len(shape) - 2)
        layer, col = self.layer, self.col
        return pl.BlockSpec((None,) + shape[1:-1] + (self.width,), lambda i: (layer,) + mid + (col,),
                            pipeline_mode=pl.Buffered(1))


def _pcall(kernel, name, n_tokens, rows, operands, row_flags, outs, scratch, vmem_limit=VMEM_LIMIT):
    in_specs = [a.spec() if isinstance(a, _LayerWindow) else
                (_row_spec(rows, a.shape[1]) if is_row else _const_spec(a.shape))
                for a, is_row in zip(operands, row_flags)]
    operands = [a.array if isinstance(a, _LayerWindow) else a for a in operands]
    many = isinstance(outs, list)
    out_specs = [_row_spec(rows, wd) for wd, _ in (outs if many else [outs])]
    out_shape = [jax.ShapeDtypeStruct((n_tokens, wd), dt) for wd, dt in (outs if many else [outs])]
    return pl.pallas_call(
        kernel,
        name=name,
        grid=(n_tokens // rows,),
        in_specs=in_specs,
        out_specs=out_specs if many else out_specs[0],
        out_shape=out_shape if many else out_shape[0],
        scratch_shapes=scratch,
        compiler_params=pltpu.CompilerParams(dimension_semantics=("arbitrary",), vmem_limit_bytes=vmem_limit),
    )(*operands)


def _tril_const(t):
    return jnp.asarray(np.tril(np.ones((t, t), np.float32)), BF16)


def _gla_scratch(nk, nb):
    ntile = nk // LANES
    per_block = [
        pltpu.VMEM((GLA_PARTS, T_BLK, nk), BF16),
        pltpu.VMEM((GLA_PARTS, T_BLK, nk), BF16),
        pltpu.VMEM((2 * T_BLK, nk), BF16),
        pltpu.VMEM((T_BLK, BRANCH_WIDTH), BF16),
    ]
    return [pltpu.VMEM((ntile, HEAD_V, LANES), F32)] + per_block * nb


def _row(x):
    return x.reshape(1, -1).astype(F32)


PACK_MERGE = (4096, 0)
PACK_HGRN = (2048, 2)
PACK_SSD = (2048, 3)
PACK_GLA = (2048, 4)
PACK_LRU = (1024, 10)


PACK_COLS = 11 * 1024
PACK_ROWS = 128
IN_COLS = [0] + np.cumsum([512, 512, 512, 512, 512, 768, 8, 256, 256, 512, 512, 16, 512, 512, 4096]).tolist()


def _pack_kernel(w_ref, wg_ref, o_ref):
    c = IN_COLS

    def put(dst, lo, hi):
        o_ref[:, dst:dst + hi - lo] = w_ref[:, lo:hi].astype(BF16)
        return dst + hi - lo

    def pad(lo, hi):
        o_ref[:, lo:hi] = jnp.zeros((o_ref.shape[0], hi - lo), BF16)

    put(PACK_MERGE[0] * PACK_MERGE[1], c[14], c[15])
    put(PACK_HGRN[0] * PACK_HGRN[1], c[0], c[4])
    start = PACK_SSD[0] * PACK_SSD[1]
    pad(put(start, c[4], c[7]), start + PACK_SSD[0])
    start = PACK_GLA[0] * PACK_GLA[1]
    o_ref[:, start:start + GLA_KDIM] = wg_ref[...].astype(BF16)
    pad(put(start + GLA_KDIM, c[7], c[11]), start + PACK_GLA[0])
    put(PACK_LRU[0] * PACK_LRU[1], c[12], c[14])


def _pack_in_proj(w_in, gla_gate_w):
    depth, rows, cols = w_in.shape
    w_gate = jnp.einsum("lkr,lrn->lkn", w_in[:, :, IN_COLS[11]:IN_COLS[12]].astype(F32), gla_gate_w.astype(F32),
                        precision=lax.Precision.HIGHEST)
    cols = pl.cdiv(cols, LANES) * LANES
    w_in = jnp.pad(w_in.astype(BF16), ((0, 0), (0, 0), (0, cols - w_in.shape[2])))
    return pl.pallas_call(
        _pack_kernel,
        name="pack",
        grid=(depth, rows // PACK_ROWS),
        in_specs=[pl.BlockSpec((None, PACK_ROWS, cols), lambda l, i: (l, i, 0)),
                  pl.BlockSpec((None, PACK_ROWS, GLA_KDIM), lambda l, i: (l, i, 0))],
        out_specs=pl.BlockSpec((None, PACK_ROWS, PACK_COLS), lambda l, i: (l, i, 0)),
        out_shape=jax.ShapeDtypeStruct((depth, rows, PACK_COLS), BF16),
        compiler_params=pltpu.CompilerParams(dimension_semantics=("arbitrary", "arbitrary"),
                                             vmem_limit_bytes=VMEM_LIMIT),
    )(w_in, w_gate)


def _layer(h, xn, seq, l, wts, p):
    n_tokens = h.shape[0]
    bps = seq // T_BLK
    assert n_tokens % seq == 0 and seq % T_STEP == 0
    assert all(seq % (nb * T_BLK) == 0 for nb in (HGRN_CHUNKS, GLA_CHUNKS, SSD_CHUNKS, LRU_CHUNKS))
    tril = _tril_const(T_BLK)
    dmat = _gla_decay_matrix(T_BLK)
    part_ids = _gla_part_ids()
    w_all, w_br, w_o, w_fi, w_fo = wts
    win = lambda pack: _LayerWindow(w_all, l, *pack)

    lb = p["lb"].astype(F32)
    ya = _pcall(
        functools.partial(_hgrn_kernel, bps), "hgrn2", n_tokens, HGRN_CHUNKS * T_BLK,
        [xn, win(PACK_HGRN), _row(lb), _row(1.0 - lb), _row(p["hgrn_norm_w"]), dmat, part_ids],
        [True] + [False] * 6, (BRANCH_WIDTH, BF16), _gla_scratch(BRANCH_WIDTH, HGRN_CHUNKS), HGRN_VMEM_LIMIT)

    pad8 = lambda v: jnp.concatenate([v.astype(F32), jnp.zeros((LANES - SSD_HEADS,), F32)]).reshape(1, LANES)
    expand = jnp.asarray(np.arange(LANES)[:, None] == (np.arange(BRANCH_WIDTH)[None, :] // SSD_HEADDIM), BF16)
    yb = _pcall(
        functools.partial(_ssd_kernel, bps), "ssd", n_tokens, SSD_CHUNKS * T_BLK,
        [xn, win(PACK_SSD), p["ssd_conv_w"].astype(F32), _row(p["ssd_conv_b"]), pad8(p["ssd_dt_bias"]),
         pad8(-LOG2E * jnp.exp(p["ssd_a_log"].astype(F32))), _row(jnp.repeat(p["ssd_d"], SSD_HEADDIM)), _row(p["ssd_norm_w"]),
         tril, expand],
        [True] + [False] * 9, (BRANCH_WIDTH, BF16),
        [pltpu.VMEM((SSD_CHUNKS * T_BLK + SUBLANES, SSD_CONV_DIM), F32),
         pltpu.VMEM((SSD_GROUPS * SSD_STATE, BRANCH_WIDTH), F32)]
        + [pltpu.VMEM((T_BLK, BRANCH_WIDTH), F32)] * SSD_CHUNKS)

    yc = _pcall(
        functools.partial(_gla_kernel, bps), "gla", n_tokens, GLA_CHUNKS * T_BLK,
        [xn, win(PACK_GLA), _row(p["gla_gate_b"]), _row(p["gla_norm_w"]), dmat, part_ids],
        [True] + [False] * 5, (BRANCH_WIDTH, BF16), _gla_scratch(GLA_KDIM, GLA_CHUNKS))

    eye = jnp.eye(LRU_BLOCKS, dtype=F32)
    bdiag = lambda wt: jnp.einsum("kde,kl->kdle", wt.astype(F32), eye).reshape(BRANCH_WIDTH, BRANCH_WIDTH)
    w_gates = jnp.concatenate([bdiag(p["lru_wa"]), bdiag(p["lru_wx"])], axis=1).astype(BF16)
    b_gates = _row(jnp.concatenate([p["lru_ba"], p["lru_bx"]]))
    yd = _pcall(
        functools.partial(_lru_kernel, bps), "rglru", n_tokens, LRU_CHUNKS * T_BLK,
        [xn, win(PACK_LRU), p["lru_conv_w"].astype(F32), _row(p["lru_conv_b"]), w_gates,
         b_gates, _row(p["lru_lambda"])],
        [True] + [False] * 6, (BRANCH_WIDTH, BF16),
        [pltpu.VMEM((LRU_CHUNKS * T_BLK + SUBLANES, BRANCH_WIDTH), F32), pltpu.VMEM((1, BRANCH_WIDTH), F32)]
        + [pltpu.VMEM((2 * BRANCH_WIDTH // LANES, T_BLK, LANES), F32),
           pltpu.VMEM((T_BLK // SUBLANES, BRANCH_WIDTH), F32)] * LRU_CHUNKS)

    h = _pcall(
        _merge_kernel, "merge", n_tokens, T_STEP,
        [h, xn, ya, yb, yc, yd, win(PACK_MERGE), _LayerWindow(w_br, l), _LayerWindow(w_o, l)],
        [True, True, True, True, True, True, False, False, False], (D_MODEL, F32), [])

    outs = (D_MODEL, F32) if p["final"] else [(D_MODEL, F32), (D_MODEL, BF16)]
    res = _pcall(
        functools.partial(_ffn_kernel, p["final"]), "swiglu", n_tokens, T_STEP,
        [h, _row(p["norm_ffn_w"]), _LayerWindow(w_fi, l), _LayerWindow(w_fo, l), _row(p["next_norm_w"])],
        [True, False, False, False, False], outs, [])
    return (res, None) if p["final"] else tuple(res)


def kernel(x, norm_mix_w, w_in, hgrn_lower_bounds, hgrn_norm_w, ssd_conv_w, ssd_conv_b, ssd_dt_bias, ssd_a_log, ssd_d, ssd_norm_w, gla_gate_w, gla_gate_b, gla_norm_w, lru_conv_w, lru_conv_b, lru_wa, lru_ba, lru_wx, lru_bx, lru_lambda, w_branch, w_out, norm_ffn_w, w_ffn_in, w_ffn_out, norm_f_w):
    bsz, seq, _ = x.shape
    depth = w_in.shape[0]
    assert (bsz * seq) % NORM_ROWS == 0
    lb_all = jnp.cumsum(jax.nn.softmax(hgrn_lower_bounds.astype(F32), axis=0), axis=0)
    lb_all = lb_all - lb_all[0:1]
    wts = (_pack_in_proj(w_in, gla_gate_w), w_branch.astype(BF16), w_out.astype(BF16), w_ffn_in.astype(BF16),
           w_ffn_out.astype(BF16))
    h = x.reshape(bsz * seq, D_MODEL)
    xn = _pcall(_norm_kernel, "norm", bsz * seq, NORM_ROWS, [h, _row(norm_mix_w[0])], [True, False], (D_MODEL, BF16), [])
    for l in range(depth):
        p = dict(
            lb=lb_all[l], hgrn_norm_w=hgrn_norm_w[l],
            ssd_conv_w=ssd_conv_w[l], ssd_conv_b=ssd_conv_b[l], ssd_dt_bias=ssd_dt_bias[l], ssd_a_log=ssd_a_log[l],
            ssd_d=ssd_d[l], ssd_norm_w=ssd_norm_w[l], gla_gate_b=gla_gate_b[l],
            gla_norm_w=gla_norm_w[l], lru_conv_w=lru_conv_w[l], lru_conv_b=lru_conv_b[l], lru_wa=lru_wa[l],
            lru_ba=lru_ba[l], lru_wx=lru_wx[l], lru_bx=lru_bx[l], lru_lambda=lru_lambda[l],
            norm_ffn_w=norm_ffn_w[l],
            next_norm_w=(norm_f_w if l == depth - 1 else norm_mix_w[l + 1]), final=(l == depth - 1))
        h, xn = _layer(h, xn, seq, l, wts, p)
    return h.reshape(bsz, seq, D_MODEL)
```

```python
import functools
import math

import jax
import jax.numpy as jnp
import numpy as np
from jax import lax
from jax.experimental import pallas as pl
from jax.experimental.pallas import tpu as pltpu

F32 = jnp.float32
BF16 = jnp.bfloat16

D_MODEL = 1024
BRANCH_WIDTH = 512
CONV_WIDTH = 4
NORM_EPS = 1e-6
SSD_HEADS = 8
SSD_HEADDIM = 64
SSD_GROUPS = 2
SSD_STATE = 64
SSD_CONV_DIM = 768
GLA_KDIM = 256
GLA_HEAD_K = 64
GLA_GATE_NORMALIZER = 16.0
LRU_BLOCKS = 8
LRU_C = 8.0
D_FF = 2816

LANES = 128
SUBLANES = 8
T_BLK = 256
T_STEP = 2 * T_BLK
HGRN_CHUNKS, GLA_CHUNKS, SSD_CHUNKS, LRU_CHUNKS = 2, 4, 4, 4
NORM_ROWS = 2048
HEAD_V = 128
VMEM_LIMIT = 48 * 1024 * 1024
F32_FLOOR = 1e-37
SSD_PROJ = BRANCH_WIDTH + SSD_CONV_DIM + LANES
GLA_PROJ = 3 * GLA_KDIM + 2 * BRANCH_WIDTH


def _dot(a, b):
    return jnp.dot(a, b, preferred_element_type=F32)


def _dot_nt(a, b):
    return lax.dot_general(a, b, (((1,), (1,)), ((), ())), preferred_element_type=F32)


def _dot_tn(a, b):
    return lax.dot_general(a, b, (((0,), (0,)), ((), ())), preferred_element_type=F32)


def _split3(x):
    hi = x.astype(BF16)
    r1 = x - hi.astype(F32)
    mid = r1.astype(BF16)
    lo = (r1 - mid.astype(F32)).astype(BF16)
    return hi, mid, lo


def _sel_dot(mat, x):
    hi, mid, lo = _split3(x)
    return (_dot(mat, lo) + _dot(mat, mid)) + _dot(mat, hi)


def _dot_sel(x, mat):
    hi, mid, lo = _split3(x)
    return (_dot(lo, mat) + _dot(mid, mat)) + _dot(hi, mat)


def _sigmoid(x):
    return jax.nn.sigmoid(x)


def _silu(x):
    return x * _sigmoid(x)


def _softplus(x):
    return jnp.maximum(x, 0.0) + jnp.log1p(jnp.exp(-jnp.abs(x)))


def _log_sigmoid(x):
    return jnp.minimum(x, 0.0) - jnp.log1p(jnp.exp(-jnp.abs(x)))


def _rmsnorm(x, w):
    return x * lax.rsqrt(jnp.mean(x * x, axis=-1, keepdims=True) + NORM_EPS) * w


GLA_SMALL = (1, 2, 4)
GLA_MID = (8, 16, 32, 64)
GLA_PARTS = 2 + len(GLA_SMALL) + len(GLA_MID) + 1
LOG2E = 1.4426950408889634
GLA_SPAN_LIMIT = 48.0
HGRN_BASE = 32
GLA_BASE = 128


def _gla_part_ids():
    row = np.arange(LANES)[:, None]
    col = np.arange(LANES)[None, :]
    x = row ^ col
    ids = np.where(x == 0, 0, np.floor(np.log2(np.maximum(x, 1))).astype(np.int64) + 1)
    return jnp.asarray(np.where(col <= row, ids, -1), jnp.int32)


def _gla_decay_matrix(t):
    u = np.arange(t)[None, :]
    r = np.arange(t)[:, None]
    blocks = [u <= r]
    for s in GLA_SMALL:
        mid = (r // (2 * s)) * (2 * s) + s - 1
        blocks.append(((u > mid) & (u <= r)) | ((u > r) & (u <= mid)))
    m = np.concatenate(blocks, axis=0)
    return jnp.asarray(np.concatenate([m, m], axis=1), BF16)


def _store_decay_pieces(rhs_ref, logf):
    t = logf.shape[0]
    lf2 = logf * LOG2E
    hi = lf2.astype(BF16)
    rhs_ref[0:t, :] = hi
    rhs_ref[t:2 * t, :] = (lf2 - hi.astype(F32)).astype(BF16)


def _gla_chunk(q, k, logf, v_ref, st_ref, ql_ref, kl_ref, rhs_ref, dm_ref, ids_ref, hpt, base, emit):
    t, nk = q.shape
    assert t == 2 * LANES
    ntile = nk // LANES
    dk = LANES // hpt
    levels = GLA_SMALL + GLA_MID

    _store_decay_pieces(rhs_ref, logf)
    bcum = _dot(dm_ref[0:t, :], rhs_ref[...])
    blast = bcum[t - 1:t, :]
    qb = q.astype(BF16)
    kb = k.astype(BF16)
    ql_ref[0] = qb * jnp.exp2(bcum).astype(BF16)
    kl_ref[0] = kb * jnp.exp2(blast - bcum).astype(BF16)

    def level_part(part, s):
        bc3 = bcum.reshape(t // (2 * s), 2 * s, nk)
        ref = bc3[:, s - 1:s, :]
        e = jnp.exp2(jnp.concatenate([ref - bc3[:, 0:s, :], bc3[:, s:2 * s, :] - ref], axis=1))
        e = e.reshape(t, nk).astype(BF16)
        ql_ref[part] = qb * e
        kl_ref[part] = kb * e

    part_of = {}
    part = 1
    for s in levels + (LANES,):
        if s >= base:
            level_part(part, s)
            part_of[s] = part
            part += 1
    free = part
    p_top = part_of[LANES]

    bcb = bcum.reshape(t // base, base, nk)
    span = bcb - bcb[:, base // 2 - 1:base // 2, :]
    fits = jnp.max(jnp.abs(span)) <= GLA_SPAN_LIMIT

    ids = ids_ref[...]
    lane = lax.broadcasted_iota(jnp.int32, (1, LANES), 1)

    def own(x, sub):
        return x if hpt == 1 else jnp.where((lane // dk) == sub, x, jnp.zeros_like(x))

    def stacked(x):
        return x if hpt == 1 else jnp.concatenate([own(x, sub) for sub in range(hpt)], axis=0)

    ids_st = ids if hpt == 1 else jnp.concatenate([ids] * hpt, axis=0)

    def finish(first, overrides):
        intra = []
        for tau in range(ntile):
            tl = slice(tau * LANES, (tau + 1) * LANES)
            diag = []
            for c in range(2):
                rows = slice(c * LANES, (c + 1) * LANES)
                sc = jnp.where(ids_st >= 0, _dot_nt(stacked(ql_ref[first, rows, tl]), kl_ref[first, rows, tl]), 0.0)
                for pid, p in overrides:
                    sc = jnp.where(ids_st == pid, _dot_nt(stacked(ql_ref[p, rows, tl]), kl_ref[p, rows, tl]), sc)
                diag.append(sc.astype(BF16))
            low = _dot_nt(stacked(ql_ref[p_top, LANES:t, tl]), kl_ref[p_top, 0:LANES, tl]).astype(BF16)
            for sub in range(hpt):
                hv = slice((tau * hpt + sub) * HEAD_V, (tau * hpt + sub + 1) * HEAD_V)
                hr = slice(sub * LANES, (sub + 1) * LANES)
                o_top = _dot(diag[0][hr], v_ref[0:LANES, hv])
                o_bot = _dot(jnp.concatenate([low[hr], diag[1][hr]], axis=1), v_ref[:, hv])
                intra.append(jnp.concatenate([o_top, o_bot], axis=0))
            yield
        for tau in range(ntile):
            tl = slice(tau * LANES, (tau + 1) * LANES)
            inter = _dot_nt(stacked(ql_ref[0, :, tl]), st_ref[tau].astype(BF16))
            for sub in range(hpt):
                h = tau * hpt + sub
                emit(h, intra[h] + inter[sub * t:(sub + 1) * t])
        for tau in range(ntile):
            tl = slice(tau * LANES, (tau + 1) * LANES)
            kd = kl_ref[0, :, tl]
            upd = _dot_tn(v_ref[:, tau * hpt * HEAD_V:(tau * hpt + 1) * HEAD_V], kd)
            for sub in range(1, hpt):
                h = tau * hpt + sub
                upd = jnp.where((lane // dk) == sub, _dot_tn(v_ref[:, h * HEAD_V:(h + 1) * HEAD_V], kd), upd)
            st_ref[tau] = st_ref[tau] * jnp.exp2(blast[:, tl]) + upd

    def part_id(s):
        return 1 + levels.index(s)

    upper = [(part_id(s), part_of[s]) for s in levels if s >= base]

    def one_reference():
        ql_ref[free] = qb * jnp.exp2(span).reshape(t, nk).astype(BF16)
        kl_ref[free] = kb * jnp.exp2(-span).reshape(t, nk).astype(BF16)
        yield from finish(free, upper)

    def safe_parts():
        d3 = _dot(dm_ref[t:(1 + len(GLA_SMALL)) * t, :], rhs_ref[...])
        lower = []
        part = free
        for i, s in enumerate(GLA_SMALL):
            if s < base:
                e = jnp.exp2(d3[i * t:(i + 1) * t]).astype(BF16)
                ql_ref[part] = qb * e
                kl_ref[part] = kb * e
                lower.append((part_id(s), part))
                part += 1
        for s in GLA_MID:
            if s < base:
                level_part(part, s)
                lower.append((part_id(s), part))
                part += 1
        ql_ref[part] = qb
        kl_ref[part] = kb
        yield from finish(part, lower + upper)

    return fits, one_reference, safe_parts


def _interleave(chains):
    live = list(chains)
    while live:
        for c in list(live):
            try:
                next(c)
            except StopIteration:
                live.remove(c)


def _gla_finish(chunks):
    fits = functools.reduce(jnp.logical_and, [c[0] for c in chunks])

    @pl.when(fits)
    def _():
        _interleave([c[1]() for c in chunks])

    @pl.when(jnp.logical_not(fits))
    def _():
        _interleave([c[2]() for c in chunks])


def _causal_conv(buf_ref, x, w_ref, b_ref, base):
    t = x.shape[0]
    buf_ref[SUBLANES + base:SUBLANES + base + t, :] = x
    acc = b_ref[...] + w_ref[CONV_WIDTH - 1:CONV_WIDTH, :] * x
    for s in range(1, CONV_WIDTH):
        acc = acc + w_ref[CONV_WIDTH - 1 - s:CONV_WIDTH - s, :] * buf_ref[pl.ds(SUBLANES + base - s, t), :]
    return acc


def _step_start(bps, nb, *carry_refs):
    @pl.when((pl.program_id(0) * nb) % bps == 0)
    def _():
        for ref in carry_refs:
            ref[...] = jnp.zeros_like(ref)


def _conv_carry(buf_ref):
    rows = buf_ref.shape[0] - SUBLANES
    buf_ref[0:SUBLANES, :] = buf_ref[rows:rows + SUBLANES, :]


def _hgrn_kernel(bps, xn_ref, w_ref, lb_ref, oml_ref, hnw_ref, dm_ref, ids_ref,
                 o_ref, st_ref, *scratch):
    nb = xn_ref.shape[0] // T_BLK
    _step_start(bps, nb, st_ref)
    w = BRANCH_WIDTH
    staged = []
    for blk in range(nb):
        rows = slice(blk * T_BLK, (blk + 1) * T_BLK)
        ql_ref, kl_ref, rhs_ref, v_ref = scratch[4 * blk:4 * blk + 4]
        proj = _dot(xn_ref[rows, :], w_ref[...])
        fp = proj[:, w:2 * w]
        u = jnp.exp(-jnp.abs(fp))
        r = 1.0 / (1.0 + u)
        pos = fp >= 0.0
        sig = jnp.where(pos, r, u * r)
        key = oml_ref[...] * jnp.where(pos, u * r, r)
        logf = jnp.log(jnp.maximum(lb_ref[...] + oml_ref[...] * sig, F32_FLOOR))
        qf = _silu(proj[:, 0:w])
        v_ref[...] = proj[:, 2 * w:3 * w].astype(BF16)
        staged.append((qf, key, logf, _silu(proj[:, 3 * w:4 * w])))
    chunks = []
    for blk in range(nb):
        rows = slice(blk * T_BLK, (blk + 1) * T_BLK)
        ql_ref, kl_ref, rhs_ref, v_ref = scratch[4 * blk:4 * blk + 4]
        qf, key, logf, gs = staged[blk]

        def emit(h, o, rows=rows, gs=gs):
            cols = slice(h * HEAD_V, (h + 1) * HEAD_V)
            o_ref[rows, cols] = (_rmsnorm(o, hnw_ref[...]) * gs[:, cols]).astype(o_ref.dtype)

        chunks.append(_gla_chunk(qf, key, logf, v_ref, st_ref, ql_ref, kl_ref, rhs_ref, dm_ref, ids_ref, 1,
                                 HGRN_BASE, emit))
    _gla_finish(chunks)


def _gla_kernel(bps, xn_ref, w_ref, gb_ref, gnw_ref, dm_ref, ids_ref, o_ref, st_ref, *scratch):
    nb = xn_ref.shape[0] // T_BLK
    _step_start(bps, nb, st_ref)
    kd, w = GLA_KDIM, BRANCH_WIDTH
    chunks = []
    for blk in range(nb):
        rows = slice(blk * T_BLK, (blk + 1) * T_BLK)
        ql_ref, kl_ref, rhs_ref, v_ref = scratch[4 * blk:4 * blk + 4]
        proj = _dot(xn_ref[rows, :], w_ref[:, 0:GLA_PROJ])
        logf = _log_sigmoid(proj[:, 0:kd] + gb_ref[...]) * (1.0 / GLA_GATE_NORMALIZER)
        proj = proj[:, kd:]
        q = proj[:, 0:kd] * (GLA_HEAD_K ** -0.5)
        k = proj[:, kd:2 * kd]
        v_ref[...] = proj[:, 2 * kd:2 * kd + w].astype(BF16)
        gs = _silu(proj[:, 2 * kd + w:2 * kd + 2 * w])

        def emit(h, o, rows=rows, gs=gs):
            cols = slice(h * HEAD_V, (h + 1) * HEAD_V)
            o_ref[rows, cols] = (_rmsnorm(o, gnw_ref[...]) * gs[:, cols]).astype(o_ref.dtype)

        chunks.append(_gla_chunk(q, k, logf, v_ref, st_ref, ql_ref, kl_ref, rhs_ref, dm_ref, ids_ref, 2,
                                 GLA_BASE, emit))
    _gla_finish(chunks)


def _ssd_kernel(bps, xn_ref, w_ref, cw_ref, cb_ref, dtb_ref, aneg_ref, dsk_ref, snw_ref, tril_ref,
                exp_ref, o_ref, buf_ref, st_ref, *y_refs):
    nb = xn_ref.shape[0] // T_BLK
    _step_start(bps, nb, st_ref, buf_ref)
    for blk in range(nb):
        _ssd_block(blk, xn_ref, w_ref, cw_ref, cb_ref, dtb_ref, aneg_ref, dsk_ref, snw_ref, tril_ref,
                   exp_ref, o_ref, buf_ref, st_ref, y_refs[blk])
    _conv_carry(buf_ref)


def _ssd_block(blk, xn_ref, w_ref, cw_ref, cb_ref, dtb_ref, aneg_ref, dsk_ref, snw_ref, tril_ref,
               exp_ref, o_ref, buf_ref, st_ref, y_ref):
    t = T_BLK
    rows = slice(blk * t, (blk + 1) * t)
    w = BRANCH_WIDTH
    gs = SSD_GROUPS * SSD_STATE
    proj = _dot(xn_ref[rows, :], w_ref[:, 0:SSD_PROJ])
    z = proj[:, 0:w]
    xbc = _silu(_causal_conv(buf_ref, proj[:, w:w + SSD_CONV_DIM], cw_ref, cb_ref, blk * t))
    xs = xbc[:, 0:w]
    bm = xbc[:, w:w + gs].astype(BF16)
    cm = xbc[:, w + gs:w + 2 * gs]
    dt = _softplus(proj[:, w + SSD_CONV_DIM:] + dtb_ref[...])
    acs = _sel_dot(tril_ref[...], dt * aneg_ref[...])
    dt_e = _dot_sel(dt, exp_ref[...])
    acs_e = _dot_sel(acs, exp_ref[...])
    alast_e = acs_e[t - 1:t, :]
    xdt = xs * dt_e
    xdt_bf = xdt.astype(BF16)

    row = lax.broadcasted_iota(jnp.int32, (t, t), 0)
    col = lax.broadcasted_iota(jnp.int32, (t, t), 1)
    causal = col <= row
    lane = lax.broadcasted_iota(jnp.int32, (1, LANES), 1)
    acs_t = acs.T
    gmats = []
    for g in range(SSD_GROUPS):
        cg = jnp.where((lane // SSD_STATE) == g, cm, 0.0).astype(BF16)
        gmats.append(_dot_nt(cg, bm))
    hpg = SSD_HEADS // SSD_GROUPS
    for pair in range(SSD_HEADS // 2):
        tl = slice(pair * LANES, (pair + 1) * LANES)
        ys = []
        for sub in range(2):
            h = 2 * pair + sub
            seg = acs[:, h:h + 1] - acs_t[h:h + 1, :]
            m = jnp.where(causal, gmats[h // hpg] * jnp.exp2(seg), 0.0).astype(BF16)
            ys.append(_dot(m, xdt_bf[:, tl]))
        y_diag = jnp.where(lane < SSD_HEADDIM, ys[0], ys[1])
        y_off = _dot(cm.astype(BF16), st_ref[:, tl].astype(BF16)) * jnp.exp2(acs_e[:, tl])
        y = y_diag + y_off + xs[:, tl] * dsk_ref[:, tl]
        y_ref[:, tl] = y * _silu(z[:, tl])
    xdec = (xdt * jnp.exp2(alast_e - acs_e)).astype(BF16)
    upd = _dot_tn(bm, xdec)
    srow = lax.broadcasted_iota(jnp.int32, (gs, w), 0) // SSD_STATE
    scol = lax.broadcasted_iota(jnp.int32, (gs, w), 1) // (hpg * SSD_HEADDIM)
    st_ref[...] = st_ref[...] * jnp.exp2(alast_e) + jnp.where(srow == scol, upd, 0.0)
    gw = w // SSD_GROUPS
    for g in range(SSD_GROUPS):
        yg = y_ref[:, g * gw:(g + 1) * gw]
        o_ref[rows, g * gw:(g + 1) * gw] = (
            yg * lax.rsqrt(jnp.mean(yg * yg, axis=-1, keepdims=True) + NORM_EPS) * snw_ref[:, g * gw:(g + 1) * gw]
        ).astype(o_ref.dtype)


def _sqrt_one_minus_exp2x(x):
    th = jnp.tanh(x)
    y = (-2.0 * th) / (1.0 - th)
    return y * lax.rsqrt(jnp.maximum(y, 1e-30))


def _doubling_scan(a, b, pos, n, axis):
    off = 1
    while off < n:
        keep = pos >= off
        a_sh = jnp.where(keep, pltpu.roll(a, off, axis), 1.0)
        b_sh = jnp.where(keep, pltpu.roll(b, off, axis), 0.0)
        b = a * b_sh + b
        a = a * a_sh
        off *= 2
    return a, b


def _scan_rows(a, b, hc_ref, ab_ref, cg_ref):
    t, n = a.shape
    g = t // SUBLANES
    nt = n // LANES
    a8, b8 = _doubling_scan(a.reshape(g, SUBLANES, n), b.reshape(g, SUBLANES, n),
                            lax.broadcasted_iota(jnp.int32, (1, SUBLANES, 1), 1), SUBLANES, 1)
    a8 = a8.reshape(t, n)
    b8 = b8.reshape(t, n)
    for i in range(nt):
        ab_ref[i] = a8[:, i * LANES:(i + 1) * LANES]
        ab_ref[nt + i] = b8[:, i * LANES:(i + 1) * LANES]
    last = pl.ds(SUBLANES - 1, g, stride=SUBLANES)
    gidx = lax.broadcasted_iota(jnp.int32, (g, 1), 0)
    ag = jnp.concatenate([ab_ref[i, last, :] for i in range(nt)], axis=1)
    bg = jnp.concatenate([ab_ref[nt + i, last, :] for i in range(nt)], axis=1)
    ag, bg = _doubling_scan(ag, bg, gidx, g, 0)
    state = bg + ag * hc_ref[...]
    cg_ref[...] = jnp.where(gidx >= 1, pltpu.roll(state, 1, 0), hc_ref[...])
    hc_ref[...] = state[g - 1:g, :]
    out = []
    for i in range(nt):
        cg = cg_ref[:, i * LANES:(i + 1) * LANES].reshape(g, 1, LANES)
        hi = ab_ref[nt + i].reshape(g, SUBLANES, LANES) + ab_ref[i].reshape(g, SUBLANES, LANES) * cg
        out.append(hi.reshape(t, LANES))
    return jnp.concatenate(out, axis=1)


def _lru_kernel(bps, xn_ref, w_ref, cw_ref, cb_ref, wg_ref, bg_ref, lam_ref, o_ref, buf_ref, hc_ref,
                *scratch):
    nb = xn_ref.shape[0] // T_BLK
    _step_start(bps, nb, hc_ref, buf_ref)
    for blk in range(nb):
        _lru_block(blk, xn_ref, w_ref, cw_ref, cb_ref, wg_ref, bg_ref, lam_ref, o_ref, buf_ref, hc_ref,
                   *scratch[2 * blk:2 * blk + 2])
    _conv_carry(buf_ref)


def _lru_block(blk, xn_ref, w_ref, cw_ref, cb_ref, wg_ref, bg_ref, lam_ref, o_ref, buf_ref, hc_ref,
               ab_ref, cg_ref):
    t = T_BLK
    rows = slice(blk * t, (blk + 1) * t)
    w = BRANCH_WIDTH
    proj = _dot(xn_ref[rows, :], w_ref[...])
    u = _causal_conv(buf_ref, proj[:, 0:w], cw_ref, cb_ref, blk * t)
    rg = _dot(u.astype(BF16), wg_ref[...]) + bg_ref[...]
    r = _sigmoid(rg[:, 0:w])
    ig = _sigmoid(rg[:, w:2 * w])
    log_a = (-LRU_C) * r * _softplus(-lam_ref[...])
    a = jnp.exp(log_a)
    b = _sqrt_one_minus_exp2x(log_a) * (ig * u)
    hseq = _scan_rows(a, b, hc_ref, ab_ref, cg_ref)
    gate = proj[:, w:2 * w]
    gelu = 0.5 * gate * (1.0 + jnp.tanh(math.sqrt(2.0 / math.pi) * (gate + 0.044715 * gate * gate * gate)))
    o_ref[rows, :] = (hseq * gelu).astype(o_ref.dtype)


def _merge_kernel(h_ref, xn_ref, ya_ref, yb_ref, yc_ref, yd_ref, wmg_ref, wbr_ref, wout_ref, o_ref):
    xn = xn_ref[...]
    merged = None
    for n, y_ref in enumerate((ya_ref, yb_ref, yc_ref, yd_ref)):
        gate = _sigmoid(_dot(xn, wmg_ref[:, n * D_MODEL:(n + 1) * D_MODEL]))
        term = gate * _dot(y_ref[...], wbr_ref[n])
        merged = term if merged is None else merged + term
    o_ref[...] = h_ref[...] + _dot(merged.astype(BF16), wout_ref[...])


def _ffn_kernel(final, h_ref, nw_ref, win_ref, wout_ref, nnw_ref, o_ref, *xn_out):
    hres = h_ref[...]
    xn = _rmsnorm(hres, nw_ref[...]).astype(BF16)
    gu = _dot(xn, win_ref[...])
    act = (_silu(gu[:, 0:D_FF]) * gu[:, D_FF:2 * D_FF]).astype(BF16)
    out = hres + _dot(act, wout_ref[...])
    if final:
        o_ref[...] = _rmsnorm(out, nnw_ref[...])
    else:
        o_ref[...] = out
        xn_out[0][...] = _rmsnorm(out, nnw_ref[...]).astype(BF16)


def _norm_kernel(x_ref, nw_ref, o_ref):
    o_ref[...] = _rmsnorm(x_ref[...], nw_ref[...]).astype(o_ref.dtype)


def _const_spec(shape):
    nd = len(shape)
    return pl.BlockSpec(shape, lambda i, _nd=nd: (0,) * _nd, pipeline_mode=pl.Buffered(1))


def _row_spec(rows, width):
    return pl.BlockSpec((rows, width), lambda i: (i, 0))


class _LayerWindow:
    def __init__(self, array, layer, width=None, col=0):
        self.array, self.layer, self.width, self.col = array, layer, width or array.shape[-1], col

    def spec(self):
        shape = self.array.shape
        mid = (0,) * (len(shape) - 2)
        layer, col = self.layer, self.col
        return pl.BlockSpec((None,) + shape[1:-1] + (self.width,), lambda i: (layer,) + mid + (col,),
                            pipeline_mode=pl.Buffered(1))


def _pcall(kernel, name, n_tokens, rows, operands, row_flags, outs, scratch):
    in_specs = [a.spec() if isinstance(a, _LayerWindow) else
                (_row_spec(rows, a.shape[1]) if is_row else _const_spec(a.shape))
                for a, is_row in zip(operands, row_flags)]
    operands = [a.array if isinstance(a, _LayerWindow) else a for a in operands]
    many = isinstance(outs, list)
    out_specs = [_row_spec(rows, wd) for wd, _ in (outs if many else [outs])]
    out_shape = [jax.ShapeDtypeStruct((n_tokens, wd), dt) for wd, dt in (outs if many else [outs])]
    return pl.pallas_call(
        kernel,
        name=name,
        grid=(n_tokens // rows,),
        in_specs=in_specs,
        out_specs=out_specs if many else out_specs[0],
        out_shape=out_shape if many else out_shape[0],
        scratch_shapes=scratch,
        compiler_params=pltpu.CompilerParams(dimension_semantics=("arbitrary",), vmem_limit_bytes=VMEM_LIMIT),
    )(*operands)


def _tril_const(t):
    return jnp.asarray(np.tril(np.ones((t, t), np.float32)), BF16)


def _gla_scratch(nk, nb):
    ntile = nk // LANES
    per_block = [
        pltpu.VMEM((GLA_PARTS, T_BLK, nk), BF16),
        pltpu.VMEM((GLA_PARTS, T_BLK, nk), BF16),
        pltpu.VMEM((2 * T_BLK, nk), BF16),
        pltpu.VMEM((T_BLK, BRANCH_WIDTH), BF16),
    ]
    return [pltpu.VMEM((ntile, HEAD_V, LANES), F32)] + per_block * nb


def _row(x):
    return x.reshape(1, -1).astype(F32)


PACK_MERGE = (4096, 0)
PACK_HGRN = (2048, 2)
PACK_SSD = (2048, 3)
PACK_GLA = (2048, 4)
PACK_LRU = (1024, 10)


PACK_COLS = 11 * 1024
PACK_ROWS = 128
IN_COLS = [0] + np.cumsum([512, 512, 512, 512, 512, 768, 8, 256, 256, 512, 512, 16, 512, 512, 4096]).tolist()


def _pack_kernel(w_ref, wg_ref, o_ref):
    c = IN_COLS

    def put(dst, lo, hi):
        o_ref[:, dst:dst + hi - lo] = w_ref[:, lo:hi].astype(BF16)
        return dst + hi - lo

    def pad(lo, hi):
        o_ref[:, lo:hi] = jnp.zeros((o_ref.shape[0], hi - lo), BF16)

    put(PACK_MERGE[0] * PACK_MERGE[1], c[14], c[15])
    put(PACK_HGRN[0] * PACK_HGRN[1], c[0], c[4])
    start = PACK_SSD[0] * PACK_SSD[1]
    pad(put(start, c[4], c[7]), start + PACK_SSD[0])
    start = PACK_GLA[0] * PACK_GLA[1]
    o_ref[:, start:start + GLA_KDIM] = wg_ref[...].astype(BF16)
    pad(put(start + GLA_KDIM, c[7], c[11]), start + PACK_GLA[0])
    put(PACK_LRU[0] * PACK_LRU[1], c[12], c[14])


def _pack_in_proj(w_in, gla_gate_w):
    depth, rows, cols = w_in.shape
    w_gate = jnp.einsum("lkr,lrn->lkn", w_in[:, :, IN_COLS[11]:IN_COLS[12]].astype(F32), gla_gate_w.astype(F32),
                        precision=lax.Precision.HIGHEST)
    cols = pl.cdiv(cols, LANES) * LANES
    w_in = jnp.pad(w_in.astype(BF16), ((0, 0), (0, 0), (0, cols - w_in.shape[2])))
    return pl.pallas_call(
        _pack_kernel,
        name="pack",
        grid=(depth, rows // PACK_ROWS),
        in_specs=[pl.BlockSpec((None, PACK_ROWS, cols), lambda l, i: (l, i, 0)),
                  pl.BlockSpec((None, PACK_ROWS, GLA_KDIM), lambda l, i: (l, i, 0))],
        out_specs=pl.BlockSpec((None, PACK_ROWS, PACK_COLS), lambda l, i: (l, i, 0)),
        out_shape=jax.ShapeDtypeStruct((depth, rows, PACK_COLS), BF16),
        compiler_params=pltpu.CompilerParams(dimension_semantics=("arbitrary", "arbitrary"),
                                             vmem_limit_bytes=VMEM_LIMIT),
    )(w_in, w_gate)


def _layer(h, xn, seq, l, wts, p):
    n_tokens = h.shape[0]
    bps = seq // T_BLK
    assert n_tokens % seq == 0 and seq % T_STEP == 0
    assert all(seq % (nb * T_BLK) == 0 for nb in (HGRN_CHUNKS, GLA_CHUNKS, SSD_CHUNKS, LRU_CHUNKS))
    tril = _tril_const(T_BLK)
    dmat = _gla_decay_matrix(T_BLK)
    part_ids = _gla_part_ids()
    w_all, w_br, w_o, w_fi, w_fo = wts
    win = lambda pack: _LayerWindow(w_all, l, *pack)

    lb = p["lb"].astype(F32)
    ya = _pcall(
        functools.partial(_hgrn_kernel, bps), "hgrn2", n_tokens, HGRN_CHUNKS * T_BLK,
        [xn, win(PACK_HGRN), _row(lb), _row(1.0 - lb), _row(p["hgrn_norm_w"]), dmat, part_ids],
        [True] + [False] * 6, (BRANCH_WIDTH, BF16), _gla_scratch(BRANCH_WIDTH, HGRN_CHUNKS))

    pad8 = lambda v: jnp.concatenate([v.astype(F32), jnp.zeros((LANES - SSD_HEADS,), F32)]).reshape(1, LANES)
    expand = jnp.asarray(np.arange(LANES)[:, None] == (np.arange(BRANCH_WIDTH)[None, :] // SSD_HEADDIM), BF16)
    yb = _pcall(
        functools.partial(_ssd_kernel, bps), "ssd", n_tokens, SSD_CHUNKS * T_BLK,
        [xn, win(PACK_SSD), p["ssd_conv_w"].astype(F32), _row(p["ssd_conv_b"]), pad8(p["ssd_dt_bias"]),
         pad8(-LOG2E * jnp.exp(p["ssd_a_log"].astype(F32))), _row(jnp.repeat(p["ssd_d"], SSD_HEADDIM)), _row(p["ssd_norm_w"]),
         tril, expand],
        [True] + [False] * 9, (BRANCH_WIDTH, BF16),
        [pltpu.VMEM((SSD_CHUNKS * T_BLK + SUBLANES, SSD_CONV_DIM), F32),
         pltpu.VMEM((SSD_GROUPS * SSD_STATE, BRANCH_WIDTH), F32)]
        + [pltpu.VMEM((T_BLK, BRANCH_WIDTH), F32)] * SSD_CHUNKS)

    yc = _pcall(
        functools.partial(_gla_kernel, bps), "gla", n_tokens, GLA_CHUNKS * T_BLK,
        [xn, win(PACK_GLA), _row(p["gla_gate_b"]), _row(p["gla_norm_w"]), dmat, part_ids],
        [True] + [False] * 5, (BRANCH_WIDTH, BF16), _gla_scratch(GLA_KDIM, GLA_CHUNKS))

    eye = jnp.eye(LRU_BLOCKS, dtype=F32)
    bdiag = lambda wt: jnp.einsum("kde,kl->kdle", wt.astype(F32), eye).reshape(BRANCH_WIDTH, BRANCH_WIDTH)
    w_gates = jnp.concatenate([bdiag(p["lru_wa"]), bdiag(p["lru_wx"])], axis=1).astype(BF16)
    b_gates = _row(jnp.concatenate([p["lru_ba"], p["lru_bx"]]))
    yd = _pcall(
        functools.partial(_lru_kernel, bps), "rglru", n_tokens, LRU_CHUNKS * T_BLK,
        [xn, win(PACK_LRU), p["lru_conv_w"].astype(F32), _row(p["lru_conv_b"]), w_gates,
         b_gates, _row(p["lru_lambda"])],
        [True] + [False] * 6, (BRANCH_WIDTH, BF16),
        [pltpu.VMEM((LRU_CHUNKS * T_BLK + SUBLANES, BRANCH_WIDTH), F32), pltpu.VMEM((1, BRANCH_WIDTH), F32)]
        + [pltpu.VMEM((2 * BRANCH_WIDTH // LANES, T_BLK, LANES), F32),
           pltpu.VMEM((T_BLK // SUBLANES, BRANCH_WIDTH), F32)] * LRU_CHUNKS)

    h = _pcall(
        _merge_kernel, "merge", n_tokens, T_STEP,
        [h, xn, ya, yb, yc, yd, win(PACK_MERGE), _LayerWindow(w_br, l), _LayerWindow(w_o, l)],
        [True, True, True, True, True, True, False, False, False], (D_MODEL, F32), [])

    outs = (D_MODEL, F32) if p["final"] else [(D_MODEL, F32), (D_MODEL, BF16)]
    res = _pcall(
        functools.partial(_ffn_kernel, p["final"]), "swiglu", n_tokens, T_STEP,
        [h, _row(p["norm_ffn_w"]), _LayerWindow(w_fi, l), _LayerWindow(w_fo, l), _row(p["next_norm_w"])],
        [True, False, False, False, False], outs, [])
    return (res, None) if p["final"] else tuple(res)


def kernel(x, norm_mix_w, w_in, hgrn_lower_bounds, hgrn_norm_w, ssd_conv_w, ssd_conv_b, ssd_dt_bias, ssd_a_log, ssd_d, ssd_norm_w, gla_gate_w, gla_gate_b, gla_norm_w, lru_conv_w, lru_conv_b, lru_wa, lru_ba, lru_wx, lru_bx, lru_lambda, w_branch, w_out, norm_ffn_w, w_ffn_in, w_ffn_out, norm_f_w):
    bsz, seq, _ = x.shape
    depth = w_in.shape[0]
    assert (bsz * seq) % NORM_ROWS == 0
    lb_all = jnp.cumsum(jax.nn.softmax(hgrn_lower_bounds.astype(F32), axis=0), axis=0)
    lb_all = lb_all - lb_all[0:1]
    wts = (_pack_in_proj(w_in, gla_gate_w), w_branch.astype(BF16), w_out.astype(BF16), w_ffn_in.astype(BF16),
           w_ffn_out.astype(BF16))
    h = x.reshape(bsz * seq, D_MODEL)
    xn = _pcall(_norm_kernel, "norm", bsz * seq, NORM_ROWS, [h, _row(norm_mix_w[0])], [True, False], (D_MODEL, BF16), [])
    for l in range(depth):
        p = dict(
            lb=lb_all[l], hgrn_norm_w=hgrn_norm_w[l],
            ssd_conv_w=ssd_conv_w[l], ssd_conv_b=ssd_conv_b[l], ssd_dt_bias=ssd_dt_bias[l], ssd_a_log=ssd_a_log[l],
            ssd_d=ssd_d[l], ssd_norm_w=ssd_norm_w[l], gla_gate_b=gla_gate_b[l],
            gla_norm_w=gla_norm_w[l], lru_conv_w=lru_conv_w[l], lru_conv_b=lru_conv_b[l], lru_wa=lru_wa[l],
            lru_ba=lru_ba[l], lru_wx=lru_wx[l], lru_bx=lru_bx[l], lru_lambda=lru_lambda[l],
            norm_ffn_w=norm_ffn_w[l],
            next_norm_w=(norm_f_w if l == depth - 1 else norm_mix_w[l + 1]), final=(l == depth - 1))
        h, xn = _layer(h, xn, seq, l, wts, p)
    return h.reshape(bsz, seq, D_MODEL)
```

```python
import functools
import math

import jax
import jax.numpy as jnp
import numpy as np
from jax import lax
from jax.experimental import pallas as pl
from jax.experimental.pallas import tpu as pltpu

F32 = jnp.float32
BF16 = jnp.bfloat16

D_MODEL = 1024
BRANCH_WIDTH = 512
CONV_WIDTH = 4
NORM_EPS = 1e-6
SSD_HEADS = 8
SSD_HEADDIM = 64
SSD_GROUPS = 2
SSD_STATE = 64
SSD_CONV_DIM = 768
GLA_KDIM = 256
GLA_HEAD_K = 64
GLA_GATE_NORMALIZER = 16.0
LRU_BLOCKS = 8
LRU_C = 8.0
D_FF = 2816

LANES = 128
SUBLANES = 8
T_BLK = 256
T_STEP = 2 * T_BLK
HGRN_CHUNKS, GLA_CHUNKS, SSD_CHUNKS, LRU_CHUNKS = 2, 4, 4, 4
NORM_ROWS = 2048
HEAD_V = 128
VMEM_LIMIT = 48 * 1024 * 1024
F32_FLOOR = 1e-37
SSD_PROJ = BRANCH_WIDTH + SSD_CONV_DIM + LANES
GLA_PROJ = 3 * GLA_KDIM + 2 * BRANCH_WIDTH


def _dot(a, b):
    return jnp.dot(a, b, preferred_element_type=F32)


def _dot_nt(a, b):
    return lax.dot_general(a, b, (((1,), (1,)), ((), ())), preferred_element_type=F32)


def _dot_tn(a, b):
    return lax.dot_general(a, b, (((0,), (0,)), ((), ())), preferred_element_type=F32)


def _split3(x):
    hi = x.astype(BF16)
    r1 = x - hi.astype(F32)
    mid = r1.astype(BF16)
    lo = (r1 - mid.astype(F32)).astype(BF16)
    return hi, mid, lo


def _sel_dot(mat, x):
    hi, mid, lo = _split3(x)
    return (_dot(mat, lo) + _dot(mat, mid)) + _dot(mat, hi)


def _dot_sel(x, mat):
    hi, mid, lo = _split3(x)
    return (_dot(lo, mat) + _dot(mid, mat)) + _dot(hi, mat)


def _sigmoid(x):
    return jax.nn.sigmoid(x)


def _silu(x):
    return x * _sigmoid(x)


def _softplus(x):
    return jnp.maximum(x, 0.0) + jnp.log1p(jnp.exp(-jnp.abs(x)))


def _log_sigmoid(x):
    return jnp.minimum(x, 0.0) - jnp.log1p(jnp.exp(-jnp.abs(x)))


def _rmsnorm(x, w):
    return x * lax.rsqrt(jnp.mean(x * x, axis=-1, keepdims=True) + NORM_EPS) * w


GLA_SMALL = (1, 2, 4)
GLA_MID = (8, 16, 32, 64)
GLA_PARTS = 2 + len(GLA_SMALL) + len(GLA_MID) + 1
LOG2E = 1.4426950408889634
GLA_SPAN_LIMIT = 62.0
HGRN_BASE = 64
GLA_BASE = 128


def _gla_part_ids():
    row = np.arange(LANES)[:, None]
    col = np.arange(LANES)[None, :]
    x = row ^ col
    ids = np.where(x == 0, 0, np.floor(np.log2(np.maximum(x, 1))).astype(np.int64) + 1)
    return jnp.asarray(np.where(col <= row, ids, -1), jnp.int32)


def _gla_decay_matrix(t):
    u = np.arange(t)[None, :]
    r = np.arange(t)[:, None]
    blocks = [u <= r]
    for s in GLA_SMALL:
        mid = (r // (2 * s)) * (2 * s) + s - 1
        blocks.append(((u > mid) & (u <= r)) | ((u > r) & (u <= mid)))
    m = np.concatenate(blocks, axis=0)
    return jnp.asarray(np.concatenate([m, m], axis=1), BF16)


def _store_decay_pieces(rhs_ref, logf):
    t = logf.shape[0]
    lf2 = logf * LOG2E
    hi = lf2.astype(BF16)
    rhs_ref[0:t, :] = hi
    rhs_ref[t:2 * t, :] = (lf2 - hi.astype(F32)).astype(BF16)


def _gla_chunk(q, k, logf, v_ref, st_ref, ql_ref, kl_ref, rhs_ref, dm_ref, ids_ref, hpt, base, emit):
    t, nk = q.shape
    assert t == 2 * LANES
    ntile = nk // LANES
    dk = LANES // hpt
    levels = GLA_SMALL + GLA_MID

    _store_decay_pieces(rhs_ref, logf)
    bcum = _dot(dm_ref[0:t, :], rhs_ref[...])
    blast = bcum[t - 1:t, :]
    qb = q.astype(BF16)
    kb = k.astype(BF16)
    ql_ref[0] = qb * jnp.exp2(bcum).astype(BF16)
    kl_ref[0] = kb * jnp.exp2(blast - bcum).astype(BF16)

    def level_part(part, s):
        bc3 = bcum.reshape(t // (2 * s), 2 * s, nk)
        ref = bc3[:, s - 1:s, :]
        e = jnp.exp2(jnp.concatenate([ref - bc3[:, 0:s, :], bc3[:, s:2 * s, :] - ref], axis=1))
        e = e.reshape(t, nk).astype(BF16)
        ql_ref[part] = qb * e
        kl_ref[part] = kb * e

    part_of = {}
    part = 1
    for s in levels + (LANES,):
        if s >= base:
            level_part(part, s)
            part_of[s] = part
            part += 1
    free = part
    p_top = part_of[LANES]

    bcb = bcum.reshape(t // base, base, nk)
    span = bcb - bcb[:, base // 2 - 1:base // 2, :]
    fits = jnp.max(jnp.abs(span)) <= GLA_SPAN_LIMIT

    ids = ids_ref[...]
    lane = lax.broadcasted_iota(jnp.int32, (1, LANES), 1)

    def own(x, sub):
        return x if hpt == 1 else jnp.where((lane // dk) == sub, x, jnp.zeros_like(x))

    def stacked(x):
        return x if hpt == 1 else jnp.concatenate([own(x, sub) for sub in range(hpt)], axis=0)

    ids_st = ids if hpt == 1 else jnp.concatenate([ids] * hpt, axis=0)

    def finish(first, overrides):
        intra = []
        for tau in range(ntile):
            tl = slice(tau * LANES, (tau + 1) * LANES)
            diag = []
            for c in range(2):
                rows = slice(c * LANES, (c + 1) * LANES)
                sc = jnp.where(ids_st >= 0, _dot_nt(stacked(ql_ref[first, rows, tl]), kl_ref[first, rows, tl]), 0.0)
                for pid, p in overrides:
                    sc = jnp.where(ids_st == pid, _dot_nt(stacked(ql_ref[p, rows, tl]), kl_ref[p, rows, tl]), sc)
                diag.append(sc.astype(BF16))
            low = _dot_nt(stacked(ql_ref[p_top, LANES:t, tl]), kl_ref[p_top, 0:LANES, tl]).astype(BF16)
            for sub in range(hpt):
                hv = slice((tau * hpt + sub) * HEAD_V, (tau * hpt + sub + 1) * HEAD_V)
                hr = slice(sub * LANES, (sub + 1) * LANES)
                o_top = _dot(diag[0][hr], v_ref[0:LANES, hv])
                o_bot = _dot(jnp.concatenate([low[hr], diag[1][hr]], axis=1), v_ref[:, hv])
                intra.append(jnp.concatenate([o_top, o_bot], axis=0))
            yield
        for tau in range(ntile):
            tl = slice(tau * LANES, (tau + 1) * LANES)
            inter = _dot_nt(stacked(ql_ref[0, :, tl]), st_ref[tau].astype(BF16))
            for sub in range(hpt):
                h = tau * hpt + sub
                emit(h, intra[h] + inter[sub * t:(sub + 1) * t])
        for tau in range(ntile):
            tl = slice(tau * LANES, (tau + 1) * LANES)
            kd = kl_ref[0, :, tl]
            upd = _dot_tn(v_ref[:, tau * hpt * HEAD_V:(tau * hpt + 1) * HEAD_V], kd)
            for sub in range(1, hpt):
                h = tau * hpt + sub
                upd = jnp.where((lane // dk) == sub, _dot_tn(v_ref[:, h * HEAD_V:(h + 1) * HEAD_V], kd), upd)
            st_ref[tau] = st_ref[tau] * jnp.exp2(blast[:, tl]) + upd

    def part_id(s):
        return 1 + levels.index(s)

    upper = [(part_id(s), part_of[s]) for s in levels if s >= base]

    def one_reference():
        ql_ref[free] = qb * jnp.exp2(span).reshape(t, nk).astype(BF16)
        kl_ref[free] = kb * jnp.exp2(-span).reshape(t, nk).astype(BF16)
        yield from finish(free, upper)

    def safe_parts():
        d3 = _dot(dm_ref[t:(1 + len(GLA_SMALL)) * t, :], rhs_ref[...])
        lower = []
        part = free
        for i, s in enumerate(GLA_SMALL):
            if s < base:
                e = jnp.exp2(d3[i * t:(i + 1) * t]).astype(BF16)
                ql_ref[part] = qb * e
                kl_ref[part] = kb * e
                lower.append((part_id(s), part))
                part += 1
        for s in GLA_MID:
            if s < base:
                level_part(part, s)
                lower.append((part_id(s), part))
                part += 1
        ql_ref[part] = qb
        kl_ref[part] = kb
        yield from finish(part, lower + upper)

    return fits, one_reference, safe_parts


def _interleave(chains):
    live = list(chains)
    while live:
        for c in list(live):
            try:
                next(c)
            except StopIteration:
                live.remove(c)


def _gla_finish(chunks):
    fits = functools.reduce(jnp.logical_and, [c[0] for c in chunks])

    @pl.when(fits)
    def _():
        _interleave([c[1]() for c in chunks])

    @pl.when(jnp.logical_not(fits))
    def _():
        _interleave([c[2]() for c in chunks])


def _causal_conv(buf_ref, x, w_ref, b_ref, base):
    t = x.shape[0]
    buf_ref[SUBLANES + base:SUBLANES + base + t, :] = x
    acc = b_ref[...] + w_ref[CONV_WIDTH - 1:CONV_WIDTH, :] * x
    for s in range(1, CONV_WIDTH):
        acc = acc + w_ref[CONV_WIDTH - 1 - s:CONV_WIDTH - s, :] * buf_ref[pl.ds(SUBLANES + base - s, t), :]
    return acc


def _step_start(bps, nb, *carry_refs):
    @pl.when((pl.program_id(0) * nb) % bps == 0)
    def _():
        for ref in carry_refs:
            ref[...] = jnp.zeros_like(ref)


def _conv_carry(buf_ref):
    rows = buf_ref.shape[0] - SUBLANES
    buf_ref[0:SUBLANES, :] = buf_ref[rows:rows + SUBLANES, :]


def _hgrn_kernel(bps, xn_ref, w_ref, lb_ref, oml_ref, hnw_ref, dm_ref, ids_ref,
                 o_ref, st_ref, *scratch):
    nb = xn_ref.shape[0] // T_BLK
    _step_start(bps, nb, st_ref)
    w = BRANCH_WIDTH
    staged = []
    for blk in range(nb):
        rows = slice(blk * T_BLK, (blk + 1) * T_BLK)
        ql_ref, kl_ref, rhs_ref, v_ref = scratch[4 * blk:4 * blk + 4]
        proj = _dot(xn_ref[rows, :], w_ref[...])
        fp = proj[:, w:2 * w]
        u = jnp.exp(-jnp.abs(fp))
        r = 1.0 / (1.0 + u)
        pos = fp >= 0.0
        sig = jnp.where(pos, r, u * r)
        key = oml_ref[...] * jnp.where(pos, u * r, r)
        logf = jnp.log(jnp.maximum(lb_ref[...] + oml_ref[...] * sig, F32_FLOOR))
        qf = _silu(proj[:, 0:w])
        v_ref[...] = proj[:, 2 * w:3 * w].astype(BF16)
        staged.append((qf, key, logf, _silu(proj[:, 3 * w:4 * w])))
    chunks = []
    for blk in range(nb):
        rows = slice(blk * T_BLK, (blk + 1) * T_BLK)
        ql_ref, kl_ref, rhs_ref, v_ref = scratch[4 * blk:4 * blk + 4]
        qf, key, logf, gs = staged[blk]

        def emit(h, o, rows=rows, gs=gs):
            cols = slice(h * HEAD_V, (h + 1) * HEAD_V)
            o_ref[rows, cols] = (_rmsnorm(o, hnw_ref[...]) * gs[:, cols]).astype(o_ref.dtype)

        chunks.append(_gla_chunk(qf, key, logf, v_ref, st_ref, ql_ref, kl_ref, rhs_ref, dm_ref, ids_ref, 1,
                                 HGRN_BASE, emit))
    _gla_finish(chunks)


def _gla_kernel(bps, xn_ref, w_ref, gb_ref, gnw_ref, dm_ref, ids_ref, o_ref, st_ref, *scratch):
    nb = xn_ref.shape[0] // T_BLK
    _step_start(bps, nb, st_ref)
    kd, w = GLA_KDIM, BRANCH_WIDTH
    chunks = []
    for blk in range(nb):
        rows = slice(blk * T_BLK, (blk + 1) * T_BLK)
        ql_ref, kl_ref, rhs_ref, v_ref = scratch[4 * blk:4 * blk + 4]
        proj = _dot(xn_ref[rows, :], w_ref[:, 0:GLA_PROJ])
        logf = _log_sigmoid(proj[:, 0:kd] + gb_ref[...]) * (1.0 / GLA_GATE_NORMALIZER)
        proj = proj[:, kd:]
        q = proj[:, 0:kd] * (GLA_HEAD_K ** -0.5)
        k = proj[:, kd:2 * kd]
        v_ref[...] = proj[:, 2 * kd:2 * kd + w].astype(BF16)
        gs = _silu(proj[:, 2 * kd + w:2 * kd + 2 * w])

        def emit(h, o, rows=rows, gs=gs):
            cols = slice(h * HEAD_V, (h + 1) * HEAD_V)
            o_ref[rows, cols] = (_rmsnorm(o, gnw_ref[...]) * gs[:, cols]).astype(o_ref.dtype)

        chunks.append(_gla_chunk(q, k, logf, v_ref, st_ref, ql_ref, kl_ref, rhs_ref, dm_ref, ids_ref, 2,
                                 GLA_BASE, emit))
    _gla_finish(chunks)


def _ssd_kernel(bps, xn_ref, w_ref, cw_ref, cb_ref, dtb_ref, aneg_ref, dsk_ref, snw_ref, tril_ref,
                exp_ref, o_ref, buf_ref, st_ref, *y_refs):
    nb = xn_ref.shape[0] // T_BLK
    _step_start(bps, nb, st_ref, buf_ref)
    for blk in range(nb):
        _ssd_block(blk, xn_ref, w_ref, cw_ref, cb_ref, dtb_ref, aneg_ref, dsk_ref, snw_ref, tril_ref,
                   exp_ref, o_ref, buf_ref, st_ref, y_refs[blk])
    _conv_carry(buf_ref)


def _ssd_block(blk, xn_ref, w_ref, cw_ref, cb_ref, dtb_ref, aneg_ref, dsk_ref, snw_ref, tril_ref,
               exp_ref, o_ref, buf_ref, st_ref, y_ref):
    t = T_BLK
    rows = slice(blk * t, (blk + 1) * t)
    w = BRANCH_WIDTH
    gs = SSD_GROUPS * SSD_STATE
    proj = _dot(xn_ref[rows, :], w_ref[:, 0:SSD_PROJ])
    z = proj[:, 0:w]
    xbc = _silu(_causal_conv(buf_ref, proj[:, w:w + SSD_CONV_DIM], cw_ref, cb_ref, blk * t))
    xs = xbc[:, 0:w]
    bm = xbc[:, w:w + gs].astype(BF16)
    cm = xbc[:, w + gs:w + 2 * gs]
    dt = _softplus(proj[:, w + SSD_CONV_DIM:] + dtb_ref[...])
    acs = _sel_dot(tril_ref[...], dt * aneg_ref[...])
    dt_e = _dot_sel(dt, exp_ref[...])
    acs_e = _dot_sel(acs, exp_ref[...])
    alast_e = acs_e[t - 1:t, :]
    xdt = xs * dt_e
    xdt_bf = xdt.astype(BF16)

    row = lax.broadcasted_iota(jnp.int32, (t, t), 0)
    col = lax.broadcasted_iota(jnp.int32, (t, t), 1)
    causal = col <= row
    lane = lax.broadcasted_iota(jnp.int32, (1, LANES), 1)
    acs_t = acs.T
    gmats = []
    for g in range(SSD_GROUPS):
        cg = jnp.where((lane // SSD_STATE) == g, cm, 0.0).astype(BF16)
        gmats.append(_dot_nt(cg, bm))
    hpg = SSD_HEADS // SSD_GROUPS
    for pair in range(SSD_HEADS // 2):
        tl = slice(pair * LANES, (pair + 1) * LANES)
        ys = []
        for sub in range(2):
            h = 2 * pair + sub
            seg = acs[:, h:h + 1] - acs_t[h:h + 1, :]
            m = jnp.where(causal, gmats[h // hpg] * jnp.exp2(seg), 0.0).astype(BF16)
            ys.append(_dot(m, xdt_bf[:, tl]))
        y_diag = jnp.where(lane < SSD_HEADDIM, ys[0], ys[1])
        y_off = _dot(cm.astype(BF16), st_ref[:, tl].astype(BF16)) * jnp.exp2(acs_e[:, tl])
        y = y_diag + y_off + xs[:, tl] * dsk_ref[:, tl]
        y_ref[:, tl] = y * _silu(z[:, tl])
    xdec = (xdt * jnp.exp2(alast_e - acs_e)).astype(BF16)
    upd = _dot_tn(bm, xdec)
    srow = lax.broadcasted_iota(jnp.int32, (gs, w), 0) // SSD_STATE
    scol = lax.broadcasted_iota(jnp.int32, (gs, w), 1) // (hpg * SSD_HEADDIM)
    st_ref[...] = st_ref[...] * jnp.exp2(alast_e) + jnp.where(srow == scol, upd, 0.0)
    gw = w // SSD_GROUPS
    for g in range(SSD_GROUPS):
        yg = y_ref[:, g * gw:(g + 1) * gw]
        o_ref[rows, g * gw:(g + 1) * gw] = (
            yg * lax.rsqrt(jnp.mean(yg * yg, axis=-1, keepdims=True) + NORM_EPS) * snw_ref[:, g * gw:(g + 1) * gw]
        ).astype(o_ref.dtype)


def _sqrt_one_minus_exp2x(x):
    th = jnp.tanh(x)
    y = (-2.0 * th) / (1.0 - th)
    return y * lax.rsqrt(jnp.maximum(y, 1e-30))


def _doubling_scan(a, b, pos, n, axis):
    off = 1
    while off < n:
        keep = pos >= off
        a_sh = jnp.where(keep, pltpu.roll(a, off, axis), 1.0)
        b_sh = jnp.where(keep, pltpu.roll(b, off, axis), 0.0)
        b = a * b_sh + b
        a = a * a_sh
        off *= 2
    return a, b


def _scan_rows(a, b, hc_ref, ab_ref, cg_ref):
    t, n = a.shape
    g = t // SUBLANES
    nt = n // LANES
    a8, b8 = _doubling_scan(a.reshape(g, SUBLANES, n), b.reshape(g, SUBLANES, n),
                            lax.broadcasted_iota(jnp.int32, (1, SUBLANES, 1), 1), SUBLANES, 1)
    a8 = a8.reshape(t, n)
    b8 = b8.reshape(t, n)
    for i in range(nt):
        ab_ref[i] = a8[:, i * LANES:(i + 1) * LANES]
        ab_ref[nt + i] = b8[:, i * LANES:(i + 1) * LANES]
    last = pl.ds(SUBLANES - 1, g, stride=SUBLANES)
    gidx = lax.broadcasted_iota(jnp.int32, (g, 1), 0)
    ag = jnp.concatenate([ab_ref[i, last, :] for i in range(nt)], axis=1)
    bg = jnp.concatenate([ab_ref[nt + i, last, :] for i in range(nt)], axis=1)
    ag, bg = _doubling_scan(ag, bg, gidx, g, 0)
    state = bg + ag * hc_ref[...]
    cg_ref[...] = jnp.where(gidx >= 1, pltpu.roll(state, 1, 0), hc_ref[...])
    hc_ref[...] = state[g - 1:g, :]
    out = []
    for i in range(nt):
        cg = cg_ref[:, i * LANES:(i + 1) * LANES].reshape(g, 1, LANES)
        hi = ab_ref[nt + i].reshape(g, SUBLANES, LANES) + ab_ref[i].reshape(g, SUBLANES, LANES) * cg
        out.append(hi.reshape(t, LANES))
    return jnp.concatenate(out, axis=1)


def _lru_kernel(bps, xn_ref, w_ref, cw_ref, cb_ref, wg_ref, bg_ref, lam_ref, o_ref, buf_ref, hc_ref,
                *scratch):
    nb = xn_ref.shape[0] // T_BLK
    _step_start(bps, nb, hc_ref, buf_ref)
    for blk in range(nb):
        _lru_block(blk, xn_ref, w_ref, cw_ref, cb_ref, wg_ref, bg_ref, lam_ref, o_ref, buf_ref, hc_ref,
                   *scratch[2 * blk:2 * blk + 2])
    _conv_carry(buf_ref)


def _lru_block(blk, xn_ref, w_ref, cw_ref, cb_ref, wg_ref, bg_ref, lam_ref, o_ref, buf_ref, hc_ref,
               ab_ref, cg_ref):
    t = T_BLK
    rows = slice(blk * t, (blk + 1) * t)
    w = BRANCH_WIDTH
    proj = _dot(xn_ref[rows, :], w_ref[...])
    u = _causal_conv(buf_ref, proj[:, 0:w], cw_ref, cb_ref, blk * t)
    rg = _dot(u.astype(BF16), wg_ref[...]) + bg_ref[...]
    r = _sigmoid(rg[:, 0:w])
    ig = _sigmoid(rg[:, w:2 * w])
    log_a = (-LRU_C) * r * _softplus(-lam_ref[...])
    a = jnp.exp(log_a)
    b = _sqrt_one_minus_exp2x(log_a) * (ig * u)
    hseq = _scan_rows(a, b, hc_ref, ab_ref, cg_ref)
    gate = proj[:, w:2 * w]
    gelu = 0.5 * gate * (1.0 + jnp.tanh(math.sqrt(2.0 / math.pi) * (gate + 0.044715 * gate * gate * gate)))
    o_ref[rows, :] = (hseq * gelu).astype(o_ref.dtype)


def _merge_kernel(h_ref, xn_ref, ya_ref, yb_ref, yc_ref, yd_ref, wmg_ref, wbr_ref, wout_ref, o_ref):
    xn = xn_ref[...]
    merged = None
    for n, y_ref in enumerate((ya_ref, yb_ref, yc_ref, yd_ref)):
        gate = _sigmoid(_dot(xn, wmg_ref[:, n * D_MODEL:(n + 1) * D_MODEL]))
        term = gate * _dot(y_ref[...], wbr_ref[n])
        merged = term if merged is None else merged + term
    o_ref[...] = h_ref[...] + _dot(merged.astype(BF16), wout_ref[...])


def _ffn_kernel(final, h_ref, nw_ref, win_ref, wout_ref, nnw_ref, o_ref, *xn_out):
    hres = h_ref[...]
    xn = _rmsnorm(hres, nw_ref[...]).astype(BF16)
    gu = _dot(xn, win_ref[...])
    act = (_silu(gu[:, 0:D_FF]) * gu[:, D_FF:2 * D_FF]).astype(BF16)
    out = hres + _dot(act, wout_ref[...])
    if final:
        o_ref[...] = _rmsnorm(out, nnw_ref[...])
    else:
        o_ref[...] = out
        xn_out[0][...] = _rmsnorm(out, nnw_ref[...]).astype(BF16)


def _norm_kernel(x_ref, nw_ref, o_ref):
    o_ref[...] = _rmsnorm(x_ref[...], nw_ref[...]).astype(o_ref.dtype)


def _const_spec(shape):
    nd = len(shape)
    return pl.BlockSpec(shape, lambda i, _nd=nd: (0,) * _nd, pipeline_mode=pl.Buffered(1))


def _row_spec(rows, width):
    return pl.BlockSpec((rows, width), lambda i: (i, 0))


class _LayerWindow:
    def __init__(self, array, layer, width=None, col=0):
        self.array, self.layer, self.width, self.col = array, layer, width or array.shape[-1], col

    def spec(self):
        shape = self.array.shape
        mid = (0,) * (len(shape) - 2)
        layer, col = self.layer, self.col
        return pl.BlockSpec((None,) + shape[1:-1] + (self.width,), lambda i: (layer,) + mid + (col,),
                            pipeline_mode=pl.Buffered(1))


def _pcall(kernel, name, n_tokens, rows, operands, row_flags, outs, scratch):
    in_specs = [a.spec() if isinstance(a, _LayerWindow) else
                (_row_spec(rows, a.shape[1]) if is_row else _const_spec(a.shape))
                for a, is_row in zip(operands, row_flags)]
    operands = [a.array if isinstance(a, _LayerWindow) else a for a in operands]
    many = isinstance(outs, list)
    out_specs = [_row_spec(rows, wd) for wd, _ in (outs if many else [outs])]
    out_shape = [jax.ShapeDtypeStruct((n_tokens, wd), dt) for wd, dt in (outs if many else [outs])]
    return pl.pallas_call(
        kernel,
        name=name,
        grid=(n_tokens // rows,),
        in_specs=in_specs,
        out_specs=out_specs if many else out_specs[0],
        out_shape=out_shape if many else out_shape[0],
        scratch_shapes=scratch,
        compiler_params=pltpu.CompilerParams(dimension_semantics=("arbitrary",), vmem_limit_bytes=VMEM_LIMIT),
    )(*operands)


def _tril_const(t):
    return jnp.asarray(np.tril(np.ones((t, t), np.float32)), BF16)


def _gla_scratch(nk, nb):
    ntile = nk // LANES
    per_block = [
        pltpu.VMEM((GLA_PARTS, T_BLK, nk), BF16),
        pltpu.VMEM((GLA_PARTS, T_BLK, nk), BF16),
        pltpu.VMEM((2 * T_BLK, nk), BF16),
        pltpu.VMEM((T_BLK, BRANCH_WIDTH), BF16),
    ]
    return [pltpu.VMEM((ntile, HEAD_V, LANES), F32)] + per_block * nb


def _row(x):
    return x.reshape(1, -1).astype(F32)


PACK_MERGE = (4096, 0)
PACK_HGRN = (2048, 2)
PACK_SSD = (2048, 3)
PACK_GLA = (2048, 4)
PACK_LRU = (1024, 10)


PACK_COLS = 11 * 1024
PACK_ROWS = 128
IN_COLS = [0] + np.cumsum([512, 512, 512, 512, 512, 768, 8, 256, 256, 512, 512, 16, 512, 512, 4096]).tolist()


def _pack_kernel(w_ref, wg_ref, o_ref):
    c = IN_COLS

    def put(dst, lo, hi):
        o_ref[:, dst:dst + hi - lo] = w_ref[:, lo:hi].astype(BF16)
        return dst + hi - lo

    def pad(lo, hi):
        o_ref[:, lo:hi] = jnp.zeros((o_ref.shape[0], hi - lo), BF16)

    put(PACK_MERGE[0] * PACK_MERGE[1], c[14], c[15])
    put(PACK_HGRN[0] * PACK_HGRN[1], c[0], c[4])
    start = PACK_SSD[0] * PACK_SSD[1]
    pad(put(start, c[4], c[7]), start + PACK_SSD[0])
    start = PACK_GLA[0] * PACK_GLA[1]
    o_ref[:, start:start + GLA_KDIM] = wg_ref[...].astype(BF16)
    pad(put(start + GLA_KDIM, c[7], c[11]), start + PACK_GLA[0])
    put(PACK_LRU[0] * PACK_LRU[1], c[12], c[14])


def _pack_in_proj(w_in, gla_gate_w):
    depth, rows, cols = w_in.shape
    w_gate = jnp.einsum("lkr,lrn->lkn", w_in[:, :, IN_COLS[11]:IN_COLS[12]].astype(F32), gla_gate_w.astype(F32),
                        precision=lax.Precision.HIGHEST)
    cols = pl.cdiv(cols, LANES) * LANES
    w_in = jnp.pad(w_in.astype(BF16), ((0, 0), (0, 0), (0, cols - w_in.shape[2])))
    return pl.pallas_call(
        _pack_kernel,
        name="pack",
        grid=(depth, rows // PACK_ROWS),
        in_specs=[pl.BlockSpec((None, PACK_ROWS, cols), lambda l, i: (l, i, 0)),
                  pl.BlockSpec((None, PACK_ROWS, GLA_KDIM), lambda l, i: (l, i, 0))],
        out_specs=pl.BlockSpec((None, PACK_ROWS, PACK_COLS), lambda l, i: (l, i, 0)),
        out_shape=jax.ShapeDtypeStruct((depth, rows, PACK_COLS), BF16),
        compiler_params=pltpu.CompilerParams(dimension_semantics=("arbitrary", "arbitrary"),
                                             vmem_limit_bytes=VMEM_LIMIT),
    )(w_in, w_gate)


def _layer(h, xn, seq, l, wts, p):
    n_tokens = h.shape[0]
    bps = seq // T_BLK
    assert n_tokens % seq == 0 and seq % T_STEP == 0
    assert all(seq % (nb * T_BLK) == 0 for nb in (HGRN_CHUNKS, GLA_CHUNKS, SSD_CHUNKS, LRU_CHUNKS))
    tril = _tril_const(T_BLK)
    dmat = _gla_decay_matrix(T_BLK)
    part_ids = _gla_part_ids()
    w_all, w_br, w_o, w_fi, w_fo = wts
    win = lambda pack: _LayerWindow(w_all, l, *pack)

    lb = p["lb"].astype(F32)
    ya = _pcall(
        functools.partial(_hgrn_kernel, bps), "hgrn2", n_tokens, HGRN_CHUNKS * T_BLK,
        [xn, win(PACK_HGRN), _row(lb), _row(1.0 - lb), _row(p["hgrn_norm_w"]), dmat, part_ids],
        [True] + [False] * 6, (BRANCH_WIDTH, BF16), _gla_scratch(BRANCH_WIDTH, HGRN_CHUNKS))

    pad8 = lambda v: jnp.concatenate([v.astype(F32), jnp.zeros((LANES - SSD_HEADS,), F32)]).reshape(1, LANES)
    expand = jnp.asarray(np.arange(LANES)[:, None] == (np.arange(BRANCH_WIDTH)[None, :] // SSD_HEADDIM), BF16)
    yb = _pcall(
        functools.partial(_ssd_kernel, bps), "ssd", n_tokens, SSD_CHUNKS * T_BLK,
        [xn, win(PACK_SSD), p["ssd_conv_w"].astype(F32), _row(p["ssd_conv_b"]), pad8(p["ssd_dt_bias"]),
         pad8(-LOG2E * jnp.exp(p["ssd_a_log"].astype(F32))), _row(jnp.repeat(p["ssd_d"], SSD_HEADDIM)), _row(p["ssd_norm_w"]),
         tril, expand],
        [True] + [False] * 9, (BRANCH_WIDTH, BF16),
        [pltpu.VMEM((SSD_CHUNKS * T_BLK + SUBLANES, SSD_CONV_DIM), F32),
         pltpu.VMEM((SSD_GROUPS * SSD_STATE, BRANCH_WIDTH), F32)]
        + [pltpu.VMEM((T_BLK, BRANCH_WIDTH), F32)] * SSD_CHUNKS)

    yc = _pcall(
        functools.partial(_gla_kernel, bps), "gla", n_tokens, GLA_CHUNKS * T_BLK,
        [xn, win(PACK_GLA), _row(p["gla_gate_b"]), _row(p["gla_norm_w"]), dmat, part_ids],
        [True] + [False] * 5, (BRANCH_WIDTH, BF16), _gla_scratch(GLA_KDIM, GLA_CHUNKS))

    eye = jnp.eye(LRU_BLOCKS, dtype=F32)
    bdiag = lambda wt: jnp.einsum("kde,kl->kdle", wt.astype(F32), eye).reshape(BRANCH_WIDTH, BRANCH_WIDTH)
    w_gates = jnp.concatenate([bdiag(p["lru_wa"]), bdiag(p["lru_wx"])], axis=1).astype(BF16)
    b_gates = _row(jnp.concatenate([p["lru_ba"], p["lru_bx"]]))
    yd = _pcall(
        functools.partial(_lru_kernel, bps), "rglru", n_tokens, LRU_CHUNKS * T_BLK,
        [xn, win(PACK_LRU), p["lru_conv_w"].astype(F32), _row(p["lru_conv_b"]), w_gates,
         b_gates, _row(p["lru_lambda"])],
        [True] + [False] * 6, (BRANCH_WIDTH, BF16),
        [pltpu.VMEM((LRU_CHUNKS * T_BLK + SUBLANES, BRANCH_WIDTH), F32), pltpu.VMEM((1, BRANCH_WIDTH), F32)]
        + [pltpu.VMEM((2 * BRANCH_WIDTH // LANES, T_BLK, LANES), F32),
           pltpu.VMEM((T_BLK // SUBLANES, BRANCH_WIDTH), F32)] * LRU_CHUNKS)

    h = _pcall(
        _merge_kernel, "merge", n_tokens, T_STEP,
        [h, xn, ya, yb, yc, yd, win(PACK_MERGE), _LayerWindow(w_br, l), _LayerWindow(w_o, l)],
        [True, True, True, True, True, True, False, False, False], (D_MODEL, F32), [])

    outs = (D_MODEL, F32) if p["final"] else [(D_MODEL, F32), (D_MODEL, BF16)]
    res = _pcall(
        functools.partial(_ffn_kernel, p["final"]), "swiglu", n_tokens, T_STEP,
        [h, _row(p["norm_ffn_w"]), _LayerWindow(w_fi, l), _LayerWindow(w_fo, l), _row(p["next_norm_w"])],
        [True, False, False, False, False], outs, [])
    return (res, None) if p["final"] else tuple(res)


def kernel(x, norm_mix_w, w_in, hgrn_lower_bounds, hgrn_norm_w, ssd_conv_w, ssd_conv_b, ssd_dt_bias, ssd_a_log, ssd_d, ssd_norm_w, gla_gate_w, gla_gate_b, gla_norm_w, lru_conv_w, lru_conv_b, lru_wa, lru_ba, lru_wx, lru_bx, lru_lambda, w_branch, w_out, norm_ffn_w, w_ffn_in, w_ffn_out, norm_f_w):
    bsz, seq, _ = x.shape
    depth = w_in.shape[0]
    assert (bsz * seq) % NORM_ROWS == 0
    lb_all = jnp.cumsum(jax.nn.softmax(hgrn_lower_bounds.astype(F32), axis=0), axis=0)
    lb_all = lb_all - lb_all[0:1]
    wts = (_pack_in_proj(w_in, gla_gate_w), w_branch.astype(BF16), w_out.astype(BF16), w_ffn_in.astype(BF16),
           w_ffn_out.astype(BF16))
    h = x.reshape(bsz * seq, D_MODEL)
    xn = _pcall(_norm_kernel, "norm", bsz * seq, NORM_ROWS, [h, _row(norm_mix_w[0])], [True, False], (D_MODEL, BF16), [])
    for l in range(depth):
        p = dict(
            lb=lb_all[l], hgrn_norm_w=hgrn_norm_w[l],
            ssd_conv_w=ssd_conv_w[l], ssd_conv_b=ssd_conv_b[l], ssd_dt_bias=ssd_dt_bias[l], ssd_a_log=ssd_a_log[l],
            ssd_d=ssd_d[l], ssd_norm_w=ssd_norm_w[l], gla_gate_b=gla_gate_b[l],
            gla_norm_w=gla_norm_w[l], lru_conv_w=lru_conv_w[l], lru_conv_b=lru_conv_b[l], lru_wa=lru_wa[l],
            lru_ba=lru_ba[l], lru_wx=lru_wx[l], lru_bx=lru_bx[l], lru_lambda=lru_lambda[l],
            norm_ffn_w=norm_ffn_w[l],
            next_norm_w=(norm_f_w if l == depth - 1 else norm_mix_w[l + 1]), final=(l == depth - 1))
        h, xn = _layer(h, xn, seq, l, wts, p)
    return h.reshape(bsz, seq, D_MODEL)
```

```python
import functools
import math

import jax
import jax.numpy as jnp
import numpy as np
from jax import lax
from jax.experimental import pallas as pl
from jax.experimental.pallas import tpu as pltpu

F32 = jnp.float32
BF16 = jnp.bfloat16

D_MODEL = 1024
BRANCH_WIDTH = 512
CONV_WIDTH = 4
NORM_EPS = 1e-6
SSD_HEADS = 8
SSD_HEADDIM = 64
SSD_GROUPS = 2
SSD_STATE = 64
SSD_CONV_DIM = 768
GLA_KDIM = 256
GLA_HEAD_K = 64
GLA_GATE_NORMALIZER = 16.0
LRU_BLOCKS = 8
LRU_C = 8.0
D_FF = 2816

LANES = 128
SUBLANES = 8
T_BLK = 256
T_STEP = 2 * T_BLK
HGRN_CHUNKS, GLA_CHUNKS, SSD_CHUNKS, LRU_CHUNKS = 2, 4, 4, 4
NORM_ROWS = 2048
HEAD_V = 128
VMEM_LIMIT = 48 * 1024 * 1024
F32_FLOOR = 1e-37
SSD_PROJ = BRANCH_WIDTH + SSD_CONV_DIM + LANES
GLA_PROJ = 3 * GLA_KDIM + 2 * BRANCH_WIDTH


def _dot(a, b):
    return jnp.dot(a, b, preferred_element_type=F32)


def _dot_nt(a, b):
    return lax.dot_general(a, b, (((1,), (1,)), ((), ())), preferred_element_type=F32)


def _dot_tn(a, b):
    return lax.dot_general(a, b, (((0,), (0,)), ((), ())), preferred_element_type=F32)


def _split3(x):
    hi = x.astype(BF16)
    r1 = x - hi.astype(F32)
    mid = r1.astype(BF16)
    lo = (r1 - mid.astype(F32)).astype(BF16)
    return hi, mid, lo


def _sel_dot(mat, x):
    hi, mid, lo = _split3(x)
    return (_dot(mat, lo) + _dot(mat, mid)) + _dot(mat, hi)


def _dot_sel(x, mat):
    hi, mid, lo = _split3(x)
    return (_dot(lo, mat) + _dot(mid, mat)) + _dot(hi, mat)


def _sigmoid(x):
    return jax.nn.sigmoid(x)


def _silu(x):
    return x * _sigmoid(x)


def _softplus(x):
    return jnp.maximum(x, 0.0) + jnp.log1p(jnp.exp(-jnp.abs(x)))


def _log_sigmoid(x):
    return jnp.minimum(x, 0.0) - jnp.log1p(jnp.exp(-jnp.abs(x)))


def _rmsnorm(x, w):
    return x * lax.rsqrt(jnp.mean(x * x, axis=-1, keepdims=True) + NORM_EPS) * w


GLA_SMALL = (1, 2, 4)
GLA_MID = (8, 16, 32, 64)
GLA_PARTS = 2 + len(GLA_SMALL) + len(GLA_MID) + 1
LOG2E = 1.4426950408889634
GLA_SPAN_LIMIT = 62.0
HGRN_BASE = 64
GLA_BASE = 128


def _gla_part_ids():
    row = np.arange(LANES)[:, None]
    col = np.arange(LANES)[None, :]
    x = row ^ col
    ids = np.where(x == 0, 0, np.floor(np.log2(np.maximum(x, 1))).astype(np.int64) + 1)
    return jnp.asarray(np.where(col <= row, ids, -1), jnp.int32)


def _gla_decay_matrix(t):
    u = np.arange(t)[None, :]
    r = np.arange(t)[:, None]
    blocks = [u <= r]
    for s in GLA_SMALL:
        mid = (r // (2 * s)) * (2 * s) + s - 1
        blocks.append(((u > mid) & (u <= r)) | ((u > r) & (u <= mid)))
    m = np.concatenate(blocks, axis=0)
    return jnp.asarray(np.concatenate([m, m], axis=1), BF16)


def _store_decay_pieces(rhs_ref, logf):
    t = logf.shape[0]
    lf2 = logf * LOG2E
    hi = lf2.astype(BF16)
    rhs_ref[0:t, :] = hi
    rhs_ref[t:2 * t, :] = (lf2 - hi.astype(F32)).astype(BF16)


def _gla_chunk(q, k, logf, v_ref, st_ref, ql_ref, kl_ref, rhs_ref, dm_ref, ids_ref, hpt, base, emit):
    t, nk = q.shape
    assert t == 2 * LANES
    ntile = nk // LANES
    dk = LANES // hpt
    levels = GLA_SMALL + GLA_MID

    _store_decay_pieces(rhs_ref, logf)
    bcum = _dot(dm_ref[0:t, :], rhs_ref[...])
    blast = bcum[t - 1:t, :]
    qb = q.astype(BF16)
    kb = k.astype(BF16)
    ql_ref[0] = qb * jnp.exp2(bcum).astype(BF16)
    kl_ref[0] = kb * jnp.exp2(blast - bcum).astype(BF16)

    def level_part(part, s):
        bc3 = bcum.reshape(t // (2 * s), 2 * s, nk)
        ref = bc3[:, s - 1:s, :]
        e = jnp.exp2(jnp.concatenate([ref - bc3[:, 0:s, :], bc3[:, s:2 * s, :] - ref], axis=1))
        e = e.reshape(t, nk).astype(BF16)
        ql_ref[part] = qb * e
        kl_ref[part] = kb * e

    part_of = {}
    part = 1
    for s in levels + (LANES,):
        if s >= base:
            level_part(part, s)
            part_of[s] = part
            part += 1
    free = part
    p_top = part_of[LANES]

    bcb = bcum.reshape(t // base, base, nk)
    span = bcb - bcb[:, base // 2 - 1:base // 2, :]
    fits = jnp.max(jnp.abs(span)) <= GLA_SPAN_LIMIT

    ids = ids_ref[...]
    lane = lax.broadcasted_iota(jnp.int32, (1, LANES), 1)

    def own(x, sub):
        return x if hpt == 1 else jnp.where((lane // dk) == sub, x, jnp.zeros_like(x))

    def stacked(x):
        return x if hpt == 1 else jnp.concatenate([own(x, sub) for sub in range(hpt)], axis=0)

    ids_st = ids if hpt == 1 else jnp.concatenate([ids] * hpt, axis=0)

    def finish(first, overrides):
        intra = []
        for tau in range(ntile):
            tl = slice(tau * LANES, (tau + 1) * LANES)
            diag = []
            for c in range(2):
                rows = slice(c * LANES, (c + 1) * LANES)
                sc = jnp.where(ids_st >= 0, _dot_nt(stacked(ql_ref[first, rows, tl]), kl_ref[first, rows, tl]), 0.0)
                for pid, p in overrides:
                    sc = jnp.where(ids_st == pid, _dot_nt(stacked(ql_ref[p, rows, tl]), kl_ref[p, rows, tl]), sc)
                diag.append(sc.astype(BF16))
            low = _dot_nt(stacked(ql_ref[p_top, LANES:t, tl]), kl_ref[p_top, 0:LANES, tl]).astype(BF16)
            for sub in range(hpt):
                hv = slice((tau * hpt + sub) * HEAD_V, (tau * hpt + sub + 1) * HEAD_V)
                hr = slice(sub * LANES, (sub + 1) * LANES)
                o_top = _dot(diag[0][hr], v_ref[0:LANES, hv])
                o_bot = _dot(jnp.concatenate([low[hr], diag[1][hr]], axis=1), v_ref[:, hv])
                intra.append(jnp.concatenate([o_top, o_bot], axis=0))
            yield
        for tau in range(ntile):
            tl = slice(tau * LANES, (tau + 1) * LANES)
            inter = _dot_nt(stacked(ql_ref[0, :, tl]), st_ref[tau].astype(BF16))
            for sub in range(hpt):
                h = tau * hpt + sub
                emit(h, intra[h] + inter[sub * t:(sub + 1) * t])
        for tau in range(ntile):
            tl = slice(tau * LANES, (tau + 1) * LANES)
            kd = kl_ref[0, :, tl]
            upd = _dot_tn(v_ref[:, tau * hpt * HEAD_V:(tau * hpt + 1) * HEAD_V], kd)
            for sub in range(1, hpt):
                h = tau * hpt + sub
                upd = jnp.where((lane // dk) == sub, _dot_tn(v_ref[:, h * HEAD_V:(h + 1) * HEAD_V], kd), upd)
            st_ref[tau] = st_ref[tau] * jnp.exp2(blast[:, tl]) + upd

    def part_id(s):
        return 1 + levels.index(s)

    upper = [(part_id(s), part_of[s]) for s in levels if s >= base]

    def one_reference():
        ql_ref[free] = qb * jnp.exp2(span).reshape(t, nk).astype(BF16)
        kl_ref[free] = kb * jnp.exp2(-span).reshape(t, nk).astype(BF16)
        yield from finish(free, upper)

    def safe_parts():
        d3 = _dot(dm_ref[t:(1 + len(GLA_SMALL)) * t, :], rhs_ref[...])
        lower = []
        part = free
        for i, s in enumerate(GLA_SMALL):
            if s < base:
                e = jnp.exp2(d3[i * t:(i + 1) * t]).astype(BF16)
                ql_ref[part] = qb * e
                kl_ref[part] = kb * e
                lower.append((part_id(s), part))
                part += 1
        for s in GLA_MID:
            if s < base:
                level_part(part, s)
                lower.append((part_id(s), part))
                part += 1
        ql_ref[part] = qb
        kl_ref[part] = kb
        yield from finish(part, lower + upper)

    return fits, one_reference, safe_parts


def _interleave(chains):
    live = list(chains)
    while live:
        for c in list(live):
            try:
                next(c)
            except StopIteration:
                live.remove(c)


def _gla_finish(chunks):
    fits = functools.reduce(jnp.logical_and, [c[0] for c in chunks])

    @pl.when(fits)
    def _():
        _interleave([c[1]() for c in chunks])

    @pl.when(jnp.logical_not(fits))
    def _():
        _interleave([c[2]() for c in chunks])


def _causal_conv(buf_ref, x, w_ref, b_ref, base, act=lambda y: y):
    t = x.shape[0]
    buf_ref[SUBLANES + base:SUBLANES + base + t, :] = x
    out = []
    for c0 in range(0, x.shape[1], LANES):
        cs = slice(c0, c0 + LANES)
        acc = b_ref[:, cs] + w_ref[CONV_WIDTH - 1:CONV_WIDTH, cs] * buf_ref[pl.ds(SUBLANES + base, t), cs]
        for s in range(1, CONV_WIDTH):
            acc = acc + w_ref[CONV_WIDTH - 1 - s:CONV_WIDTH - s, cs] * buf_ref[pl.ds(SUBLANES + base - s, t), cs]
        out.append(act(acc))
    return jnp.concatenate(out, axis=1)


def _step_start(bps, nb, *carry_refs):
    @pl.when((pl.program_id(0) * nb) % bps == 0)
    def _():
        for ref in carry_refs:
            ref[...] = jnp.zeros_like(ref)


def _conv_carry(buf_ref):
    rows = buf_ref.shape[0] - SUBLANES
    buf_ref[0:SUBLANES, :] = buf_ref[rows:rows + SUBLANES, :]


def _hgrn_kernel(bps, xn_ref, w_ref, lb_ref, oml_ref, hnw_ref, dm_ref, ids_ref,
                 o_ref, st_ref, *scratch):
    nb = xn_ref.shape[0] // T_BLK
    _step_start(bps, nb, st_ref)
    w = BRANCH_WIDTH
    staged = []
    for blk in range(nb):
        rows = slice(blk * T_BLK, (blk + 1) * T_BLK)
        ql_ref, kl_ref, rhs_ref, v_ref = scratch[4 * blk:4 * blk + 4]
        proj = _dot(xn_ref[rows, :], w_ref[...])
        fp = proj[:, w:2 * w]
        u = jnp.exp(-jnp.abs(fp))
        r = 1.0 / (1.0 + u)
        pos = fp >= 0.0
        sig = jnp.where(pos, r, u * r)
        key = oml_ref[...] * jnp.where(pos, u * r, r)
        logf = jnp.log(jnp.maximum(lb_ref[...] + oml_ref[...] * sig, F32_FLOOR))
        qf = _silu(proj[:, 0:w])
        v_ref[...] = proj[:, 2 * w:3 * w].astype(BF16)
        staged.append((qf, key, logf, _silu(proj[:, 3 * w:4 * w])))
    chunks = []
    for blk in range(nb):
        rows = slice(blk * T_BLK, (blk + 1) * T_BLK)
        ql_ref, kl_ref, rhs_ref, v_ref = scratch[4 * blk:4 * blk + 4]
        qf, key, logf, gs = staged[blk]

        def emit(h, o, rows=rows, gs=gs):
            cols = slice(h * HEAD_V, (h + 1) * HEAD_V)
            o_ref[rows, cols] = (_rmsnorm(o, hnw_ref[...]) * gs[:, cols]).astype(o_ref.dtype)

        chunks.append(_gla_chunk(qf, key, logf, v_ref, st_ref, ql_ref, kl_ref, rhs_ref, dm_ref, ids_ref, 1,
                                 HGRN_BASE, emit))
    _gla_finish(chunks)


def _gla_kernel(bps, xn_ref, w_ref, gb_ref, gnw_ref, dm_ref, ids_ref, o_ref, st_ref, *scratch):
    nb = xn_ref.shape[0] // T_BLK
    _step_start(bps, nb, st_ref)
    kd, w = GLA_KDIM, BRANCH_WIDTH
    chunks = []
    for blk in range(nb):
        rows = slice(blk * T_BLK, (blk + 1) * T_BLK)
        ql_ref, kl_ref, rhs_ref, v_ref = scratch[4 * blk:4 * blk + 4]
        proj = _dot(xn_ref[rows, :], w_ref[:, 0:GLA_PROJ])
        logf = _log_sigmoid(proj[:, 0:kd] + gb_ref[...]) * (1.0 / GLA_GATE_NORMALIZER)
        proj = proj[:, kd:]
        q = proj[:, 0:kd] * (GLA_HEAD_K ** -0.5)
        k = proj[:, kd:2 * kd]
        v_ref[...] = proj[:, 2 * kd:2 * kd + w].astype(BF16)
        gs = _silu(proj[:, 2 * kd + w:2 * kd + 2 * w])

        def emit(h, o, rows=rows, gs=gs):
            cols = slice(h * HEAD_V, (h + 1) * HEAD_V)
            o_ref[rows, cols] = (_rmsnorm(o, gnw_ref[...]) * gs[:, cols]).astype(o_ref.dtype)

        chunks.append(_gla_chunk(q, k, logf, v_ref, st_ref, ql_ref, kl_ref, rhs_ref, dm_ref, ids_ref, 2,
                                 GLA_BASE, emit))
    _gla_finish(chunks)


def _ssd_kernel(bps, xn_ref, w_ref, cw_ref, cb_ref, dtb_ref, aneg_ref, dsk_ref, snw_ref, tril_ref,
                exp_ref, o_ref, buf_ref, st_ref, *y_refs):
    nb = xn_ref.shape[0] // T_BLK
    _step_start(bps, nb, st_ref, buf_ref)
    for blk in range(nb):
        _ssd_block(blk, xn_ref, w_ref, cw_ref, cb_ref, dtb_ref, aneg_ref, dsk_ref, snw_ref, tril_ref,
                   exp_ref, o_ref, buf_ref, st_ref, y_refs[blk])
    _conv_carry(buf_ref)


def _ssd_block(blk, xn_ref, w_ref, cw_ref, cb_ref, dtb_ref, aneg_ref, dsk_ref, snw_ref, tril_ref,
               exp_ref, o_ref, buf_ref, st_ref, y_ref):
    t = T_BLK
    rows = slice(blk * t, (blk + 1) * t)
    w = BRANCH_WIDTH
    gs = SSD_GROUPS * SSD_STATE
    proj = _dot(xn_ref[rows, :], w_ref[:, 0:SSD_PROJ])
    z = proj[:, 0:w]
    xbc = _causal_conv(buf_ref, proj[:, w:w + SSD_CONV_DIM], cw_ref, cb_ref, blk * t, _silu)
    xs = xbc[:, 0:w]
    bm = xbc[:, w:w + gs].astype(BF16)
    cm = xbc[:, w + gs:w + 2 * gs]
    dt = _softplus(proj[:, w + SSD_CONV_DIM:] + dtb_ref[...])
    acs = _sel_dot(tril_ref[...], dt * aneg_ref[...])
    dt_e = _dot_sel(dt, exp_ref[...])
    acs_e = _dot_sel(acs, exp_ref[...])
    alast_e = acs_e[t - 1:t, :]
    xdt = xs * dt_e
    xdt_bf = xdt.astype(BF16)

    row = lax.broadcasted_iota(jnp.int32, (t, t), 0)
    col = lax.broadcasted_iota(jnp.int32, (t, t), 1)
    causal = col <= row
    lane = lax.broadcasted_iota(jnp.int32, (1, LANES), 1)
    acs_t = acs.T
    gmats = []
    for g in range(SSD_GROUPS):
        cg = jnp.where((lane // SSD_STATE) == g, cm, 0.0).astype(BF16)
        gmats.append(_dot_nt(cg, bm))
    hpg = SSD_HEADS // SSD_GROUPS
    for pair in range(SSD_HEADS // 2):
        tl = slice(pair * LANES, (pair + 1) * LANES)
        ys = []
        for sub in range(2):
            h = 2 * pair + sub
            seg = acs[:, h:h + 1] - acs_t[h:h + 1, :]
            m = jnp.where(causal, gmats[h // hpg] * jnp.exp2(seg), 0.0).astype(BF16)
            ys.append(_dot(m, xdt_bf[:, tl]))
        y_diag = jnp.where(lane < SSD_HEADDIM, ys[0], ys[1])
        y_off = _dot(cm.astype(BF16), st_ref[:, tl].astype(BF16)) * jnp.exp2(acs_e[:, tl])
        y = y_diag + y_off + xs[:, tl] * dsk_ref[:, tl]
        y_ref[:, tl] = y * _silu(z[:, tl])
    xdec = (xdt * jnp.exp2(alast_e - acs_e)).astype(BF16)
    upd = _dot_tn(bm, xdec)
    srow = lax.broadcasted_iota(jnp.int32, (gs, w), 0) // SSD_STATE
    scol = lax.broadcasted_iota(jnp.int32, (gs, w), 1) // (hpg * SSD_HEADDIM)
    st_ref[...] = st_ref[...] * jnp.exp2(alast_e) + jnp.where(srow == scol, upd, 0.0)
    gw = w // SSD_GROUPS
    for g in range(SSD_GROUPS):
        yg = y_ref[:, g * gw:(g + 1) * gw]
        o_ref[rows, g * gw:(g + 1) * gw] = (
            yg * lax.rsqrt(jnp.mean(yg * yg, axis=-1, keepdims=True) + NORM_EPS) * snw_ref[:, g * gw:(g + 1) * gw]
        ).astype(o_ref.dtype)


def _sqrt_one_minus_exp2x(x):
    th = jnp.tanh(x)
    y = (-2.0 * th) / (1.0 - th)
    return y * lax.rsqrt(jnp.maximum(y, 1e-30))


def _doubling_scan(a, b, pos, n, axis):
    off = 1
    while off < n:
        keep = pos >= off
        a_sh = jnp.where(keep, pltpu.roll(a, off, axis), 1.0)
        b_sh = jnp.where(keep, pltpu.roll(b, off, axis), 0.0)
        b = a * b_sh + b
        a = a * a_sh
        off *= 2
    return a, b


def _scan_rows(a, b, hc_ref, ab_ref, cg_ref):
    t, n = a.shape
    g = t // SUBLANES
    nt = n // LANES
    a8, b8 = _doubling_scan(a.reshape(g, SUBLANES, n), b.reshape(g, SUBLANES, n),
                            lax.broadcasted_iota(jnp.int32, (1, SUBLANES, 1), 1), SUBLANES, 1)
    a8 = a8.reshape(t, n)
    b8 = b8.reshape(t, n)
    for i in range(nt):
        ab_ref[i] = a8[:, i * LANES:(i + 1) * LANES]
        ab_ref[nt + i] = b8[:, i * LANES:(i + 1) * LANES]
    last = pl.ds(SUBLANES - 1, g, stride=SUBLANES)
    gidx = lax.broadcasted_iota(jnp.int32, (g, 1), 0)
    ag = jnp.concatenate([ab_ref[i, last, :] for i in range(nt)], axis=1)
    bg = jnp.concatenate([ab_ref[nt + i, last, :] for i in range(nt)], axis=1)
    ag, bg = _doubling_scan(ag, bg, gidx, g, 0)
    state = bg + ag * hc_ref[...]
    cg_ref[...] = jnp.where(gidx >= 1, pltpu.roll(state, 1, 0), hc_ref[...])
    hc_ref[...] = state[g - 1:g, :]
    out = []
    for i in range(nt):
        cg = cg_ref[:, i * LANES:(i + 1) * LANES].reshape(g, 1, LANES)
        hi = ab_ref[nt + i].reshape(g, SUBLANES, LANES) + ab_ref[i].reshape(g, SUBLANES, LANES) * cg
        out.append(hi.reshape(t, LANES))
    return jnp.concatenate(out, axis=1)


def _lru_kernel(bps, xn_ref, w_ref, cw_ref, cb_ref, wg_ref, bg_ref, lam_ref, o_ref, buf_ref, hc_ref,
                *scratch):
    nb = xn_ref.shape[0] // T_BLK
    _step_start(bps, nb, hc_ref, buf_ref)
    for blk in range(nb):
        _lru_block(blk, xn_ref, w_ref, cw_ref, cb_ref, wg_ref, bg_ref, lam_ref, o_ref, buf_ref, hc_ref,
                   *scratch[2 * blk:2 * blk + 2])
    _conv_carry(buf_ref)


def _lru_block(blk, xn_ref, w_ref, cw_ref, cb_ref, wg_ref, bg_ref, lam_ref, o_ref, buf_ref, hc_ref,
               ab_ref, cg_ref):
    t = T_BLK
    rows = slice(blk * t, (blk + 1) * t)
    w = BRANCH_WIDTH
    proj = _dot(xn_ref[rows, :], w_ref[...])
    u = _causal_conv(buf_ref, proj[:, 0:w], cw_ref, cb_ref, blk * t)
    rg = _dot(u.astype(BF16), wg_ref[...]) + bg_ref[...]
    r = _sigmoid(rg[:, 0:w])
    ig = _sigmoid(rg[:, w:2 * w])
    log_a = (-LRU_C) * r * _softplus(-lam_ref[...])
    a = jnp.exp(log_a)
    b = _sqrt_one_minus_exp2x(log_a) * (ig * u)
    hseq = _scan_rows(a, b, hc_ref, ab_ref, cg_ref)
    gate = proj[:, w:2 * w]
    gelu = 0.5 * gate * (1.0 + jnp.tanh(math.sqrt(2.0 / math.pi) * (gate + 0.044715 * gate * gate * gate)))
    o_ref[rows, :] = (hseq * gelu).astype(o_ref.dtype)


def _merge_kernel(h_ref, xn_ref, ya_ref, yb_ref, yc_ref, yd_ref, wmg_ref, wbr_ref, wout_ref, o_ref):
    xn = xn_ref[...]
    merged = None
    for n, y_ref in enumerate((ya_ref, yb_ref, yc_ref, yd_ref)):
        gate = _sigmoid(_dot(xn, wmg_ref[:, n * D_MODEL:(n + 1) * D_MODEL]))
        term = gate * _dot(y_ref[...], wbr_ref[n])
        merged = term if merged is None else merged + term
    o_ref[...] = h_ref[...] + _dot(merged.astype(BF16), wout_ref[...])


def _ffn_kernel(final, h_ref, nw_ref, win_ref, wout_ref, nnw_ref, o_ref, *xn_out):
    hres = h_ref[...]
    xn = _rmsnorm(hres, nw_ref[...]).astype(BF16)
    gu = _dot(xn, win_ref[...])
    act = (_silu(gu[:, 0:D_FF]) * gu[:, D_FF:2 * D_FF]).astype(BF16)
    out = hres + _dot(act, wout_ref[...])
    if final:
        o_ref[...] = _rmsnorm(out, nnw_ref[...])
    else:
        o_ref[...] = out
        xn_out[0][...] = _rmsnorm(out, nnw_ref[...]).astype(BF16)


def _norm_kernel(x_ref, nw_ref, o_ref):
    o_ref[...] = _rmsnorm(x_ref[...], nw_ref[...]).astype(o_ref.dtype)


def _const_spec(shape):
    nd = len(shape)
    return pl.BlockSpec(shape, lambda i, _nd=nd: (0,) * _nd, pipeline_mode=pl.Buffered(1))


def _row_spec(rows, width):
    return pl.BlockSpec((rows, width), lambda i: (i, 0))


class _LayerWindow:
    def __init__(self, array, layer, width=None, col=0):
        self.array, self.layer, self.width, self.col = array, layer, width or array.shape[-1], col

    def spec(self):
        shape = self.array.shape
        mid = (0,) * (len(shape) - 2)
        layer, col = self.layer, self.col
        return pl.BlockSpec((None,) + shape[1:-1] + (self.width,), lambda i: (layer,) + mid + (col,),
                            pipeline_mode=pl.Buffered(1))


def _pcall(kernel, name, n_tokens, rows, operands, row_flags, outs, scratch):
    in_specs = [a.spec() if isinstance(a, _LayerWindow) else
                (_row_spec(rows, a.shape[1]) if is_row else _const_spec(a.shape))
                for a, is_row in zip(operands, row_flags)]
    operands = [a.array if isinstance(a, _LayerWindow) else a for a in operands]
    many = isinstance(outs, list)
    out_specs = [_row_spec(rows, wd) for wd, _ in (outs if many else [outs])]
    out_shape = [jax.ShapeDtypeStruct((n_tokens, wd), dt) for wd, dt in (outs if many else [outs])]
    return pl.pallas_call(
        kernel,
        name=name,
        grid=(n_tokens // rows,),
        in_specs=in_specs,
        out_specs=out_specs if many else out_specs[0],
        out_shape=out_shape if many else out_shape[0],
        scratch_shapes=scratch,
        compiler_params=pltpu.CompilerParams(dimension_semantics=("arbitrary",), vmem_limit_bytes=VMEM_LIMIT),
    )(*operands)


def _tril_const(t):
    return jnp.asarray(np.tril(np.ones((t, t), np.float32)), BF16)


def _gla_scratch(nk, nb):
    ntile = nk // LANES
    per_block = [
        pltpu.VMEM((GLA_PARTS, T_BLK, nk), BF16),
        pltpu.VMEM((GLA_PARTS, T_BLK, nk), BF16),
        pltpu.VMEM((2 * T_BLK, nk), BF16),
        pltpu.VMEM((T_BLK, BRANCH_WIDTH), BF16),
    ]
    return [pltpu.VMEM((ntile, HEAD_V, LANES), F32)] + per_block * nb


def _row(x):
    return x.reshape(1, -1).astype(F32)


PACK_MERGE = (4096, 0)
PACK_HGRN = (2048, 2)
PACK_SSD = (2048, 3)
PACK_GLA = (2048, 4)
PACK_LRU = (1024, 10)


PACK_COLS = 11 * 1024
PACK_ROWS = 128
IN_COLS = [0] + np.cumsum([512, 512, 512, 512, 512, 768, 8, 256, 256, 512, 512, 16, 512, 512, 4096]).tolist()


def _pack_kernel(w_ref, wg_ref, o_ref):
    c = IN_COLS

    def put(dst, lo, hi):
        o_ref[:, dst:dst + hi - lo] = w_ref[:, lo:hi].astype(BF16)
        return dst + hi - lo

    def pad(lo, hi):
        o_ref[:, lo:hi] = jnp.zeros((o_ref.shape[0], hi - lo), BF16)

    put(PACK_MERGE[0] * PACK_MERGE[1], c[14], c[15])
    put(PACK_HGRN[0] * PACK_HGRN[1], c[0], c[4])
    start = PACK_SSD[0] * PACK_SSD[1]
    pad(put(start, c[4], c[7]), start + PACK_SSD[0])
    start = PACK_GLA[0] * PACK_GLA[1]
    o_ref[:, start:start + GLA_KDIM] = wg_ref[...].astype(BF16)
    pad(put(start + GLA_KDIM, c[7], c[11]), start + PACK_GLA[0])
    put(PACK_LRU[0] * PACK_LRU[1], c[12], c[14])


def _pack_in_proj(w_in, gla_gate_w):
    depth, rows, cols = w_in.shape
    w_gate = jnp.einsum("lkr,lrn->lkn", w_in[:, :, IN_COLS[11]:IN_COLS[12]].astype(F32), gla_gate_w.astype(F32),
                        precision=lax.Precision.HIGHEST)
    cols = pl.cdiv(cols, LANES) * LANES
    w_in = jnp.pad(w_in.astype(BF16), ((0, 0), (0, 0), (0, cols - w_in.shape[2])))
    return pl.pallas_call(
        _pack_kernel,
        name="pack",
        grid=(depth, rows // PACK_ROWS),
        in_specs=[pl.BlockSpec((None, PACK_ROWS, cols), lambda l, i: (l, i, 0)),
                  pl.BlockSpec((None, PACK_ROWS, GLA_KDIM), lambda l, i: (l, i, 0))],
        out_specs=pl.BlockSpec((None, PACK_ROWS, PACK_COLS), lambda l, i: (l, i, 0)),
        out_shape=jax.ShapeDtypeStruct((depth, rows, PACK_COLS), BF16),
        compiler_params=pltpu.CompilerParams(dimension_semantics=("arbitrary", "arbitrary"),
                                             vmem_limit_bytes=VMEM_LIMIT),
    )(w_in, w_gate)


def _layer(h, xn, seq, l, wts, p):
    n_tokens = h.shape[0]
    bps = seq // T_BLK
    assert n_tokens % seq == 0 and seq % T_STEP == 0
    assert all(seq % (nb * T_BLK) == 0 for nb in (HGRN_CHUNKS, GLA_CHUNKS, SSD_CHUNKS, LRU_CHUNKS))
    tril = _tril_const(T_BLK)
    dmat = _gla_decay_matrix(T_BLK)
    part_ids = _gla_part_ids()
    w_all, w_br, w_o, w_fi, w_fo = wts
    win = lambda pack: _LayerWindow(w_all, l, *pack)

    lb = p["lb"].astype(F32)
    ya = _pcall(
        functools.partial(_hgrn_kernel, bps), "hgrn2", n_tokens, HGRN_CHUNKS * T_BLK,
        [xn, win(PACK_HGRN), _row(lb), _row(1.0 - lb), _row(p["hgrn_norm_w"]), dmat, part_ids],
        [True] + [False] * 6, (BRANCH_WIDTH, BF16), _gla_scratch(BRANCH_WIDTH, HGRN_CHUNKS))

    pad8 = lambda v: jnp.concatenate([v.astype(F32), jnp.zeros((LANES - SSD_HEADS,), F32)]).reshape(1, LANES)
    expand = jnp.asarray(np.arange(LANES)[:, None] == (np.arange(BRANCH_WIDTH)[None, :] // SSD_HEADDIM), BF16)
    yb = _pcall(
        functools.partial(_ssd_kernel, bps), "ssd", n_tokens, SSD_CHUNKS * T_BLK,
        [xn, win(PACK_SSD), p["ssd_conv_w"].astype(F32), _row(p["ssd_conv_b"]), pad8(p["ssd_dt_bias"]),
         pad8(-LOG2E * jnp.exp(p["ssd_a_log"].astype(F32))), _row(jnp.repeat(p["ssd_d"], SSD_HEADDIM)), _row(p["ssd_norm_w"]),
         tril, expand],
        [True] + [False] * 9, (BRANCH_WIDTH, BF16),
        [pltpu.VMEM((SSD_CHUNKS * T_BLK + SUBLANES, SSD_CONV_DIM), F32),
         pltpu.VMEM((SSD_GROUPS * SSD_STATE, BRANCH_WIDTH), F32)]
        + [pltpu.VMEM((T_BLK, BRANCH_WIDTH), F32)] * SSD_CHUNKS)

    yc = _pcall(
        functools.partial(_gla_kernel, bps), "gla", n_tokens, GLA_CHUNKS * T_BLK,
        [xn, win(PACK_GLA), _row(p["gla_gate_b"]), _row(p["gla_norm_w"]), dmat, part_ids],
        [True] + [False] * 5, (BRANCH_WIDTH, BF16), _gla_scratch(GLA_KDIM, GLA_CHUNKS))

    eye = jnp.eye(LRU_BLOCKS, dtype=F32)
    bdiag = lambda wt: jnp.einsum("kde,kl->kdle", wt.astype(F32), eye).reshape(BRANCH_WIDTH, BRANCH_WIDTH)
    w_gates = jnp.concatenate([bdiag(p["lru_wa"]), bdiag(p["lru_wx"])], axis=1).astype(BF16)
    b_gates = _row(jnp.concatenate([p["lru_ba"], p["lru_bx"]]))
    yd = _pcall(
        functools.partial(_lru_kernel, bps), "rglru", n_tokens, LRU_CHUNKS * T_BLK,
        [xn, win(PACK_LRU), p["lru_conv_w"].astype(F32), _row(p["lru_conv_b"]), w_gates,
         b_gates, _row(p["lru_lambda"])],
        [True] + [False] * 6, (BRANCH_WIDTH, BF16),
        [pltpu.VMEM((LRU_CHUNKS * T_BLK + SUBLANES, BRANCH_WIDTH), F32), pltpu.VMEM((1, BRANCH_WIDTH), F32)]
        + [pltpu.VMEM((2 * BRANCH_WIDTH // LANES, T_BLK, LANES), F32),
           pltpu.VMEM((T_BLK // SUBLANES, BRANCH_WIDTH), F32)] * LRU_CHUNKS)

    h = _pcall(
        _merge_kernel, "merge", n_tokens, T_STEP,
        [h, xn, ya, yb, yc, yd, win(PACK_MERGE), _LayerWindow(w_br, l), _LayerWindow(w_o, l)],
        [True, True, True, True, True, True, False, False, False], (D_MODEL, F32), [])

    outs = (D_MODEL, F32) if p["final"] else [(D_MODEL, F32), (D_MODEL, BF16)]
    res = _pcall(
        functools.partial(_ffn_kernel, p["final"]), "swiglu", n_tokens, T_STEP,
        [h, _row(p["norm_ffn_w"]), _LayerWindow(w_fi, l), _LayerWindow(w_fo, l), _row(p["next_norm_w"])],
        [True, False, False, False, False], outs, [])
    return (res, None) if p["final"] else tuple(res)


def kernel(x, norm_mix_w, w_in, hgrn_lower_bounds, hgrn_norm_w, ssd_conv_w, ssd_conv_b, ssd_dt_bias, ssd_a_log, ssd_d, ssd_norm_w, gla_gate_w, gla_gate_b, gla_norm_w, lru_conv_w, lru_conv_b, lru_wa, lru_ba, lru_wx, lru_bx, lru_lambda, w_branch, w_out, norm_ffn_w, w_ffn_in, w_ffn_out, norm_f_w):
    bsz, seq, _ = x.shape
    depth = w_in.shape[0]
    assert (bsz * seq) % NORM_ROWS == 0
    lb_all = jnp.cumsum(jax.nn.softmax(hgrn_lower_bounds.astype(F32), axis=0), axis=0)
    lb_all = lb_all - lb_all[0:1]
    wts = (_pack_in_proj(w_in, gla_gate_w), w_branch.astype(BF16), w_out.astype(BF16), w_ffn_in.astype(BF16),
           w_ffn_out.astype(BF16))
    h = x.reshape(bsz * seq, D_MODEL)
    xn = _pcall(_norm_kernel, "norm", bsz * seq, NORM_ROWS, [h, _row(norm_mix_w[0])], [True, False], (D_MODEL, BF16), [])
    for l in range(depth):
        p = dict(
            lb=lb_all[l], hgrn_norm_w=hgrn_norm_w[l],
            ssd_conv_w=ssd_conv_w[l], ssd_conv_b=ssd_conv_b[l], ssd_dt_bias=ssd_dt_bias[l], ssd_a_log=ssd_a_log[l],
            ssd_d=ssd_d[l], ssd_norm_w=ssd_norm_w[l], gla_gate_b=gla_gate_b[l],
            gla_norm_w=gla_norm_w[l], lru_conv_w=lru_conv_w[l], lru_conv_b=lru_conv_b[l], lru_wa=lru_wa[l],
            lru_ba=lru_ba[l], lru_wx=lru_wx[l], lru_bx=lru_bx[l], lru_lambda=lru_lambda[l],
            norm_ffn_w=norm_ffn_w[l],
            next_norm_w=(norm_f_w if l == depth - 1 else norm_mix_w[l + 1]), final=(l == depth - 1))
        h, xn = _layer(h, xn, seq, l, wts, p)
    return h.reshape(bsz, seq, D_MODEL)
```
